```python
import jax, jax.numpy as jnp
from jax import lax
import numpy as np


D_MODEL = 2048
BATCH = 1
SEQ = 16384
DEPTH = 2

HEAD_DIM = 128
FOX_WIDTH = D_MODEL // 2
MLSTM_WIDTH = D_MODEL // 2
FOX_HEADS = FOX_WIDTH // HEAD_DIM
MLSTM_HEADS = MLSTM_WIDTH // HEAD_DIM
Q_BLOCK = 128
MLSTM_CHUNK = 64
GATE_SOFTCAP = 15.0
POOL_WINDOWS = (2, 4, 8, 16)
N_POOL_GROUPS = len(POOL_WINDOWS)
POOL_GROUP = D_MODEL // N_POOL_GROUPS
N_GROUPS = 4
EXPERTS_PER_GROUP = 8
N_EXPERTS = N_GROUPS * EXPERTS_PER_GROUP
TOP_K = 2
D_EXPERT = D_MODEL // 2
MOE_BLOCK = 128
RMS_EPS = 1e-6
N_EVEN = (DEPTH + 1) // 2
N_ODD = DEPTH // 2
IN_COLS = 3 * FOX_WIDTH + FOX_HEADS + 3 * MLSTM_WIDTH + 2 * MLSTM_HEADS + MLSTM_WIDTH

kernel_name = 'fox_mlstm_pool_hmoe_hybrid'


def rms_norm(x, g):
    xf = x.astype(jnp.float32)
    y = xf * lax.rsqrt(jnp.mean(xf * xf, axis=-1, keepdims=True) + RMS_EPS)
    return (y * g.astype(jnp.float32)).astype(x.dtype)


def soft_cap(x):
    return GATE_SOFTCAP * jnp.tanh(x / GATE_SOFTCAP)


def forgetting_attention(q, k, v, log_f):
    B, S, H, Dh = q.shape
    nb = S // Q_BLOCK
    scale = Dh ** -0.5
    c = jnp.cumsum(log_f, axis=1)
    c_keys = c.transpose(0, 2, 1)
    qb = jnp.moveaxis(q.reshape(B, nb, Q_BLOCK, H, Dh), 1, 0)
    cb = jnp.moveaxis(c.reshape(B, nb, Q_BLOCK, H), 1, 0)
    kpos = jnp.arange(S)

    def block(args):
        i, q_i, c_i = args
        s = jnp.einsum('bqhd,bkhd->bhqk', q_i, k, preferred_element_type=jnp.float32) * scale
        s = s + c_i.transpose(0, 2, 1)[..., None] - c_keys[:, :, None, :]
        qpos = i * Q_BLOCK + jnp.arange(Q_BLOCK)
        s = jnp.where(kpos[None, :] <= qpos[:, None], s, -jnp.inf)
        p = jax.nn.softmax(s, axis=-1)
        return jnp.einsum('bhqk,bkhd->bqhd', p.astype(v.dtype), v)

    out = lax.map(block, (jnp.arange(nb), qb, cb))
    return jnp.moveaxis(out, 0, 1).reshape(B, S, H, Dh)


def mlstm_chunkwise(q, k, v, i_pre, log_f):
    B, S, H, Dh = q.shape
    L = MLSTM_CHUNK
    nc = S // L

    def to_chunks(a):
        a = a.reshape((B, nc, L, H) + a.shape[3:])
        return jnp.swapaxes(jnp.moveaxis(a, 1, 0), 2, 3)

    qc = to_chunks(q.astype(jnp.float32))
    kc = to_chunks(k.astype(jnp.float32) * (Dh ** -0.5))
    vc = to_chunks(v.astype(jnp.float32))
    ic = to_chunks(i_pre)
    fc = to_chunks(log_f)
    causal = jnp.tril(jnp.ones((L, L), dtype=bool))

    def step(carry, xs):
        C, n, m = carry
        q_c, k_c, v_c, i_c, f_c = xs
        b = jnp.cumsum(f_c, axis=-1)
        log_intra = b[..., :, None] - b[..., None, :] + i_c[..., None, :]
        log_intra = jnp.where(causal, log_intra, -jnp.inf)
        log_inter = b + m[..., None]
        m_t = jnp.maximum(log_inter, jnp.max(log_intra, axis=-1))
        w_intra = jnp.exp(log_intra - m_t[..., None])
        w_inter = jnp.exp(log_inter - m_t)
        qk = jnp.einsum('bhtd,bhsd->bhts', q_c, k_c) * w_intra
        num = (jnp.einsum('bhts,bhsv->bhtv', qk, v_c)
               + w_inter[..., None] * jnp.einsum('bhtd,bhdv->bhtv', q_c, C))
        den = jnp.sum(qk, axis=-1) + w_inter * jnp.einsum('bhtd,bhd->bht', q_c, n)
        h = num / jnp.maximum(jnp.abs(den), jnp.exp(-m_t))[..., None]
        b_last = b[..., -1]
        log_w_state = b_last[..., None] - b + i_c
        m_new = jnp.maximum(b_last + m, jnp.max(log_w_state, axis=-1))
        decay = jnp.exp(b_last + m - m_new)
        w_s = jnp.exp(log_w_state - m_new[..., None])
        C_new = decay[..., None, None] * C + jnp.einsum('bhs,bhsd,bhsv->bhdv', w_s, k_c, v_c)
        n_new = decay[..., None] * n + jnp.einsum('bhs,bhsd->bhd', w_s, k_c)
        return (C_new, n_new, m_new), h

    init = (jnp.zeros((B, H, Dh, Dh), jnp.float32),
            jnp.zeros((B, H, Dh), jnp.float32),
            jnp.zeros((B, H), jnp.float32))
    _, hs = lax.scan(step, init, (qc, kc, vc, ic, fc))
    hs = jnp.moveaxis(jnp.swapaxes(hs, 2, 3), 0, 1)
    return hs.reshape(B, S, H, Dh)


def fox_mlstm_mixer(h, w_in, fox_f_bias, fox_q_gain, fox_k_gain,
                    mlstm_i_bias, mlstm_f_bias, mlstm_out_gain, w_out):
    B, S, _ = h.shape
    proj = jnp.einsum('bsd,dc->bsc', h, w_in)
    sizes = ([FOX_WIDTH] * 3 + [FOX_HEADS] + [MLSTM_WIDTH] * 3
             + [MLSTM_HEADS] * 2 + [MLSTM_WIDTH])
    cuts = np.cumsum(sizes)[:-1].tolist()
    fq, fk, fv, ff, mq, mk, mv, mi, mf, mo = jnp.split(proj, cuts, axis=-1)

    def heads(a, n_heads):
        return a.reshape(B, S, n_heads, HEAD_DIM)

    fq = rms_norm(heads(fq, FOX_HEADS), fox_q_gain)
    fk = rms_norm(heads(fk, FOX_HEADS), fox_k_gain)
    fox_log_f = jax.nn.log_sigmoid(ff.astype(jnp.float32) + fox_f_bias)
    y_fox = forgetting_attention(fq, fk, heads(fv, FOX_HEADS), fox_log_f)
    y_fox = y_fox.reshape(B, S, FOX_WIDTH).astype(h.dtype)

    i_pre = soft_cap(mi.astype(jnp.float32) + mlstm_i_bias)
    m_log_f = jax.nn.log_sigmoid(soft_cap(mf.astype(jnp.float32) + mlstm_f_bias))
    hm = mlstm_chunkwise(heads(mq, MLSTM_HEADS), heads(mk, MLSTM_HEADS),
                         heads(mv, MLSTM_HEADS), i_pre, m_log_f)
    hm = rms_norm(hm, mlstm_out_gain.reshape(MLSTM_HEADS, HEAD_DIM)).reshape(B, S, MLSTM_WIDTH)
    y_mlstm = (jax.nn.sigmoid(mo.astype(jnp.float32)) * hm).astype(h.dtype)

    y = jnp.concatenate([y_fox, y_mlstm], axis=-1)
    return jnp.einsum('bsc,cd->bsd', y, w_out)


def multiscale_pool_mixer(h, pool_w, pool_b, pool_scale):
    B, S, _ = h.shape
    hf = h.astype(jnp.float32)
    cs = jnp.pad(jnp.cumsum(hf, axis=1), ((0, 0), (1, 0), (0, 0)))
    t = jnp.arange(S)
    outs = []
    for g, w in enumerate(POOL_WINDOWS):
        lo, hi = g * POOL_GROUP, (g + 1) * POOL_GROUP
        c = cs[:, :, lo:hi]
        start = jnp.maximum(t + 1 - w, 0)
        window_sum = c[:, 1:] - c[:, start]
        count = (t + 1 - start).astype(jnp.float32)
        pooled = window_sum / count[None, :, None] - hf[:, :, lo:hi]
        outs.append(jnp.einsum('bsc,ce->bse', pooled.astype(h.dtype), pool_w[g]) + pool_b[g])
    return jnp.concatenate(outs, axis=-1) * pool_scale


def hierarchical_moe(h, router_group_w, router_group_b, router_expert_w, router_expert_b,
                     w_gate, w_up, w_down):
    B, S, D = h.shape
    T = B * S
    xt = h.reshape(T, D)
    g_logits = jnp.einsum('td,dg->tg', xt, router_group_w).astype(jnp.float32) + router_group_b
    g_prob = jax.nn.softmax(g_logits, axis=-1)
    g_top, g_sel = lax.top_k(g_logits, 1)
    g_sel = g_sel[:, 0]
    g_w = jnp.take_along_axis(g_prob, g_sel[:, None], axis=-1)[:, 0]
    e_logits = jnp.einsum('td,de->te', xt, router_expert_w).astype(jnp.float32) + router_expert_b
    e_logits = e_logits.reshape(T, N_GROUPS, EXPERTS_PER_GROUP)
    e_in = jnp.take_along_axis(e_logits, g_sel[:, None, None], axis=1)[:, 0]
    top_v, top_i = lax.top_k(e_in, TOP_K)
    combine = g_w[:, None] * jax.nn.softmax(top_v, axis=-1)
    expert_id = g_sel[:, None] * EXPERTS_PER_GROUP + top_i

    n_assign = T * TOP_K
    flat_e = expert_id.reshape(-1)
    flat_tok = jnp.repeat(jnp.arange(T, dtype=jnp.int32), TOP_K)
    flat_w = combine.reshape(-1)
    order = jnp.argsort(flat_e)
    sorted_e = flat_e[order]
    counts = jnp.bincount(flat_e, length=N_EXPERTS)
    padded = (counts + MOE_BLOCK - 1) // MOE_BLOCK * MOE_BLOCK
    starts = jnp.cumsum(counts) - counts
    pends = jnp.cumsum(padded)
    pstarts = pends - padded
    dest = pstarts[sorted_e] + (jnp.arange(n_assign) - starts[sorted_e])
    n_rows = (n_assign + N_EXPERTS * (MOE_BLOCK - 1) + MOE_BLOCK - 1) // MOE_BLOCK * MOE_BLOCK
    n_blocks = n_rows // MOE_BLOCK
    row_tok = jnp.full((n_rows,), T, jnp.int32).at[dest].set(flat_tok[order])
    row_w = jnp.zeros((n_rows,), jnp.float32).at[dest].set(flat_w[order])
    block_e = jnp.clip(jnp.searchsorted(pends, jnp.arange(n_blocks) * MOE_BLOCK, side='right'),
                       0, N_EXPERTS - 1)
    x_pad = jnp.concatenate([xt, jnp.zeros((1, D), xt.dtype)], axis=0)
    xb = x_pad[row_tok].reshape(n_blocks, MOE_BLOCK, D)

    def expert_block(args):
        x_blk, e = args
        a = x_blk @ w_gate[e]
        u = x_blk @ w_up[e]
        return (jax.nn.silu(a) * u) @ w_down[e]

    yb = lax.map(expert_block, (xb, block_e)).reshape(n_rows, D)
    y = jnp.zeros((T + 1, D), jnp.float32).at[row_tok].add(yb.astype(jnp.float32) * row_w[:, None])
    return y[:T].astype(h.dtype).reshape(B, S, D)


def setup_inputs(seed: int = 0) -> dict:
    key = jax.random.key(seed)
    ks = jax.random.split(key, 24)
    f32 = jnp.float32
    D = D_MODEL
    MIX = FOX_WIDTH + MLSTM_WIDTH

    def nrm(k, shape, scale):
        return jax.random.normal(k, shape, f32) * scale

    return {
        'x': nrm(ks[0], (BATCH, SEQ, D), 1.0),
        'norm_mix': 1.0 + nrm(ks[1], (DEPTH, D), 0.05),
        'norm_ffn': 1.0 + nrm(ks[2], (DEPTH, D), 0.05),
        'w_in': nrm(ks[3], (N_EVEN, D, IN_COLS), D ** -0.5),
        'fox_f_bias': jax.random.uniform(ks[4], (N_EVEN, FOX_HEADS), f32, 2.0, 5.0),
        'fox_q_gain': 1.0 + nrm(ks[5], (N_EVEN, HEAD_DIM), 0.05),
        'fox_k_gain': 1.0 + nrm(ks[6], (N_EVEN, HEAD_DIM), 0.05),
        'mlstm_i_bias': nrm(ks[7], (N_EVEN, MLSTM_HEADS), 0.1) - 1.0,
        'mlstm_f_bias': jax.random.uniform(ks[8], (N_EVEN, MLSTM_HEADS), f32, 3.0, 6.0),
        'mlstm_out_gain': 1.0 + nrm(ks[9], (N_EVEN, MLSTM_WIDTH), 0.05),
        'w_out': nrm(ks[10], (N_EVEN, MIX, D), MIX ** -0.5),
        'pool_w': nrm(ks[11], (N_ODD, N_POOL_GROUPS, POOL_GROUP, POOL_GROUP), POOL_GROUP ** -0.5),
        'pool_b': nrm(ks[12], (N_ODD, N_POOL_GROUPS, POOL_GROUP), 0.01),
        'pool_scale': 1.0 + nrm(ks[13], (N_ODD, D), 0.1),
        'router_group_w': nrm(ks[14], (DEPTH, D, N_GROUPS), D ** -0.5),
        'router_group_b': nrm(ks[15], (DEPTH, N_GROUPS), 0.01),
        'router_expert_w': nrm(ks[16], (DEPTH, D, N_EXPERTS), D ** -0.5),
        'router_expert_b': nrm(ks[17], (DEPTH, N_EXPERTS), 0.01),
        'w_gate': nrm(ks[18], (DEPTH, N_EXPERTS, D, D_EXPERT), D ** -0.5),
        'w_up': nrm(ks[19], (DEPTH, N_EXPERTS, D, D_EXPERT), D ** -0.5),
        'w_down': nrm(ks[20], (DEPTH, N_EXPERTS, D_EXPERT, D), D_EXPERT ** -0.5),
    }


def reference(x, norm_mix, norm_ffn, w_in, fox_f_bias, fox_q_gain, fox_k_gain,
              mlstm_i_bias, mlstm_f_bias, mlstm_out_gain, w_out,
              pool_w, pool_b, pool_scale,
              router_group_w, router_group_b, router_expert_w, router_expert_b,
              w_gate, w_up, w_down):
    for layer in range(DEPTH):
        hn = rms_norm(x, norm_mix[layer])
        j = layer // 2
        if layer % 2 == 0:
            mix = fox_mlstm_mixer(hn, w_in[j], fox_f_bias[j], fox_q_gain[j], fox_k_gain[j],
                                  mlstm_i_bias[j], mlstm_f_bias[j], mlstm_out_gain[j], w_out[j])
        else:
            mix = multiscale_pool_mixer(hn, pool_w[j], pool_b[j], pool_scale[j])
        x = x + mix.astype(x.dtype)
        ffn = hierarchical_moe(rms_norm(x, norm_ffn[layer]),
                               router_group_w[layer], router_group_b[layer],
                               router_expert_w[layer], router_expert_b[layer],
                               w_gate[layer], w_up[layer], w_down[layer])
        x = x + ffn.astype(x.dtype)
    return x
```

```python
import functools

import jax
import jax.numpy as jnp
from jax import lax
from jax.experimental import pallas as pl
from jax.experimental.pallas import tpu as pltpu

F32 = jnp.float32
BF16 = jnp.bfloat16

HEAD_DIM = 128
GATE_SOFTCAP = 15.0
POOL_WINDOWS = (2, 4, 8, 16)
POOL_HALO = 16
N_GROUPS = 4
EXPERTS_PER_GROUP = 8
N_EXPERTS = N_GROUPS * EXPERTS_PER_GROUP
TOP_K = 2
RMS_EPS = 1e-6

V7X_LANES = 128
V7X_VMEM_BYTES = 64 * 1024 * 1024

NORM_ROWS = 512
MM_ROWS = 1024
MM_COLS = 1024
ATT_Q = 512
ATT_K = 512
MLSTM_CHUNK = 256
MOE_ROWS = 256
NEG_BIG = -1e30


def _vmem_limit(nbytes):
    return int(min(max(nbytes * 3 // 2, 16 * 1024 * 1024), V7X_VMEM_BYTES - 8 * 1024 * 1024))


def _rms(x, eps=RMS_EPS):
    return x * lax.rsqrt(jnp.mean(x * x, axis=-1, keepdims=True) + eps)


def _log_sigmoid(x):
    return -(jnp.maximum(-x, 0.0) + jnp.log1p(jnp.exp(-jnp.abs(x))))


def _rmsnorm_kernel(x_ref, g_ref, o_ref):
    o_ref[...] = (_rms(x_ref[...]) * g_ref[...]).astype(o_ref.dtype)


def rmsnorm_bf16(x, g):
    S, D = x.shape
    return pl.pallas_call(
        _rmsnorm_kernel,
        out_shape=jax.ShapeDtypeStruct((S, D), BF16),
        grid=(S // NORM_ROWS,),
        in_specs=[pl.BlockSpec((NORM_ROWS, D), lambda i: (i, 0)),
                  pl.BlockSpec((1, D), lambda i: (0, 0))],
        out_specs=pl.BlockSpec((NORM_ROWS, D), lambda i: (i, 0)),
        compiler_params=pltpu.CompilerParams(
            dimension_semantics=("parallel",),
            vmem_limit_bytes=_vmem_limit(2 * NORM_ROWS * D * 6)),
        name="rmsnorm",
    )(x, g.reshape(1, D))


def _inproj_main_kernel(a_ref, w_ref, gain_ref, o_ref, *, n_heads_per_tile):
    j = pl.program_id(1)
    acc = jnp.dot(a_ref[...], w_ref[...], preferred_element_type=F32)

    @pl.when(j < 2)
    def _():
        g = gain_ref[pl.ds(j, 1), :]
        for h in range(n_heads_per_tile):
            a = acc[:, h * HEAD_DIM:(h + 1) * HEAD_DIM]
            o_ref[:, h * HEAD_DIM:(h + 1) * HEAD_DIM] = (_rms(a) * g).astype(o_ref.dtype)

    @pl.when(j == 4)
    def _():
        o_ref[...] = (acc * (HEAD_DIM ** -0.5)).astype(o_ref.dtype)

    @pl.when(jnp.logical_and(j >= 2, j != 4))
    def _():
        o_ref[...] = acc.astype(o_ref.dtype)


def inproj_main(hn, w_main, gains):
    S, D = hn.shape
    N = w_main.shape[1]
    tm, tn = min(MM_ROWS, S), MM_COLS
    return pl.pallas_call(
        functools.partial(_inproj_main_kernel, n_heads_per_tile=tn // HEAD_DIM),
        out_shape=jax.ShapeDtypeStruct((S, N), BF16),
        grid=(S // tm, N // tn),
        in_specs=[pl.BlockSpec((tm, D), lambda i, j: (i, 0)),
                  pl.BlockSpec((D, tn), lambda i, j: (0, j)),
                  pl.BlockSpec((8, HEAD_DIM), lambda i, j: (0, 0))],
        out_specs=pl.BlockSpec((tm, tn), lambda i, j: (i, j)),
        compiler_params=pltpu.CompilerParams(
            dimension_semantics=("parallel", "parallel"),
            vmem_limit_bytes=_vmem_limit(2 * (tm * D * 2 + D * tn * 2 + tm * tn * 2) + 2 * tm * tn * 4)),
        name="inproj_main",
    )(hn, w_main, gains)


def _inproj_aux_kernel(a_ref, w_ref, mo_ref, gate_ref):
    acc = jnp.dot(a_ref[...], w_ref[...], preferred_element_type=F32)
    n_mo = mo_ref.shape[1]
    mo_ref[...] = acc[:, :n_mo]
    gate_ref[...] = acc[:, n_mo:]


def inproj_aux(hn, w_aux, n_mo):
    S, D = hn.shape
    N = w_aux.shape[1]
    tm = min(NORM_ROWS, S)
    return pl.pallas_call(
        _inproj_aux_kernel,
        out_shape=(jax.ShapeDtypeStruct((S, n_mo), F32),
                   jax.ShapeDtypeStruct((S, N - n_mo), F32)),
        grid=(S // tm,),
        in_specs=[pl.BlockSpec((tm, D), lambda i: (i, 0)),
                  pl.BlockSpec((D, N), lambda i: (0, 0))],
        out_specs=(pl.BlockSpec((tm, n_mo), lambda i: (i, 0)),
                   pl.BlockSpec((tm, N - n_mo), lambda i: (i, 0))),
        compiler_params=pltpu.CompilerParams(
            dimension_semantics=("parallel",),
            vmem_limit_bytes=_vmem_limit(2 * (tm * D * 2 + D * N * 2 + tm * N * 4) + tm * N * 4)),
        name="inproj_aux",
    )(hn, w_aux)


def _split3_dot(tri, val):
    v1 = val.astype(BF16)
    r1 = val - v1.astype(F32)
    v2 = r1.astype(BF16)
    v3 = (r1 - v2.astype(F32)).astype(BF16)
    out = jnp.dot(tri, v1, preferred_element_type=F32)
    out += jnp.dot(tri, v2, preferred_element_type=F32)
    out += jnp.dot(tri, v3, preferred_element_type=F32)
    return out


def _gates_kernel(g_ref, bias_ref, o_ref, carry_ref, *, n_heads):
    @pl.when(pl.program_id(0) == 0)
    def _():
        carry_ref[...] = jnp.zeros_like(carry_ref)

    rows = g_ref.shape[0]
    z = g_ref[...] + bias_ref[...]
    lane = lax.broadcasted_iota(jnp.int32, z.shape, 1)
    capped = GATE_SOFTCAP * jnp.tanh(z / GATE_SOFTCAP)
    is_fox = lane < n_heads
    is_i = jnp.logical_and(lane >= n_heads, lane < 2 * n_heads)
    is_f = jnp.logical_and(lane >= 2 * n_heads, lane < 3 * n_heads)
    logf = jnp.where(is_fox, _log_sigmoid(z), jnp.where(is_f, _log_sigmoid(capped), 0.0))
    r = lax.broadcasted_iota(jnp.int32, (rows, rows), 0)
    c = lax.broadcasted_iota(jnp.int32, (rows, rows), 1)
    tri = jnp.where(r >= c, 1.0, 0.0).astype(BF16)
    cum = _split3_dot(tri, logf)
    glob = cum + carry_ref[...]
    o_ref[...] = jnp.where(is_fox, glob, jnp.where(is_i, capped, cum))
    carry_ref[...] = glob[rows - 1:rows, :]


def gate_activations(gates_pre, bias_row, n_heads):
    S, W = gates_pre.shape
    tb = MLSTM_CHUNK
    return pl.pallas_call(
        functools.partial(_gates_kernel, n_heads=n_heads),
        out_shape=jax.ShapeDtypeStruct((S, W), F32),
        grid=(S // tb,),
        in_specs=[pl.BlockSpec((tb, W), lambda i: (i, 0)),
                  pl.BlockSpec((1, W), lambda i: (0, 0))],
        out_specs=pl.BlockSpec((tb, W), lambda i: (i, 0)),
        scratch_shapes=[pltpu.VMEM((1, W), F32)],
        compiler_params=pltpu.CompilerParams(dimension_semantics=("arbitrary",)),
        name="gate_activations",
    )(gates_pre, bias_row)


def _fox_kernel(q_ref, k_ref, v_ref, cq_ref, ck_ref, o_ref, m_ref, l_ref, acc_ref):
    i = pl.program_id(1)
    tq = q_ref.shape[0]
    tk = tq
    q = q_ref[...]
    cq = cq_ref[...]
    cq_wide = jnp.concatenate([cq] * (tk // V7X_LANES), axis=1)

    m_ref[...] = jnp.full_like(m_ref, NEG_BIG)
    l_ref[...] = jnp.zeros_like(l_ref)
    acc_ref[...] = jnp.zeros_like(acc_ref)

    def block(j, masked):
        start = pl.multiple_of(j * tk, tk)
        k = k_ref[pl.ds(start, tk), :]
        v = v_ref[pl.ds(start, tk), :]
        s = lax.dot_general(q, k, (((1,), (1,)), ((), ())), preferred_element_type=F32)
        s = s + cq_wide - ck_ref[:, pl.ds(start, tk)]
        if masked:
            row = lax.broadcasted_iota(jnp.int32, s.shape, 0)
            col = lax.broadcasted_iota(jnp.int32, s.shape, 1)
            s = jnp.where(col <= row, s, NEG_BIG)
        m_prev = m_ref[...]
        m_new = jnp.maximum(m_prev, jnp.max(s, axis=-1, keepdims=True))
        p = jnp.exp(s - m_new)
        alpha = jnp.exp(m_prev - m_new)
        l_ref[...] = alpha * l_ref[...] + jnp.sum(p, axis=-1, keepdims=True)
        acc_ref[...] = alpha * acc_ref[...] + jnp.dot(p.astype(v.dtype), v, preferred_element_type=F32)
        m_ref[...] = m_new

    def body(j, carry):
        block(j, masked=False)
        return carry

    lax.fori_loop(0, i, body, 0)
    block(i, masked=True)
    o_ref[...] = (acc_ref[...] / l_ref[...]).astype(o_ref.dtype)


def fox_attention(proj, cq, ck, n_heads):
    S = proj.shape[0]
    tq = min(ATT_Q, S)
    assert tq == ATT_K or S < ATT_K
    H = n_heads
    return pl.pallas_call(
        _fox_kernel,
        out_shape=jax.ShapeDtypeStruct((S, H * HEAD_DIM), BF16),
        grid=(H, S // tq),
        in_specs=[pl.BlockSpec((tq, HEAD_DIM), lambda h, i: (i, h)),
                  pl.BlockSpec((S, HEAD_DIM), lambda h, i: (0, H + h)),
                  pl.BlockSpec((S, HEAD_DIM), lambda h, i: (0, 2 * H + h)),
                  pl.BlockSpec((None, tq, V7X_LANES), lambda h, i: (h, i, 0)),
                  pl.BlockSpec((None, 1, S), lambda h, i: (h, 0, 0))],
        out_specs=pl.BlockSpec((tq, HEAD_DIM), lambda h, i: (i, h)),
        scratch_shapes=[pltpu.VMEM((tq, 1), F32), pltpu.VMEM((tq, 1), F32),
                        pltpu.VMEM((tq, HEAD_DIM), F32)],
        compiler_params=pltpu.CompilerParams(
            dimension_semantics=("parallel", "arbitrary"),
            vmem_limit_bytes=_vmem_limit(4 * S * HEAD_DIM * 2 + 8 * tq * ATT_K * 4)),
        name="fox_attention",
    )(proj, proj, proj, cq, ck)


def _mlstm_kernel(q_ref, k_ref, v_ref, mo_ref, gcol_ref, grow_ref, gain_ref, o_ref,
                  state_ref, m_ref, *, n_heads):
    L = q_ref.shape[0]
    d = HEAD_DIM

    @pl.when(pl.program_id(0) == 0)
    def _():
        state_ref[...] = jnp.zeros_like(state_ref)
        m_ref[...] = jnp.zeros_like(m_ref)

    row = lax.broadcasted_iota(jnp.int32, (L, L), 0)
    col = lax.broadcasted_iota(jnp.int32, (L, L), 1)
    causal = col <= row
    lane = lax.broadcasted_iota(jnp.int32, (L, d), 1)
    ones_col = jnp.where(lane == 0, 1.0, 0.0).astype(BF16)

    gcol = gcol_ref[...]
    grow = grow_ref[...]
    for h in range(n_heads):
        sl = slice(h * d, (h + 1) * d)
        q = q_ref[:, sl]
        k = k_ref[:, sl]
        v = v_ref[:, sl]
        i_col = gcol[:, n_heads + h:n_heads + h + 1]
        b_col = gcol[:, 2 * n_heads + h:2 * n_heads + h + 1]
        i_row = grow[n_heads + h:n_heads + h + 1, :]
        b_row = grow[2 * n_heads + h:2 * n_heads + h + 1, :]
        m_prev = m_ref[h:h + 1, 0:1]
        state = state_ref[h]

        log_intra = jnp.where(causal, b_col - b_row + i_row, NEG_BIG)
        log_inter = b_col + m_prev
        m_t = jnp.maximum(log_inter, jnp.max(log_intra, axis=-1, keepdims=True))
        w_intra = jnp.exp(log_intra - m_t)
        w_inter = jnp.exp(log_inter - m_t)
        qk = lax.dot_general(q, k, (((1,), (1,)), ((), ())), preferred_element_type=F32) * w_intra
        v_aug = jnp.concatenate([v, ones_col], axis=1)
        tot = jnp.dot(qk.astype(BF16), v_aug, preferred_element_type=F32)
        tot = tot + w_inter * jnp.dot(q, state.astype(BF16), preferred_element_type=F32)
        num = tot[:, :d]
        den = tot[:, d:d + 1]
        hval = num / jnp.maximum(jnp.abs(den), jnp.exp(-m_t))

        b_last = b_col[L - 1:L, :]
        log_w_state = b_last - b_col + i_col
        m_new = jnp.maximum(b_last + m_prev, jnp.max(log_w_state, axis=0, keepdims=True))
        decay = jnp.exp(b_last + m_prev - m_new)
        w_s = jnp.exp(log_w_state - m_new)
        wv = (w_s * v_aug.astype(F32)).astype(BF16)
        upd = lax.dot_general(k, wv, (((0,), (0,)), ((), ())), preferred_element_type=F32)
        state_ref[h] = decay * state + upd
        m_ref[h:h + 1, :] = jnp.broadcast_to(m_new, (1, m_ref.shape[1]))

        hn = _rms(hval) * gain_ref[:, sl]
        o_ref[:, sl] = (jax.nn.sigmoid(mo_ref[:, sl]) * hn).astype(o_ref.dtype)


def mlstm_mixer(proj, mo, gcol, grow, out_gain, n_heads, q_block):
    S = proj.shape[0]
    L = min(MLSTM_CHUNK, S)
    W = n_heads * HEAD_DIM
    return pl.pallas_call(
        functools.partial(_mlstm_kernel, n_heads=n_heads),
        out_shape=jax.ShapeDtypeStruct((S, W), BF16),
        grid=(S // L,),
        in_specs=[pl.BlockSpec((L, W), lambda c: (c, q_block)),
                  pl.BlockSpec((L, W), lambda c: (c, q_block + 1)),
                  pl.BlockSpec((L, W), lambda c: (c, q_block + 2)),
                  pl.BlockSpec((L, W), lambda c: (c, 0)),
                  pl.BlockSpec((L, gcol.shape[1]), lambda c: (c, 0)),
                  pl.BlockSpec((grow.shape[0], L), lambda c: (0, c)),
                  pl.BlockSpec((1, W), lambda c: (0, 0))],
        out_specs=pl.BlockSpec((L, W), lambda c: (c, 0)),
        scratch_shapes=[pltpu.VMEM((n_heads, HEAD_DIM, 2 * HEAD_DIM), F32),
                        pltpu.VMEM((n_heads, V7X_LANES), F32)],
        compiler_params=pltpu.CompilerParams(
            dimension_semantics=("arbitrary",),
            vmem_limit_bytes=_vmem_limit(2 * L * W * (3 * 2 + 4 + 2) + 16 * L * L * 4)),
        name="mlstm",
    )(proj, proj, proj, mo, gcol, grow, out_gain.reshape(1, W))


def _outproj_kernel(a1_ref, a2_ref, w_ref, x_ref, o_ref):
    k1 = a1_ref.shape[1]
    acc = jnp.dot(a1_ref[...], w_ref[:k1, :], preferred_element_type=F32)
    acc += jnp.dot(a2_ref[...], w_ref[k1:, :], preferred_element_type=F32)
    o_ref[...] = x_ref[...] + acc


def outproj_residual(a1, a2, w, x):
    S, K1 = a1.shape
    K2 = a2.shape[1]
    N = w.shape[1]
    tm, tn = min(MM_ROWS, S), MM_COLS
    return pl.pallas_call(
        _outproj_kernel,
        out_shape=jax.ShapeDtypeStruct((S, N), F32),
        grid=(S // tm, N // tn),
        in_specs=[pl.BlockSpec((tm, K1), lambda i, j: (i, 0)),
                  pl.BlockSpec((tm, K2), lambda i, j: (i, 0)),
                  pl.BlockSpec((K1 + K2, tn), lambda i, j: (0, j)),
                  pl.BlockSpec((tm, tn), lambda i, j: (i, j))],
        out_specs=pl.BlockSpec((tm, tn), lambda i, j: (i, j)),
        compiler_params=pltpu.CompilerParams(
            dimension_semantics=("parallel", "parallel"),
            vmem_limit_bytes=_vmem_limit(2 * (tm * (K1 + K2) * 2 + (K1 + K2) * tn * 2 + 2 * tm * tn * 4)
                                         + tm * tn * 4)),
        name="outproj",
    )(a1, a2, w, x)


def _pool_kernel(x_ref, g_ref, w_ref, b_ref, scale_ref, o_ref, carry_ref):
    i = pl.program_id(0)
    tm = x_ref.shape[0]
    gw = w_ref.shape[1]

    @pl.when(i == 0)
    def _():
        carry_ref[...] = jnp.zeros_like(carry_ref)

    x = x_ref[...]
    hn = _rms(x) * g_ref[...]
    t = i * tm + lax.broadcasted_iota(jnp.int32, (tm, 1), 0)
    for g, w in enumerate(POOL_WINDOWS):
        sl = slice(g * gw, (g + 1) * gw)
        hg = hn[:, sl]
        cur = jnp.concatenate([carry_ref[:, sl], hg], axis=0)
        k = 1
        while k < w:
            cur = cur + pltpu.roll(cur, k, axis=0)
            k *= 2
        window_sum = cur[POOL_HALO:, :]
        count = jnp.minimum(t + 1, w).astype(F32)
        pooled = window_sum / count - hg
        y = jnp.dot(pooled.astype(BF16), w_ref[g], preferred_element_type=F32) + b_ref[:, sl]
        o_ref[:, sl] = x[:, sl] + y * scale_ref[:, sl]
    carry_ref[...] = hn[tm - POOL_HALO:, :]


def pool_mixer_residual(x, g, pool_w, pool_b, pool_scale):
    S, D = x.shape
    tm = min(NORM_ROWS, S)
    G, gw, _ = pool_w.shape
    return pl.pallas_call(
        _pool_kernel,
        out_shape=jax.ShapeDtypeStruct((S, D), F32),
        grid=(S // tm,),
        in_specs=[pl.BlockSpec((tm, D), lambda i: (i, 0)),
                  pl.BlockSpec((1, D), lambda i: (0, 0)),
                  pl.BlockSpec((G, gw, gw), lambda i: (0, 0, 0)),
                  pl.BlockSpec((1, D), lambda i: (0, 0)),
                  pl.BlockSpec((1, D), lambda i: (0, 0))],
        out_specs=pl.BlockSpec((tm, D), lambda i: (i, 0)),
        scratch_shapes=[pltpu.VMEM((POOL_HALO, D), F32)],
        compiler_params=pltpu.CompilerParams(
            dimension_semantics=("arbitrary",),
            vmem_limit_bytes=_vmem_limit(4 * tm * D * 4 + 2 * G * gw * gw * 2 + 6 * tm * D * 4)),
        name="pool_mixer",
    )(x, g.reshape(1, D), pool_w, pool_b.reshape(1, D), pool_scale.reshape(1, D))


def _router_kernel(x_ref, g_ref, w_ref, b_ref, hn_ref, logit_ref):
    hn = _rms(x_ref[...]) * g_ref[...]
    hn_ref[...] = hn.astype(hn_ref.dtype)
    logit_ref[...] = jnp.dot(hn, w_ref[...], preferred_element_type=F32,
                             precision=lax.Precision.HIGHEST) + b_ref[...]


def router_logits(x, g, w_router, b_router):
    S, D = x.shape
    W = w_router.shape[1]
    tm = min(NORM_ROWS, S)
    return pl.pallas_call(
        _router_kernel,
        out_shape=(jax.ShapeDtypeStruct((S, D), BF16), jax.ShapeDtypeStruct((S, W), F32)),
        grid=(S // tm,),
        in_specs=[pl.BlockSpec((tm, D), lambda i: (i, 0)),
                  pl.BlockSpec((1, D), lambda i: (0, 0)),
                  pl.BlockSpec((D, W), lambda i: (0, 0)),
                  pl.BlockSpec((1, W), lambda i: (0, 0))],
        out_specs=(pl.BlockSpec((tm, D), lambda i: (i, 0)),
                   pl.BlockSpec((tm, W), lambda i: (i, 0))),
        compiler_params=pltpu.CompilerParams(
            dimension_semantics=("parallel",),
            vmem_limit_bytes=_vmem_limit(2 * tm * D * 6 + 2 * D * W * 4 + 4 * tm * D * 4)),
        name="router",
    )(x, g.reshape(1, D), w_router, b_router)


def _expert_kernel(be_ref, x_ref, wg_ref, wu_ref, wd_ref, rw_ref, o_ref):
    del be_ref
    x = x_ref[...]
    a = jnp.dot(x, wg_ref[...], preferred_element_type=F32)
    u = jnp.dot(x, wu_ref[...], preferred_element_type=F32)
    hmid = (a * jax.nn.sigmoid(a) * u).astype(BF16)
    y = jnp.dot(hmid, wd_ref[...], preferred_element_type=F32)
    o_ref[...] = y * rw_ref[...]


def expert_blocks(block_e, xs, w_gate, w_up, w_down, row_w):
    n_rows, D = xs.shape
    Dh = w_gate.shape[2]
    R = MOE_ROWS
    grid_spec = pltpu.PrefetchScalarGridSpec(
        num_scalar_prefetch=1,
        grid=(n_rows // R,),
        in_specs=[pl.BlockSpec((R, D), lambda b, be: (b, 0)),
                  pl.BlockSpec((None, D, Dh), lambda b, be: (be[b], 0, 0)),
                  pl.BlockSpec((None, D, Dh), lambda b, be: (be[b], 0, 0)),
                  pl.BlockSpec((None, Dh, D), lambda b, be: (be[b], 0, 0)),
                  pl.BlockSpec((R, 1), lambda b, be: (b, 0))],
        out_specs=pl.BlockSpec((R, D), lambda b, be: (b, 0)),
    )
    return pl.pallas_call(
        _expert_kernel,
        out_shape=jax.ShapeDtypeStruct((n_rows, D), F32),
        grid_spec=grid_spec,
        compiler_params=pltpu.CompilerParams(
            dimension_semantics=("arbitrary",),
            vmem_limit_bytes=_vmem_limit(2 * (3 * D * Dh * 2 + R * D * 2 + R * D * 4) + 4 * R * Dh * 4)),
        name="moe_experts",
    )(block_e, xs, w_gate, w_up, w_down, row_w)


def moe_residual(x, g, rgw, rgb, rew, reb, w_gate, w_up, w_down):
    T, D = x.shape
    R = MOE_ROWS
    pad = V7X_LANES - N_GROUPS - N_EXPERTS
    w_router = jnp.concatenate([rgw, rew, jnp.zeros((D, pad), F32)], axis=1)
    b_router = jnp.concatenate([rgb, reb, jnp.zeros((pad,), F32)]).reshape(1, V7X_LANES)
    hn, logits = router_logits(x, g, w_router, b_router)

    g_logits = logits[:, :N_GROUPS]
    g_prob = jax.nn.softmax(g_logits, axis=-1)
    _, g_sel = lax.top_k(g_logits, 1)
    g_sel = g_sel[:, 0]
    g_w = jnp.take_along_axis(g_prob, g_sel[:, None], axis=-1)[:, 0]
    e_logits = logits[:, N_GROUPS:N_GROUPS + N_EXPERTS].reshape(T, N_GROUPS, EXPERTS_PER_GROUP)
    e_in = jnp.take_along_axis(e_logits, g_sel[:, None, None], axis=1)[:, 0]
    top_v, top_i = lax.top_k(e_in, TOP_K)
    combine = g_w[:, None] * jax.nn.softmax(top_v, axis=-1)
    expert_id = g_sel[:, None] * EXPERTS_PER_GROUP + top_i

    n_assign = T * TOP_K
    flat_e = expert_id.reshape(-1).astype(jnp.int32)
    onehot = (flat_e[:, None] == jnp.arange(N_EXPERTS, dtype=jnp.int32)[None, :]).astype(jnp.int32)
    running = jnp.cumsum(onehot, axis=0)
    counts = running[-1]
    rank = jnp.take_along_axis(running, flat_e[:, None], axis=1)[:, 0] - 1
    padded = (counts + R - 1) // R * R
    pends = jnp.cumsum(padded)
    pstarts = pends - padded
    dest = pstarts[flat_e] + rank
    n_blocks = (n_assign + N_EXPERTS * (R - 1) + R - 1) // R
    n_rows = n_blocks * R
    flat_tok = jnp.repeat(jnp.arange(T, dtype=jnp.int32), TOP_K)
    row_tok = jnp.full((n_rows,), T, jnp.int32).at[dest].set(flat_tok)
    row_w = jnp.zeros((n_rows,), F32).at[dest].set(combine.reshape(-1))
    block_e = jnp.clip(jnp.searchsorted(pends, jnp.arange(n_blocks, dtype=jnp.int32) * R, side='right'),
                       0, N_EXPERTS - 1).astype(jnp.int32)

    hn_pad = jnp.concatenate([hn, jnp.zeros((1, D), hn.dtype)], axis=0)
    xs = hn_pad[row_tok]
    yb = expert_blocks(block_e, xs, w_gate, w_up, w_down, row_w.reshape(n_rows, 1))
    contrib = yb[dest].reshape(T, TOP_K, D)
    return x + contrib[:, 0] + contrib[:, 1]


def fox_mlstm_residual(xt, norm_g, w, fox_f_bias, fox_q_gain, fox_k_gain, mlstm_i_bias, mlstm_f_bias,
                       mlstm_out_gain, w_out):
    S, D = xt.shape
    H = fox_f_bias.shape[0]
    assert mlstm_i_bias.shape[0] == H and 3 * H <= 32
    fw = H * HEAD_DIM
    o_ff = 3 * fw
    o_mq = o_ff + H
    o_mi = o_mq + 3 * fw
    o_mo = o_mi + 2 * H
    w_main = jnp.concatenate([w[:, :o_ff], w[:, o_mq:o_mi]], axis=1).astype(BF16)
    gate_pad = V7X_LANES - 3 * H
    w_aux = jnp.concatenate([w[:, o_mo:], w[:, o_ff:o_mq], w[:, o_mi:o_mo],
                             jnp.zeros((D, gate_pad), F32)], axis=1).astype(BF16)
    gains = jnp.concatenate([fox_q_gain[None] * (HEAD_DIM ** -0.5), fox_k_gain[None],
                             jnp.zeros((6, HEAD_DIM), F32)], axis=0)
    bias_row = jnp.concatenate([fox_f_bias, mlstm_i_bias, mlstm_f_bias,
                                jnp.zeros((gate_pad,), F32)]).reshape(1, V7X_LANES)

    hn = rmsnorm_bf16(xt, norm_g)
    proj = inproj_main(hn, w_main, gains)
    mo, gates_pre = inproj_aux(hn, w_aux, fw)
    gcol = gate_activations(gates_pre, bias_row, H)
    grow = gcol[:, :32].T
    cq = jnp.broadcast_to(grow[:H, :, None], (H, S, V7X_LANES))
    ck = grow[:H].reshape(H, 1, S)
    y_fox = fox_attention(proj, cq, ck, H)
    y_mlstm = mlstm_mixer(proj, mo, gcol, grow, mlstm_out_gain, H, 3)
    return outproj_residual(y_fox, y_mlstm, w_out.astype(BF16), xt)


def kernel(x, norm_mix, norm_ffn, w_in, fox_f_bias, fox_q_gain, fox_k_gain, mlstm_i_bias, mlstm_f_bias,
           mlstm_out_gain, w_out, pool_w, pool_b, pool_scale, router_group_w, router_group_b,
           router_expert_w, router_expert_b, w_gate, w_up, w_down):
    B, S, D = x.shape
    assert B == 1
    depth = norm_mix.shape[0]
    xt = x.reshape(S, D)

    for layer in range(depth):
        j = layer // 2
        if layer % 2 == 0:
            xt = fox_mlstm_residual(xt, norm_mix[layer], w_in[j], fox_f_bias[j], fox_q_gain[j],
                                    fox_k_gain[j], mlstm_i_bias[j], mlstm_f_bias[j], mlstm_out_gain[j],
                                    w_out[j])
        else:
            xt = pool_mixer_residual(xt, norm_mix[layer], pool_w[j].astype(BF16), pool_b[j], pool_scale[j])
        xt = moe_residual(xt, norm_ffn[layer], router_group_w[layer], router_group_b[layer],
                          router_expert_w[layer], router_expert_b[layer],
                          w_gate[layer].astype(BF16), w_up[layer].astype(BF16),
                          w_down[layer].astype(BF16))
    return xt.reshape(B, S, D)
```

```python
import functools

import jax
import jax.numpy as jnp
from jax import lax
from jax.experimental import pallas as pl
from jax.experimental.pallas import tpu as pltpu

F32 = jnp.float32
BF16 = jnp.bfloat16

HEAD_DIM = 128
GATE_SOFTCAP = 15.0
POOL_WINDOWS = (2, 4, 8, 16)
POOL_HALO = 16
N_GROUPS = 4
EXPERTS_PER_GROUP = 8
N_EXPERTS = N_GROUPS * EXPERTS_PER_GROUP
TOP_K = 2
RMS_EPS = 1e-6

V7X_LANES = 128
V7X_VMEM_BYTES = 64 * 1024 * 1024

NORM_ROWS = 512
MM_ROWS = 1024
MM_COLS = 1024
ATT_Q = 1024
ATT_K = 512
MLSTM_CHUNK = 256
MOE_ROWS = 256
NEG_BIG = -1e30
LOG2E = 1.4426950408889634


def _vmem_limit(nbytes):
    return int(min(max(nbytes * 3 // 2, 16 * 1024 * 1024), V7X_VMEM_BYTES - 8 * 1024 * 1024))


def _rms(x, eps=RMS_EPS):
    return x * lax.rsqrt(jnp.mean(x * x, axis=-1, keepdims=True) + eps)


def _log_sigmoid(x):
    return -(jnp.maximum(-x, 0.0) + jnp.log1p(jnp.exp(-jnp.abs(x))))


def _rmsnorm_kernel(x_ref, g_ref, o_ref):
    o_ref[...] = (_rms(x_ref[...]) * g_ref[...]).astype(o_ref.dtype)


def rmsnorm_bf16(x, g):
    S, D = x.shape
    return pl.pallas_call(
        _rmsnorm_kernel,
        out_shape=jax.ShapeDtypeStruct((S, D), BF16),
        grid=(S // NORM_ROWS,),
        in_specs=[pl.BlockSpec((NORM_ROWS, D), lambda i: (i, 0)),
                  pl.BlockSpec((1, D), lambda i: (0, 0))],
        out_specs=pl.BlockSpec((NORM_ROWS, D), lambda i: (i, 0)),
        compiler_params=pltpu.CompilerParams(
            dimension_semantics=("parallel",),
            vmem_limit_bytes=_vmem_limit(2 * NORM_ROWS * D * 6)),
        name="rmsnorm",
    )(x, g.reshape(1, D))


def _inproj_main_kernel(a_ref, w_ref, gain_ref, o_ref, *, n_heads_per_tile):
    j = pl.program_id(1)
    acc = jnp.dot(a_ref[...], w_ref[...], preferred_element_type=F32)

    @pl.when(j < 2)
    def _():
        g = gain_ref[pl.ds(j, 1), :]
        for h in range(n_heads_per_tile):
            a = acc[:, h * HEAD_DIM:(h + 1) * HEAD_DIM]
            o_ref[:, h * HEAD_DIM:(h + 1) * HEAD_DIM] = (_rms(a) * g).astype(o_ref.dtype)

    @pl.when(j == 4)
    def _():
        o_ref[...] = (acc * (HEAD_DIM ** -0.5)).astype(o_ref.dtype)

    @pl.when(jnp.logical_and(j >= 2, j != 4))
    def _():
        o_ref[...] = acc.astype(o_ref.dtype)


def inproj_main(hn, w_main, gains):
    S, D = hn.shape
    N = w_main.shape[1]
    tm, tn = min(MM_ROWS, S), MM_COLS
    return pl.pallas_call(
        functools.partial(_inproj_main_kernel, n_heads_per_tile=tn // HEAD_DIM),
        out_shape=jax.ShapeDtypeStruct((S, N), BF16),
        grid=(S // tm, N // tn),
        in_specs=[pl.BlockSpec((tm, D), lambda i, j: (i, 0)),
                  pl.BlockSpec((D, tn), lambda i, j: (0, j)),
                  pl.BlockSpec((8, HEAD_DIM), lambda i, j: (0, 0))],
        out_specs=pl.BlockSpec((tm, tn), lambda i, j: (i, j)),
        compiler_params=pltpu.CompilerParams(
            dimension_semantics=("parallel", "parallel"),
            vmem_limit_bytes=_vmem_limit(2 * (tm * D * 2 + D * tn * 2 + tm * tn * 2) + 2 * tm * tn * 4)),
        name="inproj_main",
    )(hn, w_main, gains)


def _inproj_aux_kernel(a_ref, w_ref, mo_ref, gate_ref):
    acc = jnp.dot(a_ref[...], w_ref[...], preferred_element_type=F32)
    n_mo = mo_ref.shape[1]
    mo_ref[...] = acc[:, :n_mo]
    gate_ref[...] = acc[:, n_mo:]


def inproj_aux(hn, w_aux, n_mo):
    S, D = hn.shape
    N = w_aux.shape[1]
    tm = min(NORM_ROWS, S)
    return pl.pallas_call(
        _inproj_aux_kernel,
        out_shape=(jax.ShapeDtypeStruct((S, n_mo), F32),
                   jax.ShapeDtypeStruct((S, N - n_mo), F32)),
        grid=(S // tm,),
        in_specs=[pl.BlockSpec((tm, D), lambda i: (i, 0)),
                  pl.BlockSpec((D, N), lambda i: (0, 0))],
        out_specs=(pl.BlockSpec((tm, n_mo), lambda i: (i, 0)),
                   pl.BlockSpec((tm, N - n_mo), lambda i: (i, 0))),
        compiler_params=pltpu.CompilerParams(
            dimension_semantics=("parallel",),
            vmem_limit_bytes=_vmem_limit(2 * (tm * D * 2 + D * N * 2 + tm * N * 4) + tm * N * 4)),
        name="inproj_aux",
    )(hn, w_aux)


def _split3_dot(tri, val):
    v1 = val.astype(BF16)
    r1 = val - v1.astype(F32)
    v2 = r1.astype(BF16)
    v3 = (r1 - v2.astype(F32)).astype(BF16)
    out = jnp.dot(tri, v1, preferred_element_type=F32)
    out += jnp.dot(tri, v2, preferred_element_type=F32)
    out += jnp.dot(tri, v3, preferred_element_type=F32)
    return out


def _gates_kernel(g_ref, bias_ref, o_ref, carry_ref, *, n_heads):
    @pl.when(pl.program_id(0) == 0)
    def _():
        carry_ref[...] = jnp.zeros_like(carry_ref)

    rows = g_ref.shape[0]
    z = g_ref[...] + bias_ref[...]
    lane = lax.broadcasted_iota(jnp.int32, z.shape, 1)
    capped = GATE_SOFTCAP * jnp.tanh(z / GATE_SOFTCAP)
    is_fox = lane < n_heads
    is_i = jnp.logical_and(lane >= n_heads, lane < 2 * n_heads)
    is_f = jnp.logical_and(lane >= 2 * n_heads, lane < 3 * n_heads)
    logf = jnp.where(is_fox, _log_sigmoid(z), jnp.where(is_f, _log_sigmoid(capped), 0.0))
    r = lax.broadcasted_iota(jnp.int32, (rows, rows), 0)
    c = lax.broadcasted_iota(jnp.int32, (rows, rows), 1)
    tri = jnp.where(r >= c, 1.0, 0.0).astype(BF16)
    cum = _split3_dot(tri, logf)
    glob = cum + carry_ref[...]
    o_ref[...] = jnp.where(is_fox, glob * LOG2E, jnp.where(is_i, capped, cum))
    carry_ref[...] = glob[rows - 1:rows, :]


def gate_activations(gates_pre, bias_row, n_heads):
    S, W = gates_pre.shape
    tb = MLSTM_CHUNK
    return pl.pallas_call(
        functools.partial(_gates_kernel, n_heads=n_heads),
        out_shape=jax.ShapeDtypeStruct((S, W), F32),
        grid=(S // tb,),
        in_specs=[pl.BlockSpec((tb, W), lambda i: (i, 0)),
                  pl.BlockSpec((1, W), lambda i: (0, 0))],
        out_specs=pl.BlockSpec((tb, W), lambda i: (i, 0)),
        scratch_shapes=[pltpu.VMEM((1, W), F32)],
        compiler_params=pltpu.CompilerParams(dimension_semantics=("arbitrary",)),
        name="gate_activations",
    )(gates_pre, bias_row)


def _fox_kernel(q_ref, k_ref, v_ref, cq_ref, ck_ref, o_ref, m_ref, acc_ref, s_ref, p_ref, alpha_ref, *, n_sub):
    i = pl.program_id(1)
    d = HEAD_DIM
    tk = q_ref.shape[0] // n_sub
    assert n_sub % 2 == 0

    m_ref[...] = jnp.full_like(m_ref, NEG_BIG)
    acc_ref[...] = jnp.zeros_like(acc_ref)
    p_ref[1] = jnp.zeros_like(p_ref[1])
    alpha_ref[1] = jnp.ones_like(alpha_ref[1])
    lane = lax.broadcasted_iota(jnp.int32, (tk, d), 1)
    ones_col = jnp.where(lane == 0, 1.0, 0.0).astype(BF16)
    row = lax.broadcasted_iota(jnp.int32, (tk, tk), 0)
    col = lax.broadcasted_iota(jnp.int32, (tk, tk), 1)
    causal = col <= row

    def qk_stage(sub, j, par):
        start = pl.multiple_of(j * tk, tk)
        s = lax.dot_general(q_ref[pl.ds(sub * tk, tk), :], k_ref[pl.ds(start, tk), :],
                            (((1,), (1,)), ((), ())), preferred_element_type=F32)
        s_ref[par, sub] = s - ck_ref[:, pl.ds(start, tk)]

    def sm_stage(sub, par, masked):
        rows = pl.ds(sub * tk, tk)
        s = s_ref[par, sub]
        if masked:
            s = jnp.where(causal, s, NEG_BIG)
        cq = cq_ref[rows, :]
        m_prev = m_ref[rows, :]
        m_new = jnp.maximum(m_prev, jnp.max(s, axis=-1, keepdims=True) + cq)
        p_ref[par, sub] = jnp.exp2(s - jnp.tile(m_new - cq, (1, tk // V7X_LANES))).astype(BF16)
        alpha_ref[par, rows, :] = jnp.exp2(m_prev - m_new)
        m_ref[rows, :] = m_new

    def pv_stage(sub, j, par):
        rows = pl.ds(sub * tk, tk)
        start = pl.multiple_of(j * tk, tk)
        v_aug = jnp.concatenate([v_ref[pl.ds(start, tk), :], ones_col], axis=1)
        acc_ref[rows, :] = (jnp.tile(alpha_ref[par, rows, :], (1, 2)) * acc_ref[rows, :]
                            + jnp.dot(p_ref[par, sub], v_aug, preferred_element_type=F32))

    n_full = i * n_sub
    for sub in range(n_sub):
        qk_stage(sub, 0, 0)

    def body(tt, carry):
        for par in (0, 1):
            step = 2 * tt + par
            for sub in range(n_sub):
                qk_stage(sub, step + 1, 1 - par)
                sm_stage(sub, par, masked=False)
                pv_stage(sub, jnp.maximum(step - 1, 0), 1 - par)
        return carry

    lax.fori_loop(0, n_full // 2, body, 0)
    for kk in range(n_sub + 1):
        par = kk % 2
        for sub in range(n_sub):
            if kk + 1 <= sub:
                qk_stage(sub, n_full + kk + 1, 1 - par)
            if kk <= sub:
                sm_stage(sub, par, masked=(kk == sub))
            if kk - 1 <= sub:
                pv_stage(sub, jnp.maximum(n_full + kk - 1, 0), 1 - par)
    acc = acc_ref[...]
    o_ref[...] = (acc[:, :d] / acc[:, d:d + 1]).astype(o_ref.dtype)


def fox_attention(proj, cq, ck, n_heads):
    S = proj.shape[0]
    tk = min(ATT_K, S)
    n_sub = max(1, min(ATT_Q, S) // tk)
    tq = n_sub * tk
    H = n_heads
    return pl.pallas_call(
        functools.partial(_fox_kernel, n_sub=n_sub),
        out_shape=jax.ShapeDtypeStruct((S, H * HEAD_DIM), BF16),
        grid=(H, S // tq),
        in_specs=[pl.BlockSpec((tq, HEAD_DIM), lambda h, i: (i, h)),
                  pl.BlockSpec((S, HEAD_DIM), lambda h, i: (0, H + h)),
                  pl.BlockSpec((S, HEAD_DIM), lambda h, i: (0, 2 * H + h)),
                  pl.BlockSpec((None, tq, V7X_LANES), lambda h, i: (h, i, 0)),
                  pl.BlockSpec((None, 1, S), lambda h, i: (h, 0, 0))],
        out_specs=pl.BlockSpec((tq, HEAD_DIM), lambda h, i: (i, h)),
        scratch_shapes=[pltpu.VMEM((tq, V7X_LANES), F32), pltpu.VMEM((tq, 2 * HEAD_DIM), F32),
                        pltpu.VMEM((2, n_sub, tk, tk), F32), pltpu.VMEM((2, n_sub, tk, tk), BF16),
                        pltpu.VMEM((2, tq, V7X_LANES), F32)],
        compiler_params=pltpu.CompilerParams(
            dimension_semantics=("parallel", "arbitrary"),
            vmem_limit_bytes=_vmem_limit(4 * S * HEAD_DIM * 2 + 2 * n_sub * tk * tk * (4 + 2)
                                         + 4 * tk * tk * 4 + 12 * tq * HEAD_DIM * 4 + 16 * S * 4)),
        name="fox_attention",
    )(proj, proj, proj, cq, ck)


def _mlstm_kernel(q_ref, k_ref, v_ref, mo_ref, gcol_ref, grow_ref, gain_ref, o_ref,
                  state_ref, m_ref, *, n_heads):
    L = q_ref.shape[0]
    d = HEAD_DIM

    @pl.when(pl.program_id(0) == 0)
    def _():
        state_ref[...] = jnp.zeros_like(state_ref)
        m_ref[...] = jnp.zeros_like(m_ref)

    row = lax.broadcasted_iota(jnp.int32, (L, L), 0)
    col = lax.broadcasted_iota(jnp.int32, (L, L), 1)
    causal = col <= row
    lane = lax.broadcasted_iota(jnp.int32, (L, d), 1)
    ones_col = jnp.where(lane == 0, 1.0, 0.0).astype(BF16)

    gcol = gcol_ref[...]
    grow = grow_ref[...]
    for h in range(n_heads):
        sl = slice(h * d, (h + 1) * d)
        q = q_ref[:, sl]
        k = k_ref[:, sl]
        v = v_ref[:, sl]
        i_col = gcol[:, n_heads + h:n_heads + h + 1]
        b_col = gcol[:, 2 * n_heads + h:2 * n_heads + h + 1]
        i_row = grow[n_heads + h:n_heads + h + 1, :]
        b_row = grow[2 * n_heads + h:2 * n_heads + h + 1, :]
        m_prev = m_ref[h:h + 1, 0:1]
        state = state_ref[h]

        log_intra = jnp.where(causal, b_col - b_row + i_row, NEG_BIG)
        log_inter = b_col + m_prev
        m_t = jnp.maximum(log_inter, jnp.max(log_intra, axis=-1, keepdims=True))
        w_intra = jnp.exp(log_intra - m_t)
        w_inter = jnp.exp(log_inter - m_t)
        qk = lax.dot_general(q, k, (((1,), (1,)), ((), ())), preferred_element_type=F32) * w_intra
        v_aug = jnp.concatenate([v, ones_col], axis=1)
        tot = jnp.dot(qk.astype(BF16), v_aug, preferred_element_type=F32)
        tot = tot + w_inter * jnp.dot(q, state.astype(BF16), preferred_element_type=F32)
        num = tot[:, :d]
        den = tot[:, d:d + 1]
        hval = num / jnp.maximum(jnp.abs(den), jnp.exp(-m_t))

        b_last = b_col[L - 1:L, :]
        log_w_state = b_last - b_col + i_col
        m_new = jnp.maximum(b_last + m_prev, jnp.max(log_w_state, axis=0, keepdims=True))
        decay = jnp.exp(b_last + m_prev - m_new)
        w_s = jnp.exp(log_w_state - m_new)
        wv = (w_s * v_aug.astype(F32)).astype(BF16)
        upd = lax.dot_general(k, wv, (((0,), (0,)), ((), ())), preferred_element_type=F32)
        state_ref[h] = decay * state + upd
        m_ref[h:h + 1, :] = jnp.broadcast_to(m_new, (1, m_ref.shape[1]))

        hn = _rms(hval) * gain_ref[:, sl]
        o_ref[:, sl] = (jax.nn.sigmoid(mo_ref[:, sl]) * hn).astype(o_ref.dtype)


def mlstm_mixer(proj, mo, gcol, grow, out_gain, n_heads, q_block):
    S = proj.shape[0]
    L = min(MLSTM_CHUNK, S)
    W = n_heads * HEAD_DIM
    return pl.pallas_call(
        functools.partial(_mlstm_kernel, n_heads=n_heads),
        out_shape=jax.ShapeDtypeStruct((S, W), BF16),
        grid=(S // L,),
        in_specs=[pl.BlockSpec((L, W), lambda c: (c, q_block)),
                  pl.BlockSpec((L, W), lambda c: (c, q_block + 1)),
                  pl.BlockSpec((L, W), lambda c: (c, q_block + 2)),
                  pl.BlockSpec((L, W), lambda c: (c, 0)),
                  pl.BlockSpec((L, gcol.shape[1]), lambda c: (c, 0)),
                  pl.BlockSpec((grow.shape[0], L), lambda c: (0, c)),
                  pl.BlockSpec((1, W), lambda c: (0, 0))],
        out_specs=pl.BlockSpec((L, W), lambda c: (c, 0)),
        scratch_shapes=[pltpu.VMEM((n_heads, HEAD_DIM, 2 * HEAD_DIM), F32),
                        pltpu.VMEM((n_heads, V7X_LANES), F32)],
        compiler_params=pltpu.CompilerParams(
            dimension_semantics=("arbitrary",),
            vmem_limit_bytes=_vmem_limit(2 * L * W * (3 * 2 + 4 + 2) + 16 * L * L * 4)),
        name="mlstm",
    )(proj, proj, proj, mo, gcol, grow, out_gain.reshape(1, W))


def _outproj_kernel(a1_ref, a2_ref, w_ref, x_ref, o_ref):
    k1 = a1_ref.shape[1]
    acc = jnp.dot(a1_ref[...], w_ref[:k1, :], preferred_element_type=F32)
    acc += jnp.dot(a2_ref[...], w_ref[k1:, :], preferred_element_type=F32)
    o_ref[...] = x_ref[...] + acc


def outproj_residual(a1, a2, w, x):
    S, K1 = a1.shape
    K2 = a2.shape[1]
    N = w.shape[1]
    tm, tn = min(MM_ROWS, S), MM_COLS
    return pl.pallas_call(
        _outproj_kernel,
        out_shape=jax.ShapeDtypeStruct((S, N), F32),
        grid=(S // tm, N // tn),
        in_specs=[pl.BlockSpec((tm, K1), lambda i, j: (i, 0)),
                  pl.BlockSpec((tm, K2), lambda i, j: (i, 0)),
                  pl.BlockSpec((K1 + K2, tn), lambda i, j: (0, j)),
                  pl.BlockSpec((tm, tn), lambda i, j: (i, j))],
        out_specs=pl.BlockSpec((tm, tn), lambda i, j: (i, j)),
        compiler_params=pltpu.CompilerParams(
            dimension_semantics=("parallel", "parallel"),
            vmem_limit_bytes=_vmem_limit(2 * (tm * (K1 + K2) * 2 + (K1 + K2) * tn * 2 + 2 * tm * tn * 4)
                                         + tm * tn * 4)),
        name="outproj",
    )(a1, a2, w, x)


def _pool_kernel(x_ref, g_ref, w_ref, b_ref, scale_ref, o_ref, carry_ref):
    i = pl.program_id(0)
    tm = x_ref.shape[0]
    gw = w_ref.shape[1]

    @pl.when(i == 0)
    def _():
        carry_ref[...] = jnp.zeros_like(carry_ref)

    x = x_ref[...]
    hn = _rms(x) * g_ref[...]
    t = i * tm + lax.broadcasted_iota(jnp.int32, (tm, 1), 0)
    for g, w in enumerate(POOL_WINDOWS):
        sl = slice(g * gw, (g + 1) * gw)
        hg = hn[:, sl]
        cur = jnp.concatenate([carry_ref[:, sl], hg], axis=0)
        k = 1
        while k < w:
            cur = cur + pltpu.roll(cur, k, axis=0)
            k *= 2
        window_sum = cur[POOL_HALO:, :]
        count = jnp.minimum(t + 1, w).astype(F32)
        pooled = window_sum / count - hg
        y = jnp.dot(pooled.astype(BF16), w_ref[g], preferred_element_type=F32) + b_ref[:, sl]
        o_ref[:, sl] = x[:, sl] + y * scale_ref[:, sl]
    carry_ref[...] = hn[tm - POOL_HALO:, :]


def pool_mixer_residual(x, g, pool_w, pool_b, pool_scale):
    S, D = x.shape
    tm = min(NORM_ROWS, S)
    G, gw, _ = pool_w.shape
    return pl.pallas_call(
        _pool_kernel,
        out_shape=jax.ShapeDtypeStruct((S, D), F32),
        grid=(S // tm,),
        in_specs=[pl.BlockSpec((tm, D), lambda i: (i, 0)),
                  pl.BlockSpec((1, D), lambda i: (0, 0)),
                  pl.BlockSpec((G, gw, gw), lambda i: (0, 0, 0)),
                  pl.BlockSpec((1, D), lambda i: (0, 0)),
                  pl.BlockSpec((1, D), lambda i: (0, 0))],
        out_specs=pl.BlockSpec((tm, D), lambda i: (i, 0)),
        scratch_shapes=[pltpu.VMEM((POOL_HALO, D), F32)],
        compiler_params=pltpu.CompilerParams(
            dimension_semantics=("arbitrary",),
            vmem_limit_bytes=_vmem_limit(4 * tm * D * 4 + 2 * G * gw * gw * 2 + 6 * tm * D * 4)),
        name="pool_mixer",
    )(x, g.reshape(1, D), pool_w, pool_b.reshape(1, D), pool_scale.reshape(1, D))


def _router_kernel(x_ref, g_ref, w_ref, b_ref, hn_ref, route_ref):
    hn = _rms(x_ref[...]) * g_ref[...]
    hn_ref[...] = hn.astype(hn_ref.dtype)
    logits = jnp.dot(hn, w_ref[...], preferred_element_type=F32,
                     precision=lax.Precision.HIGHEST) + b_ref[...]
    lane = lax.broadcasted_iota(jnp.int32, logits.shape, 1).astype(F32)
    n_lanes = float(logits.shape[1])

    def first_argmax(vals):
        top = jnp.max(vals, axis=-1, keepdims=True)
        return top, jnp.min(jnp.where(vals == top, lane, n_lanes), axis=-1, keepdims=True)

    is_group = lane < N_GROUPS
    g_top, g_sel = first_argmax(jnp.where(is_group, logits, NEG_BIG))
    g_w = 1.0 / jnp.sum(jnp.where(is_group, jnp.exp(logits - g_top), 0.0), axis=-1, keepdims=True)
    lo = N_GROUPS + EXPERTS_PER_GROUP * g_sel
    e_logits = jnp.where(jnp.logical_and(lane >= lo, lane < lo + EXPERTS_PER_GROUP), logits, NEG_BIG)
    v1, i1 = first_argmax(e_logits)
    v2, i2 = first_argmax(jnp.where(lane == i1, NEG_BIG, e_logits))
    e21 = jnp.exp(v2 - v1)
    w1 = g_w / (1.0 + e21)
    w2 = g_w * e21 / (1.0 + e21)
    route_ref[...] = jnp.where(lane == 0, i1 - N_GROUPS,
                               jnp.where(lane == 1, i2 - N_GROUPS,
                                         jnp.where(lane == 2, w1, jnp.where(lane == 3, w2, 0.0))))


def router(x, g, w_router, b_router):
    S, D = x.shape
    W = w_router.shape[1]
    tm = min(NORM_ROWS, S)
    return pl.pallas_call(
        _router_kernel,
        out_shape=(jax.ShapeDtypeStruct((S, D), BF16), jax.ShapeDtypeStruct((S, W), F32)),
        grid=(S // tm,),
        in_specs=[pl.BlockSpec((tm, D), lambda i: (i, 0)),
                  pl.BlockSpec((1, D), lambda i: (0, 0)),
                  pl.BlockSpec((D, W), lambda i: (0, 0)),
                  pl.BlockSpec((1, W), lambda i: (0, 0))],
        out_specs=(pl.BlockSpec((tm, D), lambda i: (i, 0)),
                   pl.BlockSpec((tm, W), lambda i: (i, 0))),
        compiler_params=pltpu.CompilerParams(
            dimension_semantics=("parallel",),
            vmem_limit_bytes=_vmem_limit(2 * tm * D * 6 + 2 * D * W * 4 + 4 * tm * D * 4)),
        name="router",
    )(x, g.reshape(1, D), w_router, b_router)


CAST_ROWS = 128


def _expert_kernel(be_ref, first_ref, next_ref, active_ref, x_ref, rw_ref, wg_hbm, wu_hbm, wd_hbm, o_ref,
                   stage_g, stage_u, stage_d, wg_ref, wu_ref, wd_ref, sem, *, layer):
    b = pl.program_id(0)

    def weight_copies(e):
        return (pltpu.make_async_copy(wg_hbm.at[layer, e], stage_g, sem.at[0]),
                pltpu.make_async_copy(wu_hbm.at[layer, e], stage_u, sem.at[1]),
                pltpu.make_async_copy(wd_hbm.at[layer, e], stage_d, sem.at[2]))

    @pl.when(b == 0)
    def _():
        for cp in weight_copies(be_ref[0]):
            cp.start()

    @pl.when(first_ref[b] == 1)
    def _():
        for cp in weight_copies(be_ref[b]):
            cp.wait()
        for stage, dst in ((stage_g, wg_ref), (stage_u, wu_ref), (stage_d, wd_ref)):
            def cast_rows(r, carry, stage=stage, dst=dst):
                rows = pl.ds(pl.multiple_of(r * CAST_ROWS, CAST_ROWS), CAST_ROWS)
                dst[rows, :] = stage[rows, :].astype(BF16)
                return carry
            lax.fori_loop(0, stage.shape[0] // CAST_ROWS, cast_rows, 0)

        @pl.when(next_ref[b] >= 0)
        def _():
            for cp in weight_copies(next_ref[b]):
                cp.start()

    @pl.when(active_ref[b] == 1)
    def _():
        x = x_ref[...]
        a = jnp.dot(x, wg_ref[...], preferred_element_type=F32)
        u = jnp.dot(x, wu_ref[...], preferred_element_type=F32)
        hmid = (a * jax.nn.sigmoid(a) * u).astype(BF16)
        y = jnp.dot(hmid, wd_ref[...], preferred_element_type=F32)
        o_ref[...] = y * rw_ref[...]

    @pl.when(active_ref[b] == 0)
    def _():
        o_ref[...] = jnp.zeros_like(o_ref)


def expert_blocks(block_e, first, next_e, active, xs, row_w, w_gate, w_up, w_down, layer):
    n_rows, D = xs.shape
    Dh = w_gate.shape[3]
    R = MOE_ROWS
    hbm = pl.BlockSpec(memory_space=pl.ANY)
    grid_spec = pltpu.PrefetchScalarGridSpec(
        num_scalar_prefetch=4,
        grid=(n_rows // R,),
        in_specs=[pl.BlockSpec((R, D), lambda b, *_: (b, 0)),
                  pl.BlockSpec((R, 1), lambda b, *_: (b, 0)),
                  hbm, hbm, hbm],
        out_specs=pl.BlockSpec((R, D), lambda b, *_: (b, 0)),
        scratch_shapes=[pltpu.VMEM((D, Dh), F32), pltpu.VMEM((D, Dh), F32), pltpu.VMEM((Dh, D), F32),
                        pltpu.VMEM((D, Dh), BF16), pltpu.VMEM((D, Dh), BF16), pltpu.VMEM((Dh, D), BF16),
                        pltpu.SemaphoreType.DMA((3,))],
    )
    return pl.pallas_call(
        functools.partial(_expert_kernel, layer=layer),
        out_shape=jax.ShapeDtypeStruct((n_rows, D), F32),
        grid_spec=grid_spec,
        compiler_params=pltpu.CompilerParams(
            dimension_semantics=("arbitrary",),
            vmem_limit_bytes=_vmem_limit(3 * D * Dh * (4 + 2) + 2 * (R * D * 2 + R * D * 4) + 6 * R * Dh * 4)),
        name="moe_experts",
    )(block_e, first, next_e, active, xs, row_w, w_gate, w_up, w_down)


def _combine_kernel(x_ref, c_ref, o_ref):
    D = x_ref.shape[1]
    o_ref[...] = x_ref[...] + c_ref[:, :D] + c_ref[:, D:]


def combine_residual(x, contrib):
    T, D = x.shape
    tm = min(NORM_ROWS, T)
    return pl.pallas_call(
        _combine_kernel,
        out_shape=jax.ShapeDtypeStruct((T, D), F32),
        grid=(T // tm,),
        in_specs=[pl.BlockSpec((tm, D), lambda i: (i, 0)),
                  pl.BlockSpec((tm, TOP_K * D), lambda i: (i, 0))],
        out_specs=pl.BlockSpec((tm, D), lambda i: (i, 0)),
        compiler_params=pltpu.CompilerParams(
            dimension_semantics=("parallel",),
            vmem_limit_bytes=_vmem_limit(2 * tm * D * 4 * (2 + TOP_K))),
        name="moe_combine",
    )(x, contrib)


def moe_residual(x, g, rgw, rgb, rew, reb, w_gate, w_up, w_down, layer):
    T, D = x.shape
    R = MOE_ROWS
    pad = V7X_LANES - N_GROUPS - N_EXPERTS
    w_router = jnp.concatenate([rgw, rew, jnp.zeros((D, pad), F32)], axis=1)
    b_router = jnp.concatenate([rgb, reb, jnp.zeros((pad,), F32)]).reshape(1, V7X_LANES)
    hn, route = router(x, g, w_router, b_router)
    expert_id = route[:, :TOP_K].astype(jnp.int32)
    combine = route[:, TOP_K:2 * TOP_K]

    n_assign = T * TOP_K
    flat_e = expert_id.reshape(-1)
    onehot = (flat_e[:, None] == jnp.arange(N_EXPERTS, dtype=jnp.int32)[None, :]).astype(jnp.int32)
    running = jnp.cumsum(onehot, axis=0)
    counts = running[-1]
    rank = jnp.take_along_axis(running, flat_e[:, None], axis=1)[:, 0] - 1
    padded = (counts + R - 1) // R * R
    pends = jnp.cumsum(padded)
    pstarts = pends - padded
    dest = pstarts[flat_e] + rank
    n_blocks = (n_assign + N_EXPERTS * (R - 1) + R - 1) // R
    n_rows = n_blocks * R
    flat_tok = jnp.repeat(jnp.arange(T, dtype=jnp.int32), TOP_K)
    row_tok = jnp.zeros((n_rows,), jnp.int32).at[dest].set(flat_tok)
    row_w = jnp.zeros((n_rows,), F32).at[dest].set(combine.reshape(-1))

    blk_start = jnp.arange(n_blocks, dtype=jnp.int32) * R
    active = blk_start < pends[-1]
    block_e = jnp.minimum(jnp.sum(blk_start[:, None] >= pends[None, :], axis=1), N_EXPERTS - 1).astype(jnp.int32)
    prev_e = jnp.concatenate([jnp.full((1,), -1, jnp.int32), block_e[:-1]])
    first = jnp.logical_and(active, block_e != prev_e)
    later = lax.cummin(jnp.where(first, block_e, N_EXPERTS)[::-1])[::-1]
    next_e = jnp.concatenate([later[1:], jnp.full((1,), N_EXPERTS, jnp.int32)])
    next_e = jnp.where(next_e >= N_EXPERTS, -1, next_e).astype(jnp.int32)

    xs = hn[row_tok]
    yb = expert_blocks(block_e, first.astype(jnp.int32), next_e, active.astype(jnp.int32),
                       xs, row_w.reshape(n_rows, 1), w_gate, w_up, w_down, layer)
    contrib = yb[dest].reshape(T, TOP_K * D)
    return combine_residual(x, contrib)


def fox_mlstm_residual(xt, norm_g, w, fox_f_bias, fox_q_gain, fox_k_gain, mlstm_i_bias, mlstm_f_bias,
                       mlstm_out_gain, w_out):
    S, D = xt.shape
    H = fox_f_bias.shape[0]
    assert mlstm_i_bias.shape[0] == H and 3 * H <= 32
    fw = H * HEAD_DIM
    o_ff = 3 * fw
    o_mq = o_ff + H
    o_mi = o_mq + 3 * fw
    o_mo = o_mi + 2 * H
    w_main = jnp.concatenate([w[:, :o_ff], w[:, o_mq:o_mi]], axis=1).astype(BF16)
    gate_pad = V7X_LANES - 3 * H
    w_aux = jnp.concatenate([w[:, o_mo:], w[:, o_ff:o_mq], w[:, o_mi:o_mo],
                             jnp.zeros((D, gate_pad), F32)], axis=1).astype(BF16)
    gains = jnp.concatenate([fox_q_gain[None] * (LOG2E * HEAD_DIM ** -0.5), fox_k_gain[None],
                             jnp.zeros((6, HEAD_DIM), F32)], axis=0)
    bias_row = jnp.concatenate([fox_f_bias, mlstm_i_bias, mlstm_f_bias,
                                jnp.zeros((gate_pad,), F32)]).reshape(1, V7X_LANES)

    hn = rmsnorm_bf16(xt, norm_g)
    proj = inproj_main(hn, w_main, gains)
    mo, gates_pre = inproj_aux(hn, w_aux, fw)
    gcol = gate_activations(gates_pre, bias_row, H)
    grow = gcol[:, :32].T
    cq = jnp.broadcast_to(grow[:H, :, None], (H, S, V7X_LANES))
    ck = grow[:H].reshape(H, 1, S)
    y_fox = fox_attention(proj, cq, ck, H)
    y_mlstm = mlstm_mixer(proj, mo, gcol, grow, mlstm_out_gain, H, 3)
    return outproj_residual(y_fox, y_mlstm, w_out.astype(BF16), xt)


def kernel(x, norm_mix, norm_ffn, w_in, fox_f_bias, fox_q_gain, fox_k_gain, mlstm_i_bias, mlstm_f_bias,
           mlstm_out_gain, w_out, pool_w, pool_b, pool_scale, router_group_w, router_group_b,
           router_expert_w, router_expert_b, w_gate, w_up, w_down):
    B, S, D = x.shape
    assert B == 1
    depth = norm_mix.shape[0]
    xt = x.reshape(S, D)

    for layer in range(depth):
        j = layer // 2
        if layer % 2 == 0:
            xt = fox_mlstm_residual(xt, norm_mix[layer], w_in[j], fox_f_bias[j], fox_q_gain[j],
                                    fox_k_gain[j], mlstm_i_bias[j], mlstm_f_bias[j], mlstm_out_gain[j],
                                    w_out[j])
        else:
            xt = pool_mixer_residual(xt, norm_mix[layer], pool_w[j].astype(BF16), pool_b[j], pool_scale[j])
        xt = moe_residual(xt, norm_ffn[layer], router_group_w[layer], router_group_b[layer],
                          router_expert_w[layer], router_expert_b[layer],
                          w_gate, w_up, w_down, layer)
    return xt.reshape(B, S, D)
```

```python
import functools

import jax
import jax.numpy as jnp
from jax import lax
from jax.experimental import pallas as pl
from jax.experimental.pallas import tpu as pltpu

F32 = jnp.float32
BF16 = jnp.bfloat16

HEAD_DIM = 128
GATE_SOFTCAP = 15.0
POOL_WINDOWS = (2, 4, 8, 16)
POOL_HALO = 16
N_GROUPS = 4
EXPERTS_PER_GROUP = 8
N_EXPERTS = N_GROUPS * EXPERTS_PER_GROUP
TOP_K = 2
RMS_EPS = 1e-6

V7X_LANES = 128
V7X_VMEM_BYTES = 64 * 1024 * 1024

NORM_ROWS = 512
MM_ROWS = 1024
MM_COLS = 1024
ATT_Q = 1024
ATT_K = 512
MLSTM_CHUNK = 256
MOE_ROWS = 256
NEG_BIG = -1e30
LOG2E = 1.4426950408889634


def _vmem_limit(nbytes):
    return int(min(max(nbytes * 3 // 2, 16 * 1024 * 1024), V7X_VMEM_BYTES - 8 * 1024 * 1024))


def _rms(x, eps=RMS_EPS):
    return x * lax.rsqrt(jnp.mean(x * x, axis=-1, keepdims=True) + eps)


def _log_sigmoid(x):
    return -(jnp.maximum(-x, 0.0) + jnp.log1p(jnp.exp(-jnp.abs(x))))


def _rmsnorm_kernel(x_ref, g_ref, o_ref):
    o_ref[...] = (_rms(x_ref[...]) * g_ref[...]).astype(o_ref.dtype)


def rmsnorm_bf16(x, g):
    S, D = x.shape
    return pl.pallas_call(
        _rmsnorm_kernel,
        out_shape=jax.ShapeDtypeStruct((S, D), BF16),
        grid=(S // NORM_ROWS,),
        in_specs=[pl.BlockSpec((NORM_ROWS, D), lambda i: (i, 0)),
                  pl.BlockSpec((1, D), lambda i: (0, 0))],
        out_specs=pl.BlockSpec((NORM_ROWS, D), lambda i: (i, 0)),
        compiler_params=pltpu.CompilerParams(
            dimension_semantics=("parallel",),
            vmem_limit_bytes=_vmem_limit(2 * NORM_ROWS * D * 6)),
        name="rmsnorm",
    )(x, g.reshape(1, D))


def _inproj_main_kernel(a_ref, w_ref, gain_ref, o_ref, *, n_heads_per_tile):
    j = pl.program_id(1)
    acc = jnp.dot(a_ref[...], w_ref[...], preferred_element_type=F32)

    @pl.when(j < 2)
    def _():
        g = gain_ref[pl.ds(j, 1), :]
        for h in range(n_heads_per_tile):
            a = acc[:, h * HEAD_DIM:(h + 1) * HEAD_DIM]
            o_ref[:, h * HEAD_DIM:(h + 1) * HEAD_DIM] = (_rms(a) * g).astype(o_ref.dtype)

    @pl.when(j == 4)
    def _():
        o_ref[...] = (acc * (HEAD_DIM ** -0.5)).astype(o_ref.dtype)

    @pl.when(jnp.logical_and(j >= 2, j != 4))
    def _():
        o_ref[...] = acc.astype(o_ref.dtype)


def inproj_main(hn, w_main, gains):
    S, D = hn.shape
    N = w_main.shape[1]
    tm, tn = min(MM_ROWS, S), MM_COLS
    return pl.pallas_call(
        functools.partial(_inproj_main_kernel, n_heads_per_tile=tn // HEAD_DIM),
        out_shape=jax.ShapeDtypeStruct((S, N), BF16),
        grid=(S // tm, N // tn),
        in_specs=[pl.BlockSpec((tm, D), lambda i, j: (i, 0)),
                  pl.BlockSpec((D, tn), lambda i, j: (0, j)),
                  pl.BlockSpec((8, HEAD_DIM), lambda i, j: (0, 0))],
        out_specs=pl.BlockSpec((tm, tn), lambda i, j: (i, j)),
        compiler_params=pltpu.CompilerParams(
            dimension_semantics=("parallel", "parallel"),
            vmem_limit_bytes=_vmem_limit(2 * (tm * D * 2 + D * tn * 2 + tm * tn * 2) + 2 * tm * tn * 4)),
        name="inproj_main",
    )(hn, w_main, gains)


def _inproj_aux_kernel(a_ref, w_ref, mo_ref, gate_ref):
    acc = jnp.dot(a_ref[...], w_ref[...], preferred_element_type=F32)
    n_mo = mo_ref.shape[1]
    mo_ref[...] = acc[:, :n_mo]
    gate_ref[...] = acc[:, n_mo:]


def inproj_aux(hn, w_aux, n_mo):
    S, D = hn.shape
    N = w_aux.shape[1]
    tm = min(NORM_ROWS, S)
    return pl.pallas_call(
        _inproj_aux_kernel,
        out_shape=(jax.ShapeDtypeStruct((S, n_mo), F32),
                   jax.ShapeDtypeStruct((S, N - n_mo), F32)),
        grid=(S // tm,),
        in_specs=[pl.BlockSpec((tm, D), lambda i: (i, 0)),
                  pl.BlockSpec((D, N), lambda i: (0, 0))],
        out_specs=(pl.BlockSpec((tm, n_mo), lambda i: (i, 0)),
                   pl.BlockSpec((tm, N - n_mo), lambda i: (i, 0))),
        compiler_params=pltpu.CompilerParams(
            dimension_semantics=("parallel",),
            vmem_limit_bytes=_vmem_limit(2 * (tm * D * 2 + D * N * 2 + tm * N * 4) + tm * N * 4)),
        name="inproj_aux",
    )(hn, w_aux)


def _split3_dot(tri, val):
    v1 = val.astype(BF16)
    r1 = val - v1.astype(F32)
    v2 = r1.astype(BF16)
    v3 = (r1 - v2.astype(F32)).astype(BF16)
    out = jnp.dot(tri, v1, preferred_element_type=F32)
    out += jnp.dot(tri, v2, preferred_element_type=F32)
    out += jnp.dot(tri, v3, preferred_element_type=F32)
    return out


def _gates_kernel(g_ref, bias_ref, o_ref, carry_ref, *, n_heads):
    @pl.when(pl.program_id(0) == 0)
    def _():
        carry_ref[...] = jnp.zeros_like(carry_ref)

    rows = g_ref.shape[0]
    z = g_ref[...] + bias_ref[...]
    lane = lax.broadcasted_iota(jnp.int32, z.shape, 1)
    capped = GATE_SOFTCAP * jnp.tanh(z / GATE_SOFTCAP)
    is_fox = lane < n_heads
    is_i = jnp.logical_and(lane >= n_heads, lane < 2 * n_heads)
    is_f = jnp.logical_and(lane >= 2 * n_heads, lane < 3 * n_heads)
    logf = jnp.where(is_fox, _log_sigmoid(z), jnp.where(is_f, _log_sigmoid(capped), 0.0))
    r = lax.broadcasted_iota(jnp.int32, (rows, rows), 0)
    c = lax.broadcasted_iota(jnp.int32, (rows, rows), 1)
    tri = jnp.where(r >= c, 1.0, 0.0).astype(BF16)
    cum = _split3_dot(tri, logf)
    glob = cum + carry_ref[...]
    o_ref[...] = jnp.where(is_fox, glob * LOG2E, jnp.where(is_i, capped, cum))
    carry_ref[...] = glob[rows - 1:rows, :]


def gate_activations(gates_pre, bias_row, n_heads):
    S, W = gates_pre.shape
    tb = MLSTM_CHUNK
    return pl.pallas_call(
        functools.partial(_gates_kernel, n_heads=n_heads),
        out_shape=jax.ShapeDtypeStruct((S, W), F32),
        grid=(S // tb,),
        in_specs=[pl.BlockSpec((tb, W), lambda i: (i, 0)),
                  pl.BlockSpec((1, W), lambda i: (0, 0))],
        out_specs=pl.BlockSpec((tb, W), lambda i: (i, 0)),
        scratch_shapes=[pltpu.VMEM((1, W), F32)],
        compiler_params=pltpu.CompilerParams(dimension_semantics=("arbitrary",)),
        name="gate_activations",
    )(gates_pre, bias_row)


def _fox_kernel(q_ref, k_ref, v_ref, cq_ref, ck_ref, o_ref, m_ref, acc_ref, s_ref, p_ref, alpha_ref, *, n_sub):
    i = pl.program_id(1)
    d = HEAD_DIM
    tk = q_ref.shape[0] // n_sub
    assert n_sub % 2 == 0

    m_ref[...] = jnp.full_like(m_ref, NEG_BIG)
    acc_ref[...] = jnp.zeros_like(acc_ref)
    p_ref[1] = jnp.zeros_like(p_ref[1])
    alpha_ref[1] = jnp.ones_like(alpha_ref[1])
    lane = lax.broadcasted_iota(jnp.int32, (tk, d), 1)
    ones_col = jnp.where(lane == 0, 1.0, 0.0).astype(BF16)
    row = lax.broadcasted_iota(jnp.int32, (tk, tk), 0)
    col = lax.broadcasted_iota(jnp.int32, (tk, tk), 1)
    causal = col <= row

    def qk_stage(sub, j, par):
        start = pl.multiple_of(j * tk, tk)
        s = lax.dot_general(q_ref[pl.ds(sub * tk, tk), :], k_ref[pl.ds(start, tk), :],
                            (((1,), (1,)), ((), ())), preferred_element_type=F32)
        s_ref[par, sub] = s - ck_ref[:, pl.ds(start, tk)]

    def sm_stage(sub, par, masked):
        rows = pl.ds(sub * tk, tk)
        s = s_ref[par, sub]
        if masked:
            s = jnp.where(causal, s, NEG_BIG)
        cq = cq_ref[rows, :]
        m_prev = m_ref[rows, :]
        m_new = jnp.maximum(m_prev, jnp.max(s, axis=-1, keepdims=True) + cq)
        p_ref[par, sub] = jnp.exp2(s - jnp.tile(m_new - cq, (1, tk // V7X_LANES))).astype(BF16)
        alpha_ref[par, rows, :] = jnp.exp2(m_prev - m_new)
        m_ref[rows, :] = m_new

    def pv_stage(sub, j, par):
        rows = pl.ds(sub * tk, tk)
        start = pl.multiple_of(j * tk, tk)
        v_aug = jnp.concatenate([v_ref[pl.ds(start, tk), :], ones_col], axis=1)
        acc_ref[rows, :] = (jnp.tile(alpha_ref[par, rows, :], (1, 2)) * acc_ref[rows, :]
                            + jnp.dot(p_ref[par, sub], v_aug, preferred_element_type=F32))

    n_full = i * n_sub
    for sub in range(n_sub):
        qk_stage(sub, 0, 0)

    def body(tt, carry):
        for par in (0, 1):
            step = 2 * tt + par
            for sub in range(n_sub):
                qk_stage(sub, step + 1, 1 - par)
                sm_stage(sub, par, masked=False)
                pv_stage(sub, jnp.maximum(step - 1, 0), 1 - par)
        return carry

    lax.fori_loop(0, n_full // 2, body, 0)
    for kk in range(n_sub + 1):
        par = kk % 2
        for sub in range(n_sub):
            if kk + 1 <= sub:
                qk_stage(sub, n_full + kk + 1, 1 - par)
            if kk <= sub:
                sm_stage(sub, par, masked=(kk == sub))
            if kk - 1 <= sub:
                pv_stage(sub, jnp.maximum(n_full + kk - 1, 0), 1 - par)
    acc = acc_ref[...]
    o_ref[...] = (acc[:, :d] / acc[:, d:d + 1]).astype(o_ref.dtype)


def fox_attention(proj, cq, ck, n_heads):
    S = proj.shape[0]
    tk = min(ATT_K, S)
    n_sub = max(1, min(ATT_Q, S) // tk)
    tq = n_sub * tk
    H = n_heads
    return pl.pallas_call(
        functools.partial(_fox_kernel, n_sub=n_sub),
        out_shape=jax.ShapeDtypeStruct((S, H * HEAD_DIM), BF16),
        grid=(H, S // tq),
        in_specs=[pl.BlockSpec((tq, HEAD_DIM), lambda h, i: (i, h)),
                  pl.BlockSpec((S, HEAD_DIM), lambda h, i: (0, H + h)),
                  pl.BlockSpec((S, HEAD_DIM), lambda h, i: (0, 2 * H + h)),
                  pl.BlockSpec((None, tq, V7X_LANES), lambda h, i: (h, i, 0)),
                  pl.BlockSpec((None, 1, S), lambda h, i: (h, 0, 0))],
        out_specs=pl.BlockSpec((tq, HEAD_DIM), lambda h, i: (i, h)),
        scratch_shapes=[pltpu.VMEM((tq, V7X_LANES), F32), pltpu.VMEM((tq, 2 * HEAD_DIM), F32),
                        pltpu.VMEM((2, n_sub, tk, tk), F32), pltpu.VMEM((2, n_sub, tk, tk), BF16),
                        pltpu.VMEM((2, tq, V7X_LANES), F32)],
        compiler_params=pltpu.CompilerParams(
            dimension_semantics=("parallel", "arbitrary"),
            vmem_limit_bytes=_vmem_limit(4 * S * HEAD_DIM * 2 + 2 * n_sub * tk * tk * (4 + 2)
                                         + 4 * tk * tk * 4 + 12 * tq * HEAD_DIM * 4 + 16 * S * 4)),
        name="fox_attention",
    )(proj, proj, proj, cq, ck)


def _mlstm_kernel(q_ref, k_ref, v_ref, mo_ref, gcol_ref, grow_ref, gain_ref, o_ref,
                  state_ref, m_ref, *, n_heads):
    L = q_ref.shape[0]
    d = HEAD_DIM

    @pl.when(pl.program_id(0) == 0)
    def _():
        state_ref[...] = jnp.zeros_like(state_ref)
        m_ref[...] = jnp.zeros_like(m_ref)

    row = lax.broadcasted_iota(jnp.int32, (L, L), 0)
    col = lax.broadcasted_iota(jnp.int32, (L, L), 1)
    causal = col <= row
    lane = lax.broadcasted_iota(jnp.int32, (L, d), 1)
    ones_col = jnp.where(lane == 0, 1.0, 0.0).astype(BF16)

    gcol = gcol_ref[...]
    grow = grow_ref[...]
    for h in range(n_heads):
        sl = slice(h * d, (h + 1) * d)
        q = q_ref[:, sl]
        k = k_ref[:, sl]
        v = v_ref[:, sl]
        i_col = gcol[:, n_heads + h:n_heads + h + 1]
        b_col = gcol[:, 2 * n_heads + h:2 * n_heads + h + 1]
        i_row = grow[n_heads + h:n_heads + h + 1, :]
        b_row = grow[2 * n_heads + h:2 * n_heads + h + 1, :]
        m_prev = m_ref[h:h + 1, 0:1]
        state = state_ref[h]

        log_intra = jnp.where(causal, b_col - b_row + i_row, NEG_BIG)
        log_inter = b_col + m_prev
        m_t = jnp.maximum(log_inter, jnp.max(log_intra, axis=-1, keepdims=True))
        w_intra = jnp.exp(log_intra - m_t)
        w_inter = jnp.exp(log_inter - m_t)
        qk = lax.dot_general(q, k, (((1,), (1,)), ((), ())), preferred_element_type=F32) * w_intra
        v_aug = jnp.concatenate([v, ones_col], axis=1)
        tot = jnp.dot(qk.astype(BF16), v_aug, preferred_element_type=F32)
        tot = tot + w_inter * jnp.dot(q, state.astype(BF16), preferred_element_type=F32)
        num = tot[:, :d]
        den = tot[:, d:d + 1]
        hval = num / jnp.maximum(jnp.abs(den), jnp.exp(-m_t))

        b_last = b_col[L - 1:L, :]
        log_w_state = b_last - b_col + i_col
        m_new = jnp.maximum(b_last + m_prev, jnp.max(log_w_state, axis=0, keepdims=True))
        decay = jnp.exp(b_last + m_prev - m_new)
        w_s = jnp.exp(log_w_state - m_new)
        wv = (w_s * v_aug.astype(F32)).astype(BF16)
        upd = lax.dot_general(k, wv, (((0,), (0,)), ((), ())), preferred_element_type=F32)
        state_ref[h] = decay * state + upd
        m_ref[h:h + 1, :] = jnp.broadcast_to(m_new, (1, m_ref.shape[1]))

        hn = _rms(hval) * gain_ref[:, sl]
        o_ref[:, sl] = (jax.nn.sigmoid(mo_ref[:, sl]) * hn).astype(o_ref.dtype)


def mlstm_mixer(proj, mo, gcol, grow, out_gain, n_heads, q_block):
    S = proj.shape[0]
    L = min(MLSTM_CHUNK, S)
    W = n_heads * HEAD_DIM
    return pl.pallas_call(
        functools.partial(_mlstm_kernel, n_heads=n_heads),
        out_shape=jax.ShapeDtypeStruct((S, W), BF16),
        grid=(S // L,),
        in_specs=[pl.BlockSpec((L, W), lambda c: (c, q_block)),
                  pl.BlockSpec((L, W), lambda c: (c, q_block + 1)),
                  pl.BlockSpec((L, W), lambda c: (c, q_block + 2)),
                  pl.BlockSpec((L, W), lambda c: (c, 0)),
                  pl.BlockSpec((L, gcol.shape[1]), lambda c: (c, 0)),
                  pl.BlockSpec((grow.shape[0], L), lambda c: (0, c)),
                  pl.BlockSpec((1, W), lambda c: (0, 0))],
        out_specs=pl.BlockSpec((L, W), lambda c: (c, 0)),
        scratch_shapes=[pltpu.VMEM((n_heads, HEAD_DIM, 2 * HEAD_DIM), F32),
                        pltpu.VMEM((n_heads, V7X_LANES), F32)],
        compiler_params=pltpu.CompilerParams(
            dimension_semantics=("arbitrary",),
            vmem_limit_bytes=_vmem_limit(2 * L * W * (3 * 2 + 4 + 2) + 16 * L * L * 4)),
        name="mlstm",
    )(proj, proj, proj, mo, gcol, grow, out_gain.reshape(1, W))


def _outproj_kernel(a1_ref, a2_ref, w_ref, x_ref, o_ref):
    k1 = a1_ref.shape[1]
    acc = jnp.dot(a1_ref[...], w_ref[:k1, :], preferred_element_type=F32)
    acc += jnp.dot(a2_ref[...], w_ref[k1:, :], preferred_element_type=F32)
    o_ref[...] = x_ref[...] + acc


def outproj_residual(a1, a2, w, x):
    S, K1 = a1.shape
    K2 = a2.shape[1]
    N = w.shape[1]
    tm, tn = min(MM_ROWS, S), MM_COLS
    return pl.pallas_call(
        _outproj_kernel,
        out_shape=jax.ShapeDtypeStruct((S, N), F32),
        grid=(S // tm, N // tn),
        in_specs=[pl.BlockSpec((tm, K1), lambda i, j: (i, 0)),
                  pl.BlockSpec((tm, K2), lambda i, j: (i, 0)),
                  pl.BlockSpec((K1 + K2, tn), lambda i, j: (0, j)),
                  pl.BlockSpec((tm, tn), lambda i, j: (i, j))],
        out_specs=pl.BlockSpec((tm, tn), lambda i, j: (i, j)),
        compiler_params=pltpu.CompilerParams(
            dimension_semantics=("parallel", "parallel"),
            vmem_limit_bytes=_vmem_limit(2 * (tm * (K1 + K2) * 2 + (K1 + K2) * tn * 2 + 2 * tm * tn * 4)
                                         + tm * tn * 4)),
        name="outproj",
    )(a1, a2, w, x)


def _pool_kernel(x_ref, g_ref, w_ref, b_ref, scale_ref, o_ref, carry_ref):
    i = pl.program_id(0)
    tm = x_ref.shape[0]
    gw = w_ref.shape[1]

    @pl.when(i == 0)
    def _():
        carry_ref[...] = jnp.zeros_like(carry_ref)

    x = x_ref[...]
    hn = _rms(x) * g_ref[...]
    t = i * tm + lax.broadcasted_iota(jnp.int32, (tm, 1), 0)
    for g, w in enumerate(POOL_WINDOWS):
        sl = slice(g * gw, (g + 1) * gw)
        hg = hn[:, sl]
        cur = jnp.concatenate([carry_ref[:, sl], hg], axis=0)
        k = 1
        while k < w:
            cur = cur + pltpu.roll(cur, k, axis=0)
            k *= 2
        window_sum = cur[POOL_HALO:, :]
        count = jnp.minimum(t + 1, w).astype(F32)
        pooled = window_sum / count - hg
        y = jnp.dot(pooled.astype(BF16), w_ref[g], preferred_element_type=F32) + b_ref[:, sl]
        o_ref[:, sl] = x[:, sl] + y * scale_ref[:, sl]
    carry_ref[...] = hn[tm - POOL_HALO:, :]


def pool_mixer_residual(x, g, pool_w, pool_b, pool_scale):
    S, D = x.shape
    tm = min(NORM_ROWS, S)
    G, gw, _ = pool_w.shape
    return pl.pallas_call(
        _pool_kernel,
        out_shape=jax.ShapeDtypeStruct((S, D), F32),
        grid=(S // tm,),
        in_specs=[pl.BlockSpec((tm, D), lambda i: (i, 0)),
                  pl.BlockSpec((1, D), lambda i: (0, 0)),
                  pl.BlockSpec((G, gw, gw), lambda i: (0, 0, 0)),
                  pl.BlockSpec((1, D), lambda i: (0, 0)),
                  pl.BlockSpec((1, D), lambda i: (0, 0))],
        out_specs=pl.BlockSpec((tm, D), lambda i: (i, 0)),
        scratch_shapes=[pltpu.VMEM((POOL_HALO, D), F32)],
        compiler_params=pltpu.CompilerParams(
            dimension_semantics=("arbitrary",),
            vmem_limit_bytes=_vmem_limit(4 * tm * D * 4 + 2 * G * gw * gw * 2 + 6 * tm * D * 4)),
        name="pool_mixer",
    )(x, g.reshape(1, D), pool_w, pool_b.reshape(1, D), pool_scale.reshape(1, D))


def _router_kernel(x_ref, g_ref, w_ref, b_ref, hn_ref, route_ref):
    hn = _rms(x_ref[...]) * g_ref[...]
    hn_ref[...] = hn.astype(hn_ref.dtype)
    logits = jnp.dot(hn, w_ref[...], preferred_element_type=F32,
                     precision=lax.Precision.HIGHEST) + b_ref[...]
    lane = lax.broadcasted_iota(jnp.int32, logits.shape, 1).astype(F32)
    n_lanes = float(logits.shape[1])

    def first_argmax(vals):
        top = jnp.max(vals, axis=-1, keepdims=True)
        return top, jnp.min(jnp.where(vals == top, lane, n_lanes), axis=-1, keepdims=True)

    is_group = lane < N_GROUPS
    g_top, g_sel = first_argmax(jnp.where(is_group, logits, NEG_BIG))
    g_w = 1.0 / jnp.sum(jnp.where(is_group, jnp.exp(logits - g_top), 0.0), axis=-1, keepdims=True)
    lo = N_GROUPS + EXPERTS_PER_GROUP * g_sel
    e_logits = jnp.where(jnp.logical_and(lane >= lo, lane < lo + EXPERTS_PER_GROUP), logits, NEG_BIG)
    v1, i1 = first_argmax(e_logits)
    v2, i2 = first_argmax(jnp.where(lane == i1, NEG_BIG, e_logits))
    e21 = jnp.exp(v2 - v1)
    w1 = g_w / (1.0 + e21)
    w2 = g_w * e21 / (1.0 + e21)
    route_ref[...] = jnp.where(lane == 0, i1 - N_GROUPS,
                               jnp.where(lane == 1, i2 - N_GROUPS,
                                         jnp.where(lane == 2, w1, jnp.where(lane == 3, w2, 0.0))))


def router(x, g, w_router, b_router):
    S, D = x.shape
    W = w_router.shape[1]
    tm = min(NORM_ROWS, S)
    return pl.pallas_call(
        _router_kernel,
        out_shape=(jax.ShapeDtypeStruct((S, D), F32), jax.ShapeDtypeStruct((S, W), F32)),
        grid=(S // tm,),
        in_specs=[pl.BlockSpec((tm, D), lambda i: (i, 0)),
                  pl.BlockSpec((1, D), lambda i: (0, 0)),
                  pl.BlockSpec((D, W), lambda i: (0, 0)),
                  pl.BlockSpec((1, W), lambda i: (0, 0))],
        out_specs=(pl.BlockSpec((tm, D), lambda i: (i, 0)),
                   pl.BlockSpec((tm, W), lambda i: (i, 0))),
        compiler_params=pltpu.CompilerParams(
            dimension_semantics=("parallel",),
            vmem_limit_bytes=_vmem_limit(2 * tm * D * 6 + 2 * D * W * 4 + 4 * tm * D * 4)),
        name="router",
    )(x, g.reshape(1, D), w_router, b_router)


CAST_ROWS = 128
GATHER_UNROLL = 8


def _expert_kernel(be_ref, first_ref, next_ref, active_ref, tok_ref, tok_next_ref, hn_hbm, wg_hbm, wu_hbm,
                   wd_hbm, o_ref, xbuf, stage_g, stage_u, stage_d, wg_ref, wu_ref, wd_ref, wsem, gsem,
                   *, layer):
    b = pl.program_id(0)
    n_blocks = pl.num_programs(0)
    R = o_ref.shape[0]
    slot = lax.rem(b, 2)

    def row_copy(idx_ref, s, r):
        return pltpu.make_async_copy(hn_hbm.at[pl.ds(idx_ref[0, r], 1), :], xbuf.at[s, pl.ds(r, 1), :],
                                     gsem.at[s])

    def start_rows_loop(idx_ref, s):
        def issue(r, carry):
            row_copy(idx_ref, s, r).start()
            return carry
        lax.fori_loop(0, R, issue, 0, unroll=GATHER_UNROLL)

    def wait_rows(s):
        pltpu.make_async_copy(hn_hbm.at[pl.ds(0, R), :], xbuf.at[s], gsem.at[s]).wait()

    def weight_copies(e):
        return (pltpu.make_async_copy(wg_hbm.at[layer, e], stage_g, wsem.at[0]),
                pltpu.make_async_copy(wu_hbm.at[layer, e], stage_u, wsem.at[1]),
                pltpu.make_async_copy(wd_hbm.at[layer, e], stage_d, wsem.at[2]))

    @pl.when(b == 0)
    def _():
        for cp in weight_copies(be_ref[0]):
            cp.start()
        start_rows_loop(tok_ref, 0)

    @pl.when(first_ref[b] == 1)
    def _():
        for cp in weight_copies(be_ref[b]):
            cp.wait()
        for stage, dst in ((stage_g, wg_ref), (stage_u, wu_ref), (stage_d, wd_ref)):
            def cast_rows(r, carry, stage=stage, dst=dst):
                rows = pl.ds(pl.multiple_of(r * CAST_ROWS, CAST_ROWS), CAST_ROWS)
                dst[rows, :] = stage[rows, :].astype(BF16)
                return carry
            lax.fori_loop(0, stage.shape[0] // CAST_ROWS, cast_rows, 0)

        @pl.when(next_ref[b] >= 0)
        def _():
            for cp in weight_copies(next_ref[b]):
                cp.start()

    @pl.when(active_ref[b] == 1)
    def _():
        wait_rows(slot)
        x = xbuf[slot].astype(BF16)
        for r in range(R):
            row_copy(tok_next_ref, 1 - slot, r).start()
        a = jnp.dot(x, wg_ref[...], preferred_element_type=F32)
        u = jnp.dot(x, wu_ref[...], preferred_element_type=F32)
        hmid = (a * jax.nn.sigmoid(a) * u).astype(BF16)
        o_ref[...] = jnp.dot(hmid, wd_ref[...], preferred_element_type=F32)

    @pl.when(active_ref[b] == 0)
    def _():
        wait_rows(slot)
        start_rows_loop(tok_next_ref, 1 - slot)
        o_ref[...] = jnp.zeros_like(o_ref)

    @pl.when(b == n_blocks - 1)
    def _():
        wait_rows(1 - slot)


def expert_blocks(block_e, first, next_e, active, row_tok, hn, w_gate, w_up, w_down, layer):
    n_blocks, _, R = row_tok.shape
    D = hn.shape[1]
    Dh = w_gate.shape[3]
    hbm = pl.BlockSpec(memory_space=pl.ANY)
    grid_spec = pltpu.PrefetchScalarGridSpec(
        num_scalar_prefetch=4,
        grid=(n_blocks,),
        in_specs=[pl.BlockSpec((None, 1, R), lambda b, *_: (b, 0, 0), memory_space=pltpu.SMEM),
                  pl.BlockSpec((None, 1, R), lambda b, *_: (jnp.minimum(b + 1, n_blocks - 1), 0, 0),
                               memory_space=pltpu.SMEM),
                  hbm, hbm, hbm, hbm],
        out_specs=pl.BlockSpec((R, D), lambda b, *_: (b, 0)),
        scratch_shapes=[pltpu.VMEM((2, R, D), F32),
                        pltpu.VMEM((D, Dh), F32), pltpu.VMEM((D, Dh), F32), pltpu.VMEM((Dh, D), F32),
                        pltpu.VMEM((D, Dh), BF16), pltpu.VMEM((D, Dh), BF16), pltpu.VMEM((Dh, D), BF16),
                        pltpu.SemaphoreType.DMA((3,)), pltpu.SemaphoreType.DMA((2,))],
    )
    return pl.pallas_call(
        functools.partial(_expert_kernel, layer=layer),
        out_shape=jax.ShapeDtypeStruct((n_blocks * R, D), F32),
        grid_spec=grid_spec,
        compiler_params=pltpu.CompilerParams(
            dimension_semantics=("arbitrary",),
            vmem_limit_bytes=_vmem_limit(3 * D * Dh * (4 + 2) + 4 * R * D * 4 + R * D * 2 + 6 * R * Dh * 4)),
        name="moe_experts",
    )(block_e, first, next_e, active, row_tok, row_tok, hn, w_gate, w_up, w_down)


COMBINE_ROWS = 256


def _combine_kernel(idx_ref, idx_next_ref, x_ref, route_ref, yb_hbm, o_ref, cbuf, gsem):
    i = pl.program_id(0)
    n_tiles = pl.num_programs(0)
    tm = x_ref.shape[0]
    slot = lax.rem(i, 2)

    def row_copy(idx, s, r, k):
        return pltpu.make_async_copy(yb_hbm.at[pl.ds(idx[0, TOP_K * r + k], 1), :],
                                     cbuf.at[s, k, pl.ds(r, 1), :], gsem.at[s])

    def wait_rows(s):
        for k in range(TOP_K):
            pltpu.make_async_copy(yb_hbm.at[pl.ds(0, tm), :], cbuf.at[s, k], gsem.at[s]).wait()

    @pl.when(i == 0)
    def _():
        def issue(r, carry):
            for k in range(TOP_K):
                row_copy(idx_ref, 0, r, k).start()
            return carry
        lax.fori_loop(0, tm, issue, 0, unroll=GATHER_UNROLL)

    wait_rows(slot)
    for r in range(tm):
        for k in range(TOP_K):
            row_copy(idx_next_ref, 1 - slot, r, k).start()
    acc = x_ref[...]
    for k in range(TOP_K):
        acc = acc + route_ref[:, TOP_K + k:TOP_K + k + 1] * cbuf[slot, k]
    o_ref[...] = acc

    @pl.when(i == n_tiles - 1)
    def _():
        wait_rows(1 - slot)


def combine_residual(x, route, dest, yb):
    T, D = x.shape
    tm = min(COMBINE_ROWS, T)
    n_tiles = T // tm
    idx = dest.reshape(n_tiles, 1, tm * TOP_K)
    return pl.pallas_call(
        _combine_kernel,
        out_shape=jax.ShapeDtypeStruct((T, D), F32),
        grid=(n_tiles,),
        in_specs=[pl.BlockSpec((None, 1, tm * TOP_K), lambda i: (i, 0, 0), memory_space=pltpu.SMEM),
                  pl.BlockSpec((None, 1, tm * TOP_K), lambda i: (jnp.minimum(i + 1, n_tiles - 1), 0, 0),
                               memory_space=pltpu.SMEM),
                  pl.BlockSpec((tm, D), lambda i: (i, 0)),
                  pl.BlockSpec((tm, route.shape[1]), lambda i: (i, 0)),
                  pl.BlockSpec(memory_space=pl.ANY)],
        out_specs=pl.BlockSpec((tm, D), lambda i: (i, 0)),
        scratch_shapes=[pltpu.VMEM((2, TOP_K, tm, D), F32), pltpu.SemaphoreType.DMA((2,))],
        compiler_params=pltpu.CompilerParams(
            dimension_semantics=("arbitrary",),
            vmem_limit_bytes=_vmem_limit(2 * TOP_K * tm * D * 4 + 6 * tm * D * 4)),
        name="moe_combine",
    )(idx, idx, x, route, yb)


def moe_residual(x, g, rgw, rgb, rew, reb, w_gate, w_up, w_down, layer):
    T, D = x.shape
    R = MOE_ROWS
    assert TOP_K == 2
    pad = V7X_LANES - N_GROUPS - N_EXPERTS
    w_router = jnp.concatenate([rgw, rew, jnp.zeros((D, pad), F32)], axis=1)
    b_router = jnp.concatenate([rgb, reb, jnp.zeros((pad,), F32)]).reshape(1, V7X_LANES)
    hn, route = router(x, g, w_router, b_router)
    expert_id = route[:, :TOP_K].astype(jnp.int32)

    n_assign = T * TOP_K
    flat_e = expert_id.reshape(-1)
    onehot = (flat_e[:, None] == jnp.arange(N_EXPERTS, dtype=jnp.int32)[None, :]).astype(jnp.int32)
    running = jnp.cumsum(onehot, axis=0)
    counts = running[-1]
    rank = jnp.take_along_axis(running, flat_e[:, None], axis=1)[:, 0] - 1
    padded = (counts + R - 1) // R * R
    pends = jnp.cumsum(padded)
    pstarts = pends - padded
    dest = pstarts[flat_e] + rank
    n_blocks = (n_assign + N_EXPERTS * (R - 1) + R - 1) // R
    flat_tok = jnp.repeat(jnp.arange(T, dtype=jnp.int32), TOP_K)
    row_tok = jnp.zeros((n_blocks * R,), jnp.int32).at[dest].set(flat_tok)

    blk_start = jnp.arange(n_blocks, dtype=jnp.int32) * R
    active = blk_start < pends[-1]
    block_e = jnp.minimum(jnp.sum(blk_start[:, None] >= pends[None, :], axis=1), N_EXPERTS - 1).astype(jnp.int32)
    prev_e = jnp.concatenate([jnp.full((1,), -1, jnp.int32), block_e[:-1]])
    first = jnp.logical_and(active, block_e != prev_e)
    later = lax.cummin(jnp.where(first, block_e, N_EXPERTS)[::-1])[::-1]
    next_e = jnp.concatenate([later[1:], jnp.full((1,), N_EXPERTS, jnp.int32)])
    next_e = jnp.where(next_e >= N_EXPERTS, -1, next_e).astype(jnp.int32)

    yb = expert_blocks(block_e, first.astype(jnp.int32), next_e, active.astype(jnp.int32),
                       row_tok.reshape(n_blocks, 1, R), hn, w_gate, w_up, w_down, layer)
    return combine_residual(x, route, dest.reshape(T, TOP_K), yb)


def fox_mlstm_residual(xt, norm_g, w, fox_f_bias, fox_q_gain, fox_k_gain, mlstm_i_bias, mlstm_f_bias,
                       mlstm_out_gain, w_out):
    S, D = xt.shape
    H = fox_f_bias.shape[0]
    assert mlstm_i_bias.shape[0] == H and 3 * H <= 32
    fw = H * HEAD_DIM
    o_ff = 3 * fw
    o_mq = o_ff + H
    o_mi = o_mq + 3 * fw
    o_mo = o_mi + 2 * H
    w_main = jnp.concatenate([w[:, :o_ff], w[:, o_mq:o_mi]], axis=1).astype(BF16)
    gate_pad = V7X_LANES - 3 * H
    w_aux = jnp.concatenate([w[:, o_mo:], w[:, o_ff:o_mq], w[:, o_mi:o_mo],
                             jnp.zeros((D, gate_pad), F32)], axis=1).astype(BF16)
    gains = jnp.concatenate([fox_q_gain[None] * (LOG2E * HEAD_DIM ** -0.5), fox_k_gain[None],
                             jnp.zeros((6, HEAD_DIM), F32)], axis=0)
    bias_row = jnp.concatenate([fox_f_bias, mlstm_i_bias, mlstm_f_bias,
                                jnp.zeros((gate_pad,), F32)]).reshape(1, V7X_LANES)

    hn = rmsnorm_bf16(xt, norm_g)
    proj = inproj_main(hn, w_main, gains)
    mo, gates_pre = inproj_aux(hn, w_aux, fw)
    gcol = gate_activations(gates_pre, bias_row, H)
    grow = gcol[:, :32].T
    cq = jnp.broadcast_to(grow[:H, :, None], (H, S, V7X_LANES))
    ck = grow[:H].reshape(H, 1, S)
    y_fox = fox_attention(proj, cq, ck, H)
    y_mlstm = mlstm_mixer(proj, mo, gcol, grow, mlstm_out_gain, H, 3)
    return outproj_residual(y_fox, y_mlstm, w_out.astype(BF16), xt)


def kernel(x, norm_mix, norm_ffn, w_in, fox_f_bias, fox_q_gain, fox_k_gain, mlstm_i_bias, mlstm_f_bias,
           mlstm_out_gain, w_out, pool_w, pool_b, pool_scale, router_group_w, router_group_b,
           router_expert_w, router_expert_b, w_gate, w_up, w_down):
    B, S, D = x.shape
    assert B == 1
    depth = norm_mix.shape[0]
    xt = x.reshape(S, D)

    for layer in range(depth):
        j = layer // 2
        if layer % 2 == 0:
            xt = fox_mlstm_residual(xt, norm_mix[layer], w_in[j], fox_f_bias[j], fox_q_gain[j],
                                    fox_k_gain[j], mlstm_i_bias[j], mlstm_f_bias[j], mlstm_out_gain[j],
                                    w_out[j])
        else:
            xt = pool_mixer_residual(xt, norm_mix[layer], pool_w[j].astype(BF16), pool_b[j], pool_scale[j])
        xt = moe_residual(xt, norm_ffn[layer], router_group_w[layer], router_group_b[layer],
                          router_expert_w[layer], router_expert_b[layer],
                          w_gate, w_up, w_down, layer)
    return xt.reshape(B, S, D)
```

```python
import functools

import jax
import jax.numpy as jnp
from jax import lax
from jax.experimental import pallas as pl
from jax.experimental.pallas import tpu as pltpu

F32 = jnp.float32
BF16 = jnp.bfloat16

HEAD_DIM = 128
GATE_SOFTCAP = 15.0
POOL_WINDOWS = (2, 4, 8, 16)
POOL_HALO = 16
N_GROUPS = 4
EXPERTS_PER_GROUP = 8
N_EXPERTS = N_GROUPS * EXPERTS_PER_GROUP
TOP_K = 2
RMS_EPS = 1e-6

V7X_LANES = 128
V7X_VMEM_BYTES = 64 * 1024 * 1024

NORM_ROWS = 512
MM_ROWS = 1024
MM_COLS = 1024
ATT_Q = 1024
ATT_K = 512
MLSTM_CHUNK = 256
MOE_ROWS = 256
NEG_BIG = -1e30
LOG2E = 1.4426950408889634


def _vmem_limit(nbytes):
    return int(min(max(nbytes * 3 // 2, 16 * 1024 * 1024), V7X_VMEM_BYTES - 8 * 1024 * 1024))


def _rms(x, eps=RMS_EPS):
    return x * lax.rsqrt(jnp.mean(x * x, axis=-1, keepdims=True) + eps)


def _log_sigmoid(x):
    return -(jnp.maximum(-x, 0.0) + jnp.log1p(jnp.exp(-jnp.abs(x))))


def _rmsnorm_kernel(x_ref, g_ref, o_ref):
    o_ref[...] = (_rms(x_ref[...]) * g_ref[...]).astype(o_ref.dtype)


def rmsnorm_bf16(x, g):
    S, D = x.shape
    return pl.pallas_call(
        _rmsnorm_kernel,
        out_shape=jax.ShapeDtypeStruct((S, D), BF16),
        grid=(S // NORM_ROWS,),
        in_specs=[pl.BlockSpec((NORM_ROWS, D), lambda i: (i, 0)),
                  pl.BlockSpec((1, D), lambda i: (0, 0))],
        out_specs=pl.BlockSpec((NORM_ROWS, D), lambda i: (i, 0)),
        compiler_params=pltpu.CompilerParams(
            dimension_semantics=("parallel",),
            vmem_limit_bytes=_vmem_limit(2 * NORM_ROWS * D * 6)),
        name="rmsnorm",
    )(x, g.reshape(1, D))


def _inproj_main_kernel(a_ref, w_ref, gain_ref, o_ref, *, n_heads_per_tile):
    j = pl.program_id(1)
    acc = jnp.dot(a_ref[...], w_ref[...], preferred_element_type=F32)

    @pl.when(j < 2)
    def _():
        g = gain_ref[pl.ds(j, 1), :]
        for h in range(n_heads_per_tile):
            a = acc[:, h * HEAD_DIM:(h + 1) * HEAD_DIM]
            o_ref[:, h * HEAD_DIM:(h + 1) * HEAD_DIM] = (_rms(a) * g).astype(o_ref.dtype)

    @pl.when(j == 4)
    def _():
        o_ref[...] = (acc * (HEAD_DIM ** -0.5)).astype(o_ref.dtype)

    @pl.when(jnp.logical_and(j >= 2, j != 4))
    def _():
        o_ref[...] = acc.astype(o_ref.dtype)


def inproj_main(hn, w_main, gains):
    S, D = hn.shape
    N = w_main.shape[1]
    tm, tn = min(MM_ROWS, S), MM_COLS
    return pl.pallas_call(
        functools.partial(_inproj_main_kernel, n_heads_per_tile=tn // HEAD_DIM),
        out_shape=jax.ShapeDtypeStruct((S, N), BF16),
        grid=(S // tm, N // tn),
        in_specs=[pl.BlockSpec((tm, D), lambda i, j: (i, 0)),
                  pl.BlockSpec((D, tn), lambda i, j: (0, j)),
                  pl.BlockSpec((8, HEAD_DIM), lambda i, j: (0, 0))],
        out_specs=pl.BlockSpec((tm, tn), lambda i, j: (i, j)),
        compiler_params=pltpu.CompilerParams(
            dimension_semantics=("parallel", "parallel"),
            vmem_limit_bytes=_vmem_limit(2 * (tm * D * 2 + D * tn * 2 + tm * tn * 2) + 2 * tm * tn * 4)),
        name="inproj_main",
    )(hn, w_main, gains)


def _inproj_aux_kernel(a_ref, w_ref, mo_ref, gate_ref):
    acc = jnp.dot(a_ref[...], w_ref[...], preferred_element_type=F32)
    n_mo = mo_ref.shape[1]
    mo_ref[...] = acc[:, :n_mo]
    gate_ref[...] = acc[:, n_mo:]


def inproj_aux(hn, w_aux, n_mo):
    S, D = hn.shape
    N = w_aux.shape[1]
    tm = min(NORM_ROWS, S)
    return pl.pallas_call(
        _inproj_aux_kernel,
        out_shape=(jax.ShapeDtypeStruct((S, n_mo), F32),
                   jax.ShapeDtypeStruct((S, N - n_mo), F32)),
        grid=(S // tm,),
        in_specs=[pl.BlockSpec((tm, D), lambda i: (i, 0)),
                  pl.BlockSpec((D, N), lambda i: (0, 0))],
        out_specs=(pl.BlockSpec((tm, n_mo), lambda i: (i, 0)),
                   pl.BlockSpec((tm, N - n_mo), lambda i: (i, 0))),
        compiler_params=pltpu.CompilerParams(
            dimension_semantics=("parallel",),
            vmem_limit_bytes=_vmem_limit(2 * (tm * D * 2 + D * N * 2 + tm * N * 4) + tm * N * 4)),
        name="inproj_aux",
    )(hn, w_aux)


def _split3_dot(tri, val):
    v1 = val.astype(BF16)
    r1 = val - v1.astype(F32)
    v2 = r1.astype(BF16)
    v3 = (r1 - v2.astype(F32)).astype(BF16)
    out = jnp.dot(tri, v1, preferred_element_type=F32)
    out += jnp.dot(tri, v2, preferred_element_type=F32)
    out += jnp.dot(tri, v3, preferred_element_type=F32)
    return out


def _gates_kernel(g_ref, bias_ref, o_ref, carry_ref, *, n_heads):
    @pl.when(pl.program_id(0) == 0)
    def _():
        carry_ref[...] = jnp.zeros_like(carry_ref)

    rows = g_ref.shape[0]
    z = g_ref[...] + bias_ref[...]
    lane = lax.broadcasted_iota(jnp.int32, z.shape, 1)
    capped = GATE_SOFTCAP * jnp.tanh(z / GATE_SOFTCAP)
    is_fox = lane < n_heads
    is_i = jnp.logical_and(lane >= n_heads, lane < 2 * n_heads)
    is_f = jnp.logical_and(lane >= 2 * n_heads, lane < 3 * n_heads)
    logf = jnp.where(is_fox, _log_sigmoid(z), jnp.where(is_f, _log_sigmoid(capped), 0.0))
    r = lax.broadcasted_iota(jnp.int32, (rows, rows), 0)
    c = lax.broadcasted_iota(jnp.int32, (rows, rows), 1)
    tri = jnp.where(r >= c, 1.0, 0.0).astype(BF16)
    cum = _split3_dot(tri, logf)
    glob = cum + carry_ref[...]
    o_ref[...] = jnp.where(is_fox, glob * LOG2E, jnp.where(is_i, capped, cum))
    carry_ref[...] = glob[rows - 1:rows, :]


def gate_activations(gates_pre, bias_row, n_heads):
    S, W = gates_pre.shape
    tb = MLSTM_CHUNK
    return pl.pallas_call(
        functools.partial(_gates_kernel, n_heads=n_heads),
        out_shape=jax.ShapeDtypeStruct((S, W), F32),
        grid=(S // tb,),
        in_specs=[pl.BlockSpec((tb, W), lambda i: (i, 0)),
                  pl.BlockSpec((1, W), lambda i: (0, 0))],
        out_specs=pl.BlockSpec((tb, W), lambda i: (i, 0)),
        scratch_shapes=[pltpu.VMEM((1, W), F32)],
        compiler_params=pltpu.CompilerParams(dimension_semantics=("arbitrary",)),
        name="gate_activations",
    )(gates_pre, bias_row)


def _fox_kernel(q_ref, k_ref, v_ref, cq_ref, ck_ref, o_ref, m_ref, acc_ref, s_ref, p_ref, alpha_ref, *, n_sub):
    i = pl.program_id(1)
    d = HEAD_DIM
    tk = q_ref.shape[0] // n_sub
    assert n_sub % 2 == 0

    m_ref[...] = jnp.full_like(m_ref, NEG_BIG)
    acc_ref[...] = jnp.zeros_like(acc_ref)
    p_ref[1] = jnp.zeros_like(p_ref[1])
    alpha_ref[1] = jnp.ones_like(alpha_ref[1])
    lane = lax.broadcasted_iota(jnp.int32, (tk, d), 1)
    ones_col = jnp.where(lane == 0, 1.0, 0.0).astype(BF16)
    row = lax.broadcasted_iota(jnp.int32, (tk, tk), 0)
    col = lax.broadcasted_iota(jnp.int32, (tk, tk), 1)
    causal = col <= row

    def qk_stage(sub, j, par):
        start = pl.multiple_of(j * tk, tk)
        s = lax.dot_general(q_ref[pl.ds(sub * tk, tk), :], k_ref[pl.ds(start, tk), :],
                            (((1,), (1,)), ((), ())), preferred_element_type=F32)
        s_ref[par, sub] = s - ck_ref[:, pl.ds(start, tk)]

    def sm_stage(sub, par, masked):
        rows = pl.ds(sub * tk, tk)
        s = s_ref[par, sub]
        if masked:
            s = jnp.where(causal, s, NEG_BIG)
        cq = cq_ref[rows, :]
        m_prev = m_ref[rows, :]
        m_new = jnp.maximum(m_prev, jnp.max(s, axis=-1, keepdims=True) + cq)
        p_ref[par, sub] = jnp.exp2(s - jnp.tile(m_new - cq, (1, tk // V7X_LANES))).astype(BF16)
        alpha_ref[par, rows, :] = jnp.exp2(m_prev - m_new)
        m_ref[rows, :] = m_new

    def pv_stage(sub, j, par):
        rows = pl.ds(sub * tk, tk)
        start = pl.multiple_of(j * tk, tk)
        v_aug = jnp.concatenate([v_ref[pl.ds(start, tk), :], ones_col], axis=1)
        acc_ref[rows, :] = (jnp.tile(alpha_ref[par, rows, :], (1, 2)) * acc_ref[rows, :]
                            + jnp.dot(p_ref[par, sub], v_aug, preferred_element_type=F32))

    n_full = i * n_sub
    for sub in range(n_sub):
        qk_stage(sub, 0, 0)

    def body(tt, carry):
        for par in (0, 1):
            step = 2 * tt + par
            for sub in range(n_sub):
                qk_stage(sub, step + 1, 1 - par)
                sm_stage(sub, par, masked=False)
                pv_stage(sub, jnp.maximum(step - 1, 0), 1 - par)
        return carry

    lax.fori_loop(0, n_full // 2, body, 0)
    for kk in range(n_sub + 1):
        par = kk % 2
        for sub in range(n_sub):
            if kk + 1 <= sub:
                qk_stage(sub, n_full + kk + 1, 1 - par)
            if kk <= sub:
                sm_stage(sub, par, masked=(kk == sub))
            if kk - 1 <= sub:
                pv_stage(sub, jnp.maximum(n_full + kk - 1, 0), 1 - par)
    acc = acc_ref[...]
    o_ref[...] = (acc[:, :d] / acc[:, d:d + 1]).astype(o_ref.dtype)


def fox_attention(proj, cq, ck, n_heads):
    S = proj.shape[0]
    tk = min(ATT_K, S)
    n_sub = max(1, min(ATT_Q, S) // tk)
    tq = n_sub * tk
    H = n_heads
    return pl.pallas_call(
        functools.partial(_fox_kernel, n_sub=n_sub),
        out_shape=jax.ShapeDtypeStruct((S, H * HEAD_DIM), BF16),
        grid=(H, S // tq),
        in_specs=[pl.BlockSpec((tq, HEAD_DIM), lambda h, i: (i, h)),
                  pl.BlockSpec((S, HEAD_DIM), lambda h, i: (0, H + h)),
                  pl.BlockSpec((S, HEAD_DIM), lambda h, i: (0, 2 * H + h)),
                  pl.BlockSpec((None, tq, V7X_LANES), lambda h, i: (h, i, 0)),
                  pl.BlockSpec((None, 1, S), lambda h, i: (h, 0, 0))],
        out_specs=pl.BlockSpec((tq, HEAD_DIM), lambda h, i: (i, h)),
        scratch_shapes=[pltpu.VMEM((tq, V7X_LANES), F32), pltpu.VMEM((tq, 2 * HEAD_DIM), F32),
                        pltpu.VMEM((2, n_sub, tk, tk), F32), pltpu.VMEM((2, n_sub, tk, tk), BF16),
                        pltpu.VMEM((2, tq, V7X_LANES), F32)],
        compiler_params=pltpu.CompilerParams(
            dimension_semantics=("parallel", "arbitrary"),
            vmem_limit_bytes=_vmem_limit(4 * S * HEAD_DIM * 2 + 2 * n_sub * tk * tk * (4 + 2)
                                         + 4 * tk * tk * 4 + 12 * tq * HEAD_DIM * 4 + 16 * S * 4)),
        name="fox_attention",
    )(proj, proj, proj, cq, ck)


def _mlstm_kernel(q_ref, k_ref, v_ref, mo_ref, gcol_ref, grow_ref, gain_ref, o_ref,
                  state_ref, m_ref, *, n_heads):
    L = q_ref.shape[0]
    d = HEAD_DIM

    @pl.when(pl.program_id(0) == 0)
    def _():
        state_ref[...] = jnp.zeros_like(state_ref)
        m_ref[...] = jnp.zeros_like(m_ref)

    row = lax.broadcasted_iota(jnp.int32, (L, L), 0)
    col = lax.broadcasted_iota(jnp.int32, (L, L), 1)
    causal = col <= row
    lane = lax.broadcasted_iota(jnp.int32, (L, d), 1)
    ones_col = jnp.where(lane == 0, 1.0, 0.0).astype(BF16)

    gcol = gcol_ref[...]
    grow = grow_ref[...]
    for h in range(n_heads):
        sl = slice(h * d, (h + 1) * d)
        q = q_ref[:, sl]
        k = k_ref[:, sl]
        v = v_ref[:, sl]
        i_col = gcol[:, n_heads + h:n_heads + h + 1]
        b_col = gcol[:, 2 * n_heads + h:2 * n_heads + h + 1]
        i_row = grow[n_heads + h:n_heads + h + 1, :]
        b_row = grow[2 * n_heads + h:2 * n_heads + h + 1, :]
        m_prev = m_ref[h:h + 1, 0:1]
        state = state_ref[h]

        log_intra = jnp.where(causal, b_col - b_row + i_row, NEG_BIG)
        log_inter = b_col + m_prev
        m_t = jnp.maximum(log_inter, jnp.max(log_intra, axis=-1, keepdims=True))
        w_intra = jnp.exp(log_intra - m_t)
        w_inter = jnp.exp(log_inter - m_t)
        qk = lax.dot_general(q, k, (((1,), (1,)), ((), ())), preferred_element_type=F32) * w_intra
        v_aug = jnp.concatenate([v, ones_col], axis=1)
        tot = jnp.dot(qk.astype(BF16), v_aug, preferred_element_type=F32)
        tot = tot + w_inter * jnp.dot(q, state.astype(BF16), preferred_element_type=F32)
        num = tot[:, :d]
        den = tot[:, d:d + 1]
        hval = num / jnp.maximum(jnp.abs(den), jnp.exp(-m_t))

        b_last = b_col[L - 1:L, :]
        log_w_state = b_last - b_col + i_col
        m_new = jnp.maximum(b_last + m_prev, jnp.max(log_w_state, axis=0, keepdims=True))
        decay = jnp.exp(b_last + m_prev - m_new)
        w_s = jnp.exp(log_w_state - m_new)
        wv = (w_s * v_aug.astype(F32)).astype(BF16)
        upd = lax.dot_general(k, wv, (((0,), (0,)), ((), ())), preferred_element_type=F32)
        state_ref[h] = decay * state + upd
        m_ref[h:h + 1, :] = jnp.broadcast_to(m_new, (1, m_ref.shape[1]))

        hn = _rms(hval) * gain_ref[:, sl]
        o_ref[:, sl] = (jax.nn.sigmoid(mo_ref[:, sl]) * hn).astype(o_ref.dtype)


def mlstm_mixer(proj, mo, gcol, grow, out_gain, n_heads, q_block):
    S = proj.shape[0]
    L = min(MLSTM_CHUNK, S)
    W = n_heads * HEAD_DIM
    return pl.pallas_call(
        functools.partial(_mlstm_kernel, n_heads=n_heads),
        out_shape=jax.ShapeDtypeStruct((S, W), BF16),
        grid=(S // L,),
        in_specs=[pl.BlockSpec((L, W), lambda c: (c, q_block)),
                  pl.BlockSpec((L, W), lambda c: (c, q_block + 1)),
                  pl.BlockSpec((L, W), lambda c: (c, q_block + 2)),
                  pl.BlockSpec((L, W), lambda c: (c, 0)),
                  pl.BlockSpec((L, gcol.shape[1]), lambda c: (c, 0)),
                  pl.BlockSpec((grow.shape[0], L), lambda c: (0, c)),
                  pl.BlockSpec((1, W), lambda c: (0, 0))],
        out_specs=pl.BlockSpec((L, W), lambda c: (c, 0)),
        scratch_shapes=[pltpu.VMEM((n_heads, HEAD_DIM, 2 * HEAD_DIM), F32),
                        pltpu.VMEM((n_heads, V7X_LANES), F32)],
        compiler_params=pltpu.CompilerParams(
            dimension_semantics=("arbitrary",),
            vmem_limit_bytes=_vmem_limit(2 * L * W * (3 * 2 + 4 + 2) + 16 * L * L * 4)),
        name="mlstm",
    )(proj, proj, proj, mo, gcol, grow, out_gain.reshape(1, W))


def _outproj_kernel(a1_ref, a2_ref, w_ref, x_ref, o_ref):
    k1 = a1_ref.shape[1]
    acc = jnp.dot(a1_ref[...], w_ref[:k1, :], preferred_element_type=F32)
    acc += jnp.dot(a2_ref[...], w_ref[k1:, :], preferred_element_type=F32)
    o_ref[...] = x_ref[...] + acc


def outproj_residual(a1, a2, w, x):
    S, K1 = a1.shape
    K2 = a2.shape[1]
    N = w.shape[1]
    tm, tn = min(MM_ROWS, S), MM_COLS
    return pl.pallas_call(
        _outproj_kernel,
        out_shape=jax.ShapeDtypeStruct((S, N), F32),
        grid=(S // tm, N // tn),
        in_specs=[pl.BlockSpec((tm, K1), lambda i, j: (i, 0)),
                  pl.BlockSpec((tm, K2), lambda i, j: (i, 0)),
                  pl.BlockSpec((K1 + K2, tn), lambda i, j: (0, j)),
                  pl.BlockSpec((tm, tn), lambda i, j: (i, j))],
        out_specs=pl.BlockSpec((tm, tn), lambda i, j: (i, j)),
        compiler_params=pltpu.CompilerParams(
            dimension_semantics=("parallel", "parallel"),
            vmem_limit_bytes=_vmem_limit(2 * (tm * (K1 + K2) * 2 + (K1 + K2) * tn * 2 + 2 * tm * tn * 4)
                                         + tm * tn * 4)),
        name="outproj",
    )(a1, a2, w, x)


def _pool_kernel(x_ref, g_ref, w_ref, b_ref, scale_ref, o_ref, carry_ref):
    i = pl.program_id(0)
    tm = x_ref.shape[0]
    gw = w_ref.shape[1]

    @pl.when(i == 0)
    def _():
        carry_ref[...] = jnp.zeros_like(carry_ref)

    x = x_ref[...]
    hn = _rms(x) * g_ref[...]
    t = i * tm + lax.broadcasted_iota(jnp.int32, (tm, 1), 0)
    for g, w in enumerate(POOL_WINDOWS):
        sl = slice(g * gw, (g + 1) * gw)
        hg = hn[:, sl]
        cur = jnp.concatenate([carry_ref[:, sl], hg], axis=0)
        k = 1
        while k < w:
            cur = cur + pltpu.roll(cur, k, axis=0)
            k *= 2
        window_sum = cur[POOL_HALO:, :]
        count = jnp.minimum(t + 1, w).astype(F32)
        pooled = window_sum / count - hg
        y = jnp.dot(pooled.astype(BF16), w_ref[g], preferred_element_type=F32) + b_ref[:, sl]
        o_ref[:, sl] = x[:, sl] + y * scale_ref[:, sl]
    carry_ref[...] = hn[tm - POOL_HALO:, :]


def pool_mixer_residual(x, g, pool_w, pool_b, pool_scale):
    S, D = x.shape
    tm = min(NORM_ROWS, S)
    G, gw, _ = pool_w.shape
    return pl.pallas_call(
        _pool_kernel,
        out_shape=jax.ShapeDtypeStruct((S, D), F32),
        grid=(S // tm,),
        in_specs=[pl.BlockSpec((tm, D), lambda i: (i, 0)),
                  pl.BlockSpec((1, D), lambda i: (0, 0)),
                  pl.BlockSpec((G, gw, gw), lambda i: (0, 0, 0)),
                  pl.BlockSpec((1, D), lambda i: (0, 0)),
                  pl.BlockSpec((1, D), lambda i: (0, 0))],
        out_specs=pl.BlockSpec((tm, D), lambda i: (i, 0)),
        scratch_shapes=[pltpu.VMEM((POOL_HALO, D), F32)],
        compiler_params=pltpu.CompilerParams(
            dimension_semantics=("arbitrary",),
            vmem_limit_bytes=_vmem_limit(4 * tm * D * 4 + 2 * G * gw * gw * 2 + 6 * tm * D * 4)),
        name="pool_mixer",
    )(x, g.reshape(1, D), pool_w, pool_b.reshape(1, D), pool_scale.reshape(1, D))


def _store_slabs(ref, val):
    n = val.shape[0]
    n_chunks = val.shape[1] // V7X_LANES
    for c in range(n_chunks):
        ref[pl.ds(c, n, stride=n_chunks), :] = val[:, c * V7X_LANES:(c + 1) * V7X_LANES]


def _load_slab_chunk(ref, lead, c, n, n_chunks):
    return ref[lead + (pl.ds(c, n, stride=n_chunks), slice(None))]


def _router_kernel(x_ref, g_ref, w_ref, b_ref, hn_ref, route_ref):
    hn = _rms(x_ref[...]) * g_ref[...]
    _store_slabs(hn_ref, hn)
    logits = jnp.dot(hn, w_ref[...], preferred_element_type=F32,
                     precision=lax.Precision.HIGHEST) + b_ref[...]
    lane = lax.broadcasted_iota(jnp.int32, logits.shape, 1).astype(F32)
    n_lanes = float(logits.shape[1])

    def first_argmax(vals):
        top = jnp.max(vals, axis=-1, keepdims=True)
        return top, jnp.min(jnp.where(vals == top, lane, n_lanes), axis=-1, keepdims=True)

    is_group = lane < N_GROUPS
    g_top, g_sel = first_argmax(jnp.where(is_group, logits, NEG_BIG))
    g_w = 1.0 / jnp.sum(jnp.where(is_group, jnp.exp(logits - g_top), 0.0), axis=-1, keepdims=True)
    lo = N_GROUPS + EXPERTS_PER_GROUP * g_sel
    e_logits = jnp.where(jnp.logical_and(lane >= lo, lane < lo + EXPERTS_PER_GROUP), logits, NEG_BIG)
    v1, i1 = first_argmax(e_logits)
    v2, i2 = first_argmax(jnp.where(lane == i1, NEG_BIG, e_logits))
    e21 = jnp.exp(v2 - v1)
    w1 = g_w / (1.0 + e21)
    w2 = g_w * e21 / (1.0 + e21)
    route_ref[...] = jnp.where(lane == 0, i1 - N_GROUPS,
                               jnp.where(lane == 1, i2 - N_GROUPS,
                                         jnp.where(lane == 2, w1, jnp.where(lane == 3, w2, 0.0))))


def router(x, g, w_router, b_router):
    S, D = x.shape
    W = w_router.shape[1]
    tm = min(NORM_ROWS, S)
    n_chunks = D // V7X_LANES
    return pl.pallas_call(
        _router_kernel,
        out_shape=(jax.ShapeDtypeStruct((S * n_chunks, V7X_LANES), F32), jax.ShapeDtypeStruct((S, W), F32)),
        grid=(S // tm,),
        in_specs=[pl.BlockSpec((tm, D), lambda i: (i, 0)),
                  pl.BlockSpec((1, D), lambda i: (0, 0)),
                  pl.BlockSpec((D, W), lambda i: (0, 0)),
                  pl.BlockSpec((1, W), lambda i: (0, 0))],
        out_specs=(pl.BlockSpec((tm * n_chunks, V7X_LANES), lambda i: (i, 0)),
                   pl.BlockSpec((tm, W), lambda i: (i, 0))),
        compiler_params=pltpu.CompilerParams(
            dimension_semantics=("parallel",),
            vmem_limit_bytes=_vmem_limit(2 * tm * D * 8 + 2 * D * W * 4 + 4 * tm * D * 4)),
        name="router",
    )(x, g.reshape(1, D), w_router, b_router)


CAST_ROWS = 128
GATHER_UNROLL = 8


def _expert_kernel(be_ref, first_ref, next_ref, active_ref, tok_ref, tok_next_ref, hn_hbm, wg_hbm, wu_hbm,
                   wd_hbm, o_ref, xbuf, stage_g, stage_u, stage_d, wg_ref, wu_ref, wd_ref, wsem, gsem,
                   *, layer):
    b = pl.program_id(0)
    n_blocks = pl.num_programs(0)
    n_chunks = wg_ref.shape[0] // V7X_LANES
    R = o_ref.shape[0] // n_chunks
    slot = lax.rem(b, 2)

    def row_copy(idx_ref, s, r):
        src = pl.multiple_of(idx_ref[0, r] * n_chunks, n_chunks)
        return pltpu.make_async_copy(hn_hbm.at[pl.ds(src, n_chunks), :],
                                     xbuf.at[s, pl.ds(r * n_chunks, n_chunks), :], gsem.at[s])

    def start_rows_loop(idx_ref, s):
        def issue(r, carry):
            row_copy(idx_ref, s, r).start()
            return carry
        lax.fori_loop(0, R, issue, 0, unroll=GATHER_UNROLL)

    def wait_rows(s):
        pltpu.make_async_copy(hn_hbm.at[pl.ds(0, R * n_chunks), :], xbuf.at[s], gsem.at[s]).wait()

    def weight_copies(e):
        return (pltpu.make_async_copy(wg_hbm.at[layer, e], stage_g, wsem.at[0]),
                pltpu.make_async_copy(wu_hbm.at[layer, e], stage_u, wsem.at[1]),
                pltpu.make_async_copy(wd_hbm.at[layer, e], stage_d, wsem.at[2]))

    @pl.when(b == 0)
    def _():
        for cp in weight_copies(be_ref[0]):
            cp.start()
        start_rows_loop(tok_ref, 0)

    @pl.when(first_ref[b] == 1)
    def _():
        for cp in weight_copies(be_ref[b]):
            cp.wait()
        for stage, dst in ((stage_g, wg_ref), (stage_u, wu_ref), (stage_d, wd_ref)):
            def cast_rows(r, carry, stage=stage, dst=dst):
                rows = pl.ds(pl.multiple_of(r * CAST_ROWS, CAST_ROWS), CAST_ROWS)
                dst[rows, :] = stage[rows, :].astype(BF16)
                return carry
            lax.fori_loop(0, stage.shape[0] // CAST_ROWS, cast_rows, 0)

        @pl.when(next_ref[b] >= 0)
        def _():
            for cp in weight_copies(next_ref[b]):
                cp.start()

    @pl.when(active_ref[b] == 1)
    def _():
        wait_rows(slot)
        x = jnp.concatenate([_load_slab_chunk(xbuf, (slot,), c, R, n_chunks).astype(BF16)
                             for c in range(n_chunks)], axis=1)
        for r in range(R):
            row_copy(tok_next_ref, 1 - slot, r).start()
        a = jnp.dot(x, wg_ref[...], preferred_element_type=F32)
        u = jnp.dot(x, wu_ref[...], preferred_element_type=F32)
        hmid = (a * jax.nn.sigmoid(a) * u).astype(BF16)
        _store_slabs(o_ref, jnp.dot(hmid, wd_ref[...], preferred_element_type=F32))

    @pl.when(active_ref[b] == 0)
    def _():
        wait_rows(slot)
        start_rows_loop(tok_next_ref, 1 - slot)
        o_ref[...] = jnp.zeros_like(o_ref)

    @pl.when(b == n_blocks - 1)
    def _():
        wait_rows(1 - slot)


def expert_blocks(block_e, first, next_e, active, row_tok, hn, w_gate, w_up, w_down, layer):
    n_blocks, _, R = row_tok.shape
    D, Dh = w_gate.shape[2:]
    n_chunks = D // V7X_LANES
    hbm = pl.BlockSpec(memory_space=pl.ANY)
    grid_spec = pltpu.PrefetchScalarGridSpec(
        num_scalar_prefetch=4,
        grid=(n_blocks,),
        in_specs=[pl.BlockSpec((None, 1, R), lambda b, *_: (b, 0, 0), memory_space=pltpu.SMEM),
                  pl.BlockSpec((None, 1, R), lambda b, *_: (jnp.minimum(b + 1, n_blocks - 1), 0, 0),
                               memory_space=pltpu.SMEM),
                  hbm, hbm, hbm, hbm],
        out_specs=pl.BlockSpec((R * n_chunks, V7X_LANES), lambda b, *_: (b, 0)),
        scratch_shapes=[pltpu.VMEM((2, R * n_chunks, V7X_LANES), F32),
                        pltpu.VMEM((D, Dh), F32), pltpu.VMEM((D, Dh), F32), pltpu.VMEM((Dh, D), F32),
                        pltpu.VMEM((D, Dh), BF16), pltpu.VMEM((D, Dh), BF16), pltpu.VMEM((Dh, D), BF16),
                        pltpu.SemaphoreType.DMA((3,)), pltpu.SemaphoreType.DMA((2,))],
    )
    return pl.pallas_call(
        functools.partial(_expert_kernel, layer=layer),
        out_shape=jax.ShapeDtypeStruct((n_blocks * R * n_chunks, V7X_LANES), F32),
        grid_spec=grid_spec,
        compiler_params=pltpu.CompilerParams(
            dimension_semantics=("arbitrary",),
            vmem_limit_bytes=_vmem_limit(3 * D * Dh * (4 + 2) + 4 * R * D * 4 + R * D * 2 + 6 * R * Dh * 4)),
        name="moe_experts",
    )(block_e, first, next_e, active, row_tok, row_tok, hn, w_gate, w_up, w_down)


COMBINE_ROWS = 256


def _combine_kernel(idx_ref, idx_next_ref, x_ref, route_ref, yb_hbm, o_ref, cbuf, gsem):
    i = pl.program_id(0)
    n_tiles = pl.num_programs(0)
    tm, D = x_ref.shape
    n_chunks = D // V7X_LANES
    slot = lax.rem(i, 2)

    def row_copy(idx, s, r, k):
        src = pl.multiple_of(idx[0, TOP_K * r + k] * n_chunks, n_chunks)
        return pltpu.make_async_copy(yb_hbm.at[pl.ds(src, n_chunks), :],
                                     cbuf.at[s, k, pl.ds(r * n_chunks, n_chunks), :], gsem.at[s])

    def wait_rows(s):
        for k in range(TOP_K):
            pltpu.make_async_copy(yb_hbm.at[pl.ds(0, tm * n_chunks), :], cbuf.at[s, k], gsem.at[s]).wait()

    @pl.when(i == 0)
    def _():
        def issue(r, carry):
            for k in range(TOP_K):
                row_copy(idx_ref, 0, r, k).start()
            return carry
        lax.fori_loop(0, tm, issue, 0, unroll=GATHER_UNROLL)

    wait_rows(slot)
    for r in range(tm):
        for k in range(TOP_K):
            row_copy(idx_next_ref, 1 - slot, r, k).start()
    for c in range(n_chunks):
        cols = slice(c * V7X_LANES, (c + 1) * V7X_LANES)
        acc = x_ref[:, cols]
        for k in range(TOP_K):
            acc = acc + route_ref[:, TOP_K + k:TOP_K + k + 1] * _load_slab_chunk(cbuf, (slot, k), c, tm, n_chunks)
        o_ref[:, cols] = acc

    @pl.when(i == n_tiles - 1)
    def _():
        wait_rows(1 - slot)


def combine_residual(x, route, dest, yb):
    T, D = x.shape
    tm = min(COMBINE_ROWS, T)
    n_tiles = T // tm
    idx = dest.reshape(n_tiles, 1, tm * TOP_K)
    return pl.pallas_call(
        _combine_kernel,
        out_shape=jax.ShapeDtypeStruct((T, D), F32),
        grid=(n_tiles,),
        in_specs=[pl.BlockSpec((None, 1, tm * TOP_K), lambda i: (i, 0, 0), memory_space=pltpu.SMEM),
                  pl.BlockSpec((None, 1, tm * TOP_K), lambda i: (jnp.minimum(i + 1, n_tiles - 1), 0, 0),
                               memory_space=pltpu.SMEM),
                  pl.BlockSpec((tm, D), lambda i: (i, 0)),
                  pl.BlockSpec((tm, route.shape[1]), lambda i: (i, 0)),
                  pl.BlockSpec(memory_space=pl.ANY)],
        out_specs=pl.BlockSpec((tm, D), lambda i: (i, 0)),
        scratch_shapes=[pltpu.VMEM((2, TOP_K, tm * (D // V7X_LANES), V7X_LANES), F32),
                        pltpu.SemaphoreType.DMA((2,))],
        compiler_params=pltpu.CompilerParams(
            dimension_semantics=("arbitrary",),
            vmem_limit_bytes=_vmem_limit(2 * TOP_K * tm * D * 4 + 6 * tm * D * 4)),
        name="moe_combine",
    )(idx, idx, x, route, yb)


def moe_residual(x, g, rgw, rgb, rew, reb, w_gate, w_up, w_down, layer):
    T, D = x.shape
    R = MOE_ROWS
    assert TOP_K == 2
    pad = V7X_LANES - N_GROUPS - N_EXPERTS
    w_router = jnp.concatenate([rgw, rew, jnp.zeros((D, pad), F32)], axis=1)
    b_router = jnp.concatenate([rgb, reb, jnp.zeros((pad,), F32)]).reshape(1, V7X_LANES)
    hn, route = router(x, g, w_router, b_router)
    expert_id = route[:, :TOP_K].astype(jnp.int32)

    n_assign = T * TOP_K
    flat_e = expert_id.reshape(-1)
    onehot = (flat_e[:, None] == jnp.arange(N_EXPERTS, dtype=jnp.int32)[None, :]).astype(jnp.int32)
    running = jnp.cumsum(onehot, axis=0)
    counts = running[-1]
    rank = jnp.take_along_axis(running, flat_e[:, None], axis=1)[:, 0] - 1
    padded = (counts + R - 1) // R * R
    pends = jnp.cumsum(padded)
    pstarts = pends - padded
    dest = pstarts[flat_e] + rank
    n_blocks = (n_assign + N_EXPERTS * (R - 1) + R - 1) // R
    flat_tok = jnp.repeat(jnp.arange(T, dtype=jnp.int32), TOP_K)
    row_tok = jnp.zeros((n_blocks * R,), jnp.int32).at[dest].set(flat_tok)

    blk_start = jnp.arange(n_blocks, dtype=jnp.int32) * R
    active = blk_start < pends[-1]
    block_e = jnp.minimum(jnp.sum(blk_start[:, None] >= pends[None, :], axis=1), N_EXPERTS - 1).astype(jnp.int32)
    prev_e = jnp.concatenate([jnp.full((1,), -1, jnp.int32), block_e[:-1]])
    first = jnp.logical_and(active, block_e != prev_e)
    later = lax.cummin(jnp.where(first, block_e, N_EXPERTS)[::-1])[::-1]
    next_e = jnp.concatenate([later[1:], jnp.full((1,), N_EXPERTS, jnp.int32)])
    next_e = jnp.where(next_e >= N_EXPERTS, -1, next_e).astype(jnp.int32)

    yb = expert_blocks(block_e, first.astype(jnp.int32), next_e, active.astype(jnp.int32),
                       row_tok.reshape(n_blocks, 1, R), hn, w_gate, w_up, w_down, layer)
    return combine_residual(x, route, dest.reshape(T, TOP_K), yb)


def fox_mlstm_residual(xt, norm_g, w, fox_f_bias, fox_q_gain, fox_k_gain, mlstm_i_bias, mlstm_f_bias,
                       mlstm_out_gain, w_out):
    S, D = xt.shape
    H = fox_f_bias.shape[0]
    assert mlstm_i_bias.shape[0] == H and 3 * H <= 32
    fw = H * HEAD_DIM
    o_ff = 3 * fw
    o_mq = o_ff + H
    o_mi = o_mq + 3 * fw
    o_mo = o_mi + 2 * H
    w_main = jnp.concatenate([w[:, :o_ff], w[:, o_mq:o_mi]], axis=1).astype(BF16)
    gate_pad = V7X_LANES - 3 * H
    w_aux = jnp.concatenate([w[:, o_mo:], w[:, o_ff:o_mq], w[:, o_mi:o_mo],
                             jnp.zeros((D, gate_pad), F32)], axis=1).astype(BF16)
    gains = jnp.concatenate([fox_q_gain[None] * (LOG2E * HEAD_DIM ** -0.5), fox_k_gain[None],
                             jnp.zeros((6, HEAD_DIM), F32)], axis=0)
    bias_row = jnp.concatenate([fox_f_bias, mlstm_i_bias, mlstm_f_bias,
                                jnp.zeros((gate_pad,), F32)]).reshape(1, V7X_LANES)

    hn = rmsnorm_bf16(xt, norm_g)
    proj = inproj_main(hn, w_main, gains)
    mo, gates_pre = inproj_aux(hn, w_aux, fw)
    gcol = gate_activations(gates_pre, bias_row, H)
    grow = gcol[:, :32].T
    cq = jnp.broadcast_to(grow[:H, :, None], (H, S, V7X_LANES))
    ck = grow[:H].reshape(H, 1, S)
    y_fox = fox_attention(proj, cq, ck, H)
    y_mlstm = mlstm_mixer(proj, mo, gcol, grow, mlstm_out_gain, H, 3)
    return outproj_residual(y_fox, y_mlstm, w_out.astype(BF16), xt)


def kernel(x, norm_mix, norm_ffn, w_in, fox_f_bias, fox_q_gain, fox_k_gain, mlstm_i_bias, mlstm_f_bias,
           mlstm_out_gain, w_out, pool_w, pool_b, pool_scale, router_group_w, router_group_b,
           router_expert_w, router_expert_b, w_gate, w_up, w_down):
    B, S, D = x.shape
    assert B == 1
    depth = norm_mix.shape[0]
    xt = x.reshape(S, D)

    for layer in range(depth):
        j = layer // 2
        if layer % 2 == 0:
            xt = fox_mlstm_residual(xt, norm_mix[layer], w_in[j], fox_f_bias[j], fox_q_gain[j],
                                    fox_k_gain[j], mlstm_i_bias[j], mlstm_f_bias[j], mlstm_out_gain[j],
                                    w_out[j])
        else:
            xt = pool_mixer_residual(xt, norm_mix[layer], pool_w[j].astype(BF16), pool_b[j], pool_scale[j])
        xt = moe_residual(xt, norm_ffn[layer], router_group_w[layer], router_group_b[layer],
                          router_expert_w[layer], router_expert_b[layer],
                          w_gate, w_up, w_down, layer)
    return xt.reshape(B, S, D)
```

```python
import functools

import jax
import jax.numpy as jnp
from jax import lax
from jax.experimental import pallas as pl
from jax.experimental.pallas import tpu as pltpu

F32 = jnp.float32
BF16 = jnp.bfloat16

HEAD_DIM = 128
GATE_SOFTCAP = 15.0
POOL_WINDOWS = (2, 4, 8, 16)
POOL_HALO = 16
N_GROUPS = 4
EXPERTS_PER_GROUP = 8
N_EXPERTS = N_GROUPS * EXPERTS_PER_GROUP
TOP_K = 2
RMS_EPS = 1e-6

V7X_LANES = 128
V7X_VMEM_BYTES = 64 * 1024 * 1024

NORM_ROWS = 512
MM_ROWS = 1024
MM_COLS = 1024
ATT_Q = 1024
ATT_K = 512
MLSTM_CHUNK = 256
MOE_ROWS = 256
NEG_BIG = -1e30
LOG2E = 1.4426950408889634


def _vmem_limit(nbytes):
    return int(min(max(nbytes * 3 // 2, 16 * 1024 * 1024), V7X_VMEM_BYTES - 8 * 1024 * 1024))


def _rms(x, eps=RMS_EPS):
    return x * lax.rsqrt(jnp.mean(x * x, axis=-1, keepdims=True) + eps)


def _log_sigmoid(x):
    return -(jnp.maximum(-x, 0.0) + jnp.log1p(jnp.exp(-jnp.abs(x))))


def _rmsnorm_kernel(x_ref, g_ref, o_ref):
    o_ref[...] = (_rms(x_ref[...]) * g_ref[...]).astype(o_ref.dtype)


def rmsnorm_bf16(x, g):
    S, D = x.shape
    return pl.pallas_call(
        _rmsnorm_kernel,
        out_shape=jax.ShapeDtypeStruct((S, D), BF16),
        grid=(S // NORM_ROWS,),
        in_specs=[pl.BlockSpec((NORM_ROWS, D), lambda i: (i, 0)),
                  pl.BlockSpec((1, D), lambda i: (0, 0))],
        out_specs=pl.BlockSpec((NORM_ROWS, D), lambda i: (i, 0)),
        compiler_params=pltpu.CompilerParams(
            dimension_semantics=("parallel",),
            vmem_limit_bytes=_vmem_limit(2 * NORM_ROWS * D * 6)),
        name="rmsnorm",
    )(x, g.reshape(1, D))


def _inproj_main_kernel(a_ref, w_ref, gain_ref, o_ref, *, n_heads_per_tile):
    j = pl.program_id(1)
    acc = jnp.dot(a_ref[...], w_ref[...], preferred_element_type=F32)

    @pl.when(j < 2)
    def _():
        g = gain_ref[pl.ds(j, 1), :]
        for h in range(n_heads_per_tile):
            a = acc[:, h * HEAD_DIM:(h + 1) * HEAD_DIM]
            o_ref[:, h * HEAD_DIM:(h + 1) * HEAD_DIM] = (_rms(a) * g).astype(o_ref.dtype)

    @pl.when(j == 4)
    def _():
        o_ref[...] = (acc * (HEAD_DIM ** -0.5)).astype(o_ref.dtype)

    @pl.when(jnp.logical_and(j >= 2, j != 4))
    def _():
        o_ref[...] = acc.astype(o_ref.dtype)


def inproj_main(hn, w_main, gains):
    S, D = hn.shape
    N = w_main.shape[1]
    tm, tn = min(MM_ROWS, S), MM_COLS
    return pl.pallas_call(
        functools.partial(_inproj_main_kernel, n_heads_per_tile=tn // HEAD_DIM),
        out_shape=jax.ShapeDtypeStruct((S, N), BF16),
        grid=(S // tm, N // tn),
        in_specs=[pl.BlockSpec((tm, D), lambda i, j: (i, 0)),
                  pl.BlockSpec((D, tn), lambda i, j: (0, j)),
                  pl.BlockSpec((8, HEAD_DIM), lambda i, j: (0, 0))],
        out_specs=pl.BlockSpec((tm, tn), lambda i, j: (i, j)),
        compiler_params=pltpu.CompilerParams(
            dimension_semantics=("parallel", "parallel"),
            vmem_limit_bytes=_vmem_limit(2 * (tm * D * 2 + D * tn * 2 + tm * tn * 2) + 2 * tm * tn * 4)),
        name="inproj_main",
    )(hn, w_main, gains)


def _inproj_aux_kernel(a_ref, w_ref, mo_ref, gate_ref):
    acc = jnp.dot(a_ref[...], w_ref[...], preferred_element_type=F32)
    n_mo = mo_ref.shape[1]
    mo_ref[...] = acc[:, :n_mo]
    gate_ref[...] = acc[:, n_mo:]


def inproj_aux(hn, w_aux, n_mo):
    S, D = hn.shape
    N = w_aux.shape[1]
    tm = min(NORM_ROWS, S)
    return pl.pallas_call(
        _inproj_aux_kernel,
        out_shape=(jax.ShapeDtypeStruct((S, n_mo), F32),
                   jax.ShapeDtypeStruct((S, N - n_mo), F32)),
        grid=(S // tm,),
        in_specs=[pl.BlockSpec((tm, D), lambda i: (i, 0)),
                  pl.BlockSpec((D, N), lambda i: (0, 0))],
        out_specs=(pl.BlockSpec((tm, n_mo), lambda i: (i, 0)),
                   pl.BlockSpec((tm, N - n_mo), lambda i: (i, 0))),
        compiler_params=pltpu.CompilerParams(
            dimension_semantics=("parallel",),
            vmem_limit_bytes=_vmem_limit(2 * (tm * D * 2 + D * N * 2 + tm * N * 4) + tm * N * 4)),
        name="inproj_aux",
    )(hn, w_aux)


def _split3_dot(tri, val):
    v1 = val.astype(BF16)
    r1 = val - v1.astype(F32)
    v2 = r1.astype(BF16)
    v3 = (r1 - v2.astype(F32)).astype(BF16)
    out = jnp.dot(tri, v1, preferred_element_type=F32)
    out += jnp.dot(tri, v2, preferred_element_type=F32)
    out += jnp.dot(tri, v3, preferred_element_type=F32)
    return out


def _gates_kernel(g_ref, bias_ref, o_ref, carry_ref, *, n_heads):
    @pl.when(pl.program_id(0) == 0)
    def _():
        carry_ref[...] = jnp.zeros_like(carry_ref)

    rows = g_ref.shape[0]
    z = g_ref[...] + bias_ref[...]
    lane = lax.broadcasted_iota(jnp.int32, z.shape, 1)
    capped = GATE_SOFTCAP * jnp.tanh(z / GATE_SOFTCAP)
    is_fox = lane < n_heads
    is_i = jnp.logical_and(lane >= n_heads, lane < 2 * n_heads)
    is_f = jnp.logical_and(lane >= 2 * n_heads, lane < 3 * n_heads)
    logf = jnp.where(is_fox, _log_sigmoid(z), jnp.where(is_f, _log_sigmoid(capped), 0.0))
    r = lax.broadcasted_iota(jnp.int32, (rows, rows), 0)
    c = lax.broadcasted_iota(jnp.int32, (rows, rows), 1)
    tri = jnp.where(r >= c, 1.0, 0.0).astype(BF16)
    cum = _split3_dot(tri, logf)
    glob = cum + carry_ref[...]
    o_ref[...] = jnp.where(is_fox, glob * LOG2E, jnp.where(is_i, capped, cum))
    carry_ref[...] = glob[rows - 1:rows, :]


def gate_activations(gates_pre, bias_row, n_heads):
    S, W = gates_pre.shape
    tb = MLSTM_CHUNK
    return pl.pallas_call(
        functools.partial(_gates_kernel, n_heads=n_heads),
        out_shape=jax.ShapeDtypeStruct((S, W), F32),
        grid=(S // tb,),
        in_specs=[pl.BlockSpec((tb, W), lambda i: (i, 0)),
                  pl.BlockSpec((1, W), lambda i: (0, 0))],
        out_specs=pl.BlockSpec((tb, W), lambda i: (i, 0)),
        scratch_shapes=[pltpu.VMEM((1, W), F32)],
        compiler_params=pltpu.CompilerParams(dimension_semantics=("arbitrary",)),
        name="gate_activations",
    )(gates_pre, bias_row)


def _fox_kernel(q_ref, k_ref, v_ref, cq_ref, ck_ref, o_ref, m_ref, acc_ref, s_ref, p_ref, alpha_ref, *, n_sub):
    i = pl.program_id(1)
    d = HEAD_DIM
    tk = q_ref.shape[0] // n_sub
    assert n_sub % 2 == 0

    m_ref[...] = jnp.full_like(m_ref, NEG_BIG)
    acc_ref[...] = jnp.zeros_like(acc_ref)
    p_ref[1] = jnp.zeros_like(p_ref[1])
    alpha_ref[1] = jnp.ones_like(alpha_ref[1])
    lane = lax.broadcasted_iota(jnp.int32, (tk, d), 1)
    ones_col = jnp.where(lane == 0, 1.0, 0.0).astype(BF16)
    row = lax.broadcasted_iota(jnp.int32, (tk, tk), 0)
    col = lax.broadcasted_iota(jnp.int32, (tk, tk), 1)
    causal = col <= row

    def qk_stage(sub, j, par):
        start = pl.multiple_of(j * tk, tk)
        s = lax.dot_general(q_ref[pl.ds(sub * tk, tk), :], k_ref[pl.ds(start, tk), :],
                            (((1,), (1,)), ((), ())), preferred_element_type=F32)
        s_ref[par, sub] = s - ck_ref[:, pl.ds(start, tk)]

    def sm_stage(sub, par, masked):
        rows = pl.ds(sub * tk, tk)
        s = s_ref[par, sub]
        if masked:
            s = jnp.where(causal, s, NEG_BIG)
        cq = cq_ref[rows, :]
        m_prev = m_ref[rows, :]
        m_new = jnp.maximum(m_prev, jnp.max(s, axis=-1, keepdims=True) + cq)
        p_ref[par, sub] = jnp.exp2(s - jnp.tile(m_new - cq, (1, tk // V7X_LANES))).astype(BF16)
        alpha_ref[par, rows, :] = jnp.exp2(m_prev - m_new)
        m_ref[rows, :] = m_new

    def pv_stage(sub, j, par):
        rows = pl.ds(sub * tk, tk)
        start = pl.multiple_of(j * tk, tk)
        v_aug = jnp.concatenate([v_ref[pl.ds(start, tk), :], ones_col], axis=1)
        acc_ref[rows, :] = (jnp.tile(alpha_ref[par, rows, :], (1, 2)) * acc_ref[rows, :]
                            + jnp.dot(p_ref[par, sub], v_aug, preferred_element_type=F32))

    n_full = i * n_sub
    for sub in range(n_sub):
        qk_stage(sub, 0, 0)

    def body(tt, carry):
        for par in (0, 1):
            step = 2 * tt + par
            for sub in range(n_sub):
                qk_stage(sub, step + 1, 1 - par)
                sm_stage(sub, par, masked=False)
                pv_stage(sub, jnp.maximum(step - 1, 0), 1 - par)
        return carry

    lax.fori_loop(0, n_full // 2, body, 0)
    for kk in range(n_sub + 1):
        par = kk % 2
        for sub in range(n_sub):
            if kk + 1 <= sub:
                qk_stage(sub, n_full + kk + 1, 1 - par)
            if kk <= sub:
                sm_stage(sub, par, masked=(kk == sub))
            if kk - 1 <= sub:
                pv_stage(sub, jnp.maximum(n_full + kk - 1, 0), 1 - par)
    acc = acc_ref[...]
    o_ref[...] = (acc[:, :d] / acc[:, d:d + 1]).astype(o_ref.dtype)


def fox_attention(proj, cq, ck, n_heads):
    S = proj.shape[0]
    tk = min(ATT_K, S)
    n_sub = max(1, min(ATT_Q, S) // tk)
    tq = n_sub * tk
    H = n_heads
    return pl.pallas_call(
        functools.partial(_fox_kernel, n_sub=n_sub),
        out_shape=jax.ShapeDtypeStruct((S, H * HEAD_DIM), BF16),
        grid=(H, S // tq),
        in_specs=[pl.BlockSpec((tq, HEAD_DIM), lambda h, i: (i, h)),
                  pl.BlockSpec((S, HEAD_DIM), lambda h, i: (0, H + h)),
                  pl.BlockSpec((S, HEAD_DIM), lambda h, i: (0, 2 * H + h)),
                  pl.BlockSpec((None, tq, V7X_LANES), lambda h, i: (h, i, 0)),
                  pl.BlockSpec((None, 1, S), lambda h, i: (h, 0, 0))],
        out_specs=pl.BlockSpec((tq, HEAD_DIM), lambda h, i: (i, h)),
        scratch_shapes=[pltpu.VMEM((tq, V7X_LANES), F32), pltpu.VMEM((tq, 2 * HEAD_DIM), F32),
                        pltpu.VMEM((2, n_sub, tk, tk), F32), pltpu.VMEM((2, n_sub, tk, tk), BF16),
                        pltpu.VMEM((2, tq, V7X_LANES), F32)],
        compiler_params=pltpu.CompilerParams(
            dimension_semantics=("parallel", "arbitrary"),
            vmem_limit_bytes=_vmem_limit(4 * S * HEAD_DIM * 2 + 2 * n_sub * tk * tk * (4 + 2)
                                         + 4 * tk * tk * 4 + 12 * tq * HEAD_DIM * 4 + 16 * S * 4)),
        name="fox_attention",
    )(proj, proj, proj, cq, ck)


def _mlstm_kernel(q_ref, k_ref, v_ref, mo_ref, gcol_ref, grow_ref, gain_ref, o_ref,
                  state_ref, m_ref, *, n_heads):
    L = q_ref.shape[0]
    d = HEAD_DIM

    @pl.when(pl.program_id(0) == 0)
    def _():
        state_ref[...] = jnp.zeros_like(state_ref)
        m_ref[...] = jnp.zeros_like(m_ref)

    row = lax.broadcasted_iota(jnp.int32, (L, L), 0)
    col = lax.broadcasted_iota(jnp.int32, (L, L), 1)
    causal = col <= row
    lane = lax.broadcasted_iota(jnp.int32, (L, d), 1)
    ones_col = jnp.where(lane == 0, 1.0, 0.0).astype(BF16)

    gcol = gcol_ref[...]
    grow = grow_ref[...]
    for h in range(n_heads):
        sl = slice(h * d, (h + 1) * d)
        q = q_ref[:, sl]
        k = k_ref[:, sl]
        v = v_ref[:, sl]
        i_col = gcol[:, n_heads + h:n_heads + h + 1]
        b_col = gcol[:, 2 * n_heads + h:2 * n_heads + h + 1]
        i_row = grow[n_heads + h:n_heads + h + 1, :]
        b_row = grow[2 * n_heads + h:2 * n_heads + h + 1, :]
        m_prev = m_ref[h:h + 1, 0:1]
        state = state_ref[h]

        log_intra = jnp.where(causal, b_col - b_row + i_row, NEG_BIG)
        log_inter = b_col + m_prev
        m_t = jnp.maximum(log_inter, jnp.max(log_intra, axis=-1, keepdims=True))
        w_intra = jnp.exp(log_intra - m_t)
        w_inter = jnp.exp(log_inter - m_t)
        qk = lax.dot_general(q, k, (((1,), (1,)), ((), ())), preferred_element_type=F32) * w_intra
        v_aug = jnp.concatenate([v, ones_col], axis=1)
        tot = jnp.dot(qk.astype(BF16), v_aug, preferred_element_type=F32)
        tot = tot + w_inter * jnp.dot(q, state.astype(BF16), preferred_element_type=F32)
        num = tot[:, :d]
        den = tot[:, d:d + 1]
        hval = num / jnp.maximum(jnp.abs(den), jnp.exp(-m_t))

        b_last = b_col[L - 1:L, :]
        log_w_state = b_last - b_col + i_col
        m_new = jnp.maximum(b_last + m_prev, jnp.max(log_w_state, axis=0, keepdims=True))
        decay = jnp.exp(b_last + m_prev - m_new)
        w_s = jnp.exp(log_w_state - m_new)
        wv = (w_s * v_aug.astype(F32)).astype(BF16)
        upd = lax.dot_general(k, wv, (((0,), (0,)), ((), ())), preferred_element_type=F32)
        state_ref[h] = decay * state + upd
        m_ref[h:h + 1, :] = jnp.broadcast_to(m_new, (1, m_ref.shape[1]))

        hn = _rms(hval) * gain_ref[:, sl]
        o_ref[:, sl] = (jax.nn.sigmoid(mo_ref[:, sl]) * hn).astype(o_ref.dtype)


def mlstm_mixer(proj, mo, gcol, grow, out_gain, n_heads, q_block):
    S = proj.shape[0]
    L = min(MLSTM_CHUNK, S)
    W = n_heads * HEAD_DIM
    return pl.pallas_call(
        functools.partial(_mlstm_kernel, n_heads=n_heads),
        out_shape=jax.ShapeDtypeStruct((S, W), BF16),
        grid=(S // L,),
        in_specs=[pl.BlockSpec((L, W), lambda c: (c, q_block)),
                  pl.BlockSpec((L, W), lambda c: (c, q_block + 1)),
                  pl.BlockSpec((L, W), lambda c: (c, q_block + 2)),
                  pl.BlockSpec((L, W), lambda c: (c, 0)),
                  pl.BlockSpec((L, gcol.shape[1]), lambda c: (c, 0)),
                  pl.BlockSpec((grow.shape[0], L), lambda c: (0, c)),
                  pl.BlockSpec((1, W), lambda c: (0, 0))],
        out_specs=pl.BlockSpec((L, W), lambda c: (c, 0)),
        scratch_shapes=[pltpu.VMEM((n_heads, HEAD_DIM, 2 * HEAD_DIM), F32),
                        pltpu.VMEM((n_heads, V7X_LANES), F32)],
        compiler_params=pltpu.CompilerParams(
            dimension_semantics=("arbitrary",),
            vmem_limit_bytes=_vmem_limit(2 * L * W * (3 * 2 + 4 + 2) + 16 * L * L * 4)),
        name="mlstm",
    )(proj, proj, proj, mo, gcol, grow, out_gain.reshape(1, W))


def _outproj_kernel(a1_ref, a2_ref, w_ref, x_ref, o_ref):
    k1 = a1_ref.shape[1]
    acc = jnp.dot(a1_ref[...], w_ref[:k1, :], preferred_element_type=F32)
    acc += jnp.dot(a2_ref[...], w_ref[k1:, :], preferred_element_type=F32)
    o_ref[...] = x_ref[...] + acc


def outproj_residual(a1, a2, w, x):
    S, K1 = a1.shape
    K2 = a2.shape[1]
    N = w.shape[1]
    tm, tn = min(MM_ROWS, S), MM_COLS
    return pl.pallas_call(
        _outproj_kernel,
        out_shape=jax.ShapeDtypeStruct((S, N), F32),
        grid=(S // tm, N // tn),
        in_specs=[pl.BlockSpec((tm, K1), lambda i, j: (i, 0)),
                  pl.BlockSpec((tm, K2), lambda i, j: (i, 0)),
                  pl.BlockSpec((K1 + K2, tn), lambda i, j: (0, j)),
                  pl.BlockSpec((tm, tn), lambda i, j: (i, j))],
        out_specs=pl.BlockSpec((tm, tn), lambda i, j: (i, j)),
        compiler_params=pltpu.CompilerParams(
            dimension_semantics=("parallel", "parallel"),
            vmem_limit_bytes=_vmem_limit(2 * (tm * (K1 + K2) * 2 + (K1 + K2) * tn * 2 + 2 * tm * tn * 4)
                                         + tm * tn * 4)),
        name="outproj",
    )(a1, a2, w, x)


def _pool_kernel(x_ref, g_ref, w_ref, b_ref, scale_ref, o_ref, carry_ref):
    i = pl.program_id(0)
    tm = x_ref.shape[0]
    gw = w_ref.shape[1]

    @pl.when(i == 0)
    def _():
        carry_ref[...] = jnp.zeros_like(carry_ref)

    x = x_ref[...]
    hn = _rms(x) * g_ref[...]
    t = i * tm + lax.broadcasted_iota(jnp.int32, (tm, 1), 0)
    for g, w in enumerate(POOL_WINDOWS):
        sl = slice(g * gw, (g + 1) * gw)
        hg = hn[:, sl]
        cur = jnp.concatenate([carry_ref[:, sl], hg], axis=0)
        k = 1
        while k < w:
            cur = cur + pltpu.roll(cur, k, axis=0)
            k *= 2
        window_sum = cur[POOL_HALO:, :]
        count = jnp.minimum(t + 1, w).astype(F32)
        pooled = window_sum / count - hg
        y = jnp.dot(pooled.astype(BF16), w_ref[g], preferred_element_type=F32) + b_ref[:, sl]
        o_ref[:, sl] = x[:, sl] + y * scale_ref[:, sl]
    carry_ref[...] = hn[tm - POOL_HALO:, :]


def pool_mixer_residual(x, g, pool_w, pool_b, pool_scale):
    S, D = x.shape
    tm = min(NORM_ROWS, S)
    G, gw, _ = pool_w.shape
    return pl.pallas_call(
        _pool_kernel,
        out_shape=jax.ShapeDtypeStruct((S, D), F32),
        grid=(S // tm,),
        in_specs=[pl.BlockSpec((tm, D), lambda i: (i, 0)),
                  pl.BlockSpec((1, D), lambda i: (0, 0)),
                  pl.BlockSpec((G, gw, gw), lambda i: (0, 0, 0)),
                  pl.BlockSpec((1, D), lambda i: (0, 0)),
                  pl.BlockSpec((1, D), lambda i: (0, 0))],
        out_specs=pl.BlockSpec((tm, D), lambda i: (i, 0)),
        scratch_shapes=[pltpu.VMEM((POOL_HALO, D), F32)],
        compiler_params=pltpu.CompilerParams(
            dimension_semantics=("arbitrary",),
            vmem_limit_bytes=_vmem_limit(4 * tm * D * 4 + 2 * G * gw * gw * 2 + 6 * tm * D * 4)),
        name="pool_mixer",
    )(x, g.reshape(1, D), pool_w, pool_b.reshape(1, D), pool_scale.reshape(1, D))


def _store_slabs(ref, val):
    n = val.shape[0]
    n_chunks = val.shape[1] // V7X_LANES
    for c in range(n_chunks):
        ref[pl.ds(c, n, stride=n_chunks), :] = val[:, c * V7X_LANES:(c + 1) * V7X_LANES]


def _load_slab_chunk(ref, lead, c, n, n_chunks):
    return ref[lead + (pl.ds(c, n, stride=n_chunks), slice(None))]


def _router_kernel(x_ref, g_ref, w_ref, b_ref, hn_ref, route_ref):
    hn = _rms(x_ref[...]) * g_ref[...]
    _store_slabs(hn_ref, hn)
    logits = jnp.dot(hn, w_ref[...], preferred_element_type=F32,
                     precision=lax.Precision.HIGHEST) + b_ref[...]
    lane = lax.broadcasted_iota(jnp.int32, logits.shape, 1).astype(F32)
    n_lanes = float(logits.shape[1])

    def first_argmax(vals):
        top = jnp.max(vals, axis=-1, keepdims=True)
        return top, jnp.min(jnp.where(vals == top, lane, n_lanes), axis=-1, keepdims=True)

    is_group = lane < N_GROUPS
    g_top, g_sel = first_argmax(jnp.where(is_group, logits, NEG_BIG))
    g_w = 1.0 / jnp.sum(jnp.where(is_group, jnp.exp(logits - g_top), 0.0), axis=-1, keepdims=True)
    lo = N_GROUPS + EXPERTS_PER_GROUP * g_sel
    e_logits = jnp.where(jnp.logical_and(lane >= lo, lane < lo + EXPERTS_PER_GROUP), logits, NEG_BIG)
    v1, i1 = first_argmax(e_logits)
    v2, i2 = first_argmax(jnp.where(lane == i1, NEG_BIG, e_logits))
    e21 = jnp.exp(v2 - v1)
    w1 = g_w / (1.0 + e21)
    w2 = g_w * e21 / (1.0 + e21)
    route_ref[...] = jnp.where(lane == 0, i1 - N_GROUPS,
                               jnp.where(lane == 1, i2 - N_GROUPS,
                                         jnp.where(lane == 2, w1, jnp.where(lane == 3, w2, 0.0))))


def router(x, g, w_router, b_router):
    S, D = x.shape
    W = w_router.shape[1]
    tm = min(NORM_ROWS, S)
    n_chunks = D // V7X_LANES
    return pl.pallas_call(
        _router_kernel,
        out_shape=(jax.ShapeDtypeStruct((S * n_chunks, V7X_LANES), F32), jax.ShapeDtypeStruct((S, W), F32)),
        grid=(S // tm,),
        in_specs=[pl.BlockSpec((tm, D), lambda i: (i, 0)),
                  pl.BlockSpec((1, D), lambda i: (0, 0)),
                  pl.BlockSpec((D, W), lambda i: (0, 0)),
                  pl.BlockSpec((1, W), lambda i: (0, 0))],
        out_specs=(pl.BlockSpec((tm * n_chunks, V7X_LANES), lambda i: (i, 0)),
                   pl.BlockSpec((tm, W), lambda i: (i, 0))),
        compiler_params=pltpu.CompilerParams(
            dimension_semantics=("parallel",),
            vmem_limit_bytes=_vmem_limit(2 * tm * D * 8 + 2 * D * W * 4 + 4 * tm * D * 4)),
        name="router",
    )(x, g.reshape(1, D), w_router, b_router)


CAST_ROWS = 128
GATHER_UNROLL = 8
WEIGHT_DMA_PRIORITY = 1


def _expert_kernel(be_ref, first_ref, next_ref, active_ref, tok_ref, tok_next_ref, hn_hbm, wg_hbm, wu_hbm,
                   wd_hbm, o_ref, xbuf, stage_g, stage_u, stage_d, wg_ref, wu_ref, wd_ref, wsem, gsem,
                   *, layer):
    b = pl.program_id(0)
    n_blocks = pl.num_programs(0)
    n_chunks = wg_ref.shape[0] // V7X_LANES
    R = o_ref.shape[0] // n_chunks
    slot = lax.rem(b, 2)

    def row_copy(idx_ref, s, r):
        src = pl.multiple_of(idx_ref[0, r] * n_chunks, n_chunks)
        return pltpu.make_async_copy(hn_hbm.at[pl.ds(src, n_chunks), :],
                                     xbuf.at[s, pl.ds(r * n_chunks, n_chunks), :], gsem.at[s])

    def start_rows_loop(idx_ref, s):
        def issue(r, carry):
            row_copy(idx_ref, s, r).start()
            return carry
        lax.fori_loop(0, R, issue, 0, unroll=GATHER_UNROLL)

    def wait_rows(s):
        pltpu.make_async_copy(hn_hbm.at[pl.ds(0, R * n_chunks), :], xbuf.at[s], gsem.at[s]).wait()

    def weight_copies(e):
        return (pltpu.make_async_copy(wg_hbm.at[layer, e], stage_g, wsem.at[0]),
                pltpu.make_async_copy(wu_hbm.at[layer, e], stage_u, wsem.at[1]),
                pltpu.make_async_copy(wd_hbm.at[layer, e], stage_d, wsem.at[2]))

    @pl.when(b == 0)
    def _():
        start_rows_loop(tok_ref, 0)
        for cp in weight_copies(be_ref[0]):
            cp.start(priority=WEIGHT_DMA_PRIORITY)

    @pl.when(first_ref[b] == 1)
    def _():
        for cp in weight_copies(be_ref[b]):
            cp.wait()
        for stage, dst in ((stage_g, wg_ref), (stage_u, wu_ref), (stage_d, wd_ref)):
            def cast_rows(r, carry, stage=stage, dst=dst):
                rows = pl.ds(pl.multiple_of(r * CAST_ROWS, CAST_ROWS), CAST_ROWS)
                dst[rows, :] = stage[rows, :].astype(BF16)
                return carry
            lax.fori_loop(0, stage.shape[0] // CAST_ROWS, cast_rows, 0)

        @pl.when(next_ref[b] >= 0)
        def _():
            for cp in weight_copies(next_ref[b]):
                cp.start(priority=WEIGHT_DMA_PRIORITY)

    @pl.when(active_ref[b] == 1)
    def _():
        wait_rows(slot)
        x = jnp.concatenate([_load_slab_chunk(xbuf, (slot,), c, R, n_chunks).astype(BF16)
                             for c in range(n_chunks)], axis=1)
        for r in range(R):
            row_copy(tok_next_ref, 1 - slot, r).start()
        a = jnp.dot(x, wg_ref[...], preferred_element_type=F32)
        u = jnp.dot(x, wu_ref[...], preferred_element_type=F32)
        hmid = (a * jax.nn.sigmoid(a) * u).astype(BF16)
        _store_slabs(o_ref, jnp.dot(hmid, wd_ref[...], preferred_element_type=F32))

    @pl.when(active_ref[b] == 0)
    def _():
        wait_rows(slot)
        start_rows_loop(tok_next_ref, 1 - slot)
        o_ref[...] = jnp.zeros_like(o_ref)

    @pl.when(b == n_blocks - 1)
    def _():
        wait_rows(1 - slot)


def expert_blocks(block_e, first, next_e, active, row_tok, hn, w_gate, w_up, w_down, layer):
    n_blocks, _, R = row_tok.shape
    D, Dh = w_gate.shape[2:]
    n_chunks = D // V7X_LANES
    hbm = pl.BlockSpec(memory_space=pl.ANY)
    grid_spec = pltpu.PrefetchScalarGridSpec(
        num_scalar_prefetch=4,
        grid=(n_blocks,),
        in_specs=[pl.BlockSpec((None, 1, R), lambda b, *_: (b, 0, 0), memory_space=pltpu.SMEM),
                  pl.BlockSpec((None, 1, R), lambda b, *_: (jnp.minimum(b + 1, n_blocks - 1), 0, 0),
                               memory_space=pltpu.SMEM),
                  hbm, hbm, hbm, hbm],
        out_specs=pl.BlockSpec((R * n_chunks, V7X_LANES), lambda b, *_: (b, 0)),
        scratch_shapes=[pltpu.VMEM((2, R * n_chunks, V7X_LANES), F32),
                        pltpu.VMEM((D, Dh), F32), pltpu.VMEM((D, Dh), F32), pltpu.VMEM((Dh, D), F32),
                        pltpu.VMEM((D, Dh), BF16), pltpu.VMEM((D, Dh), BF16), pltpu.VMEM((Dh, D), BF16),
                        pltpu.SemaphoreType.DMA((3,)), pltpu.SemaphoreType.DMA((2,))],
    )
    return pl.pallas_call(
        functools.partial(_expert_kernel, layer=layer),
        out_shape=jax.ShapeDtypeStruct((n_blocks * R * n_chunks, V7X_LANES), F32),
        grid_spec=grid_spec,
        compiler_params=pltpu.CompilerParams(
            dimension_semantics=("arbitrary",),
            vmem_limit_bytes=_vmem_limit(3 * D * Dh * (4 + 2) + 4 * R * D * 4 + R * D * 2 + 6 * R * Dh * 4)),
        name="moe_experts",
    )(block_e, first, next_e, active, row_tok, row_tok, hn, w_gate, w_up, w_down)


COMBINE_ROWS = 256


def _combine_kernel(idx_ref, idx_next_ref, x_ref, route_ref, yb_hbm, o_ref, cbuf, gsem):
    i = pl.program_id(0)
    n_tiles = pl.num_programs(0)
    tm, D = x_ref.shape
    n_chunks = D // V7X_LANES
    slot = lax.rem(i, 2)

    def row_copy(idx, s, r, k):
        src = pl.multiple_of(idx[0, TOP_K * r + k] * n_chunks, n_chunks)
        return pltpu.make_async_copy(yb_hbm.at[pl.ds(src, n_chunks), :],
                                     cbuf.at[s, k, pl.ds(r * n_chunks, n_chunks), :], gsem.at[s])

    def wait_rows(s):
        for k in range(TOP_K):
            pltpu.make_async_copy(yb_hbm.at[pl.ds(0, tm * n_chunks), :], cbuf.at[s, k], gsem.at[s]).wait()

    @pl.when(i == 0)
    def _():
        def issue(r, carry):
            for k in range(TOP_K):
                row_copy(idx_ref, 0, r, k).start()
            return carry
        lax.fori_loop(0, tm, issue, 0, unroll=GATHER_UNROLL)

    wait_rows(slot)
    for r in range(tm):
        for k in range(TOP_K):
            row_copy(idx_next_ref, 1 - slot, r, k).start(priority=k % 2)
    for c in range(n_chunks):
        cols = slice(c * V7X_LANES, (c + 1) * V7X_LANES)
        acc = x_ref[:, cols]
        for k in range(TOP_K):
            acc = acc + route_ref[:, TOP_K + k:TOP_K + k + 1] * _load_slab_chunk(cbuf, (slot, k), c, tm, n_chunks)
        o_ref[:, cols] = acc

    @pl.when(i == n_tiles - 1)
    def _():
        wait_rows(1 - slot)


def combine_residual(x, route, dest, yb):
    T, D = x.shape
    tm = min(COMBINE_ROWS, T)
    n_tiles = T // tm
    idx = dest.reshape(n_tiles, 1, tm * TOP_K)
    return pl.pallas_call(
        _combine_kernel,
        out_shape=jax.ShapeDtypeStruct((T, D), F32),
        grid=(n_tiles,),
        in_specs=[pl.BlockSpec((None, 1, tm * TOP_K), lambda i: (i, 0, 0), memory_space=pltpu.SMEM),
                  pl.BlockSpec((None, 1, tm * TOP_K), lambda i: (jnp.minimum(i + 1, n_tiles - 1), 0, 0),
                               memory_space=pltpu.SMEM),
                  pl.BlockSpec((tm, D), lambda i: (i, 0)),
                  pl.BlockSpec((tm, route.shape[1]), lambda i: (i, 0)),
                  pl.BlockSpec(memory_space=pl.ANY)],
        out_specs=pl.BlockSpec((tm, D), lambda i: (i, 0)),
        scratch_shapes=[pltpu.VMEM((2, TOP_K, tm * (D // V7X_LANES), V7X_LANES), F32),
                        pltpu.SemaphoreType.DMA((2,))],
        compiler_params=pltpu.CompilerParams(
            dimension_semantics=("arbitrary",),
            vmem_limit_bytes=_vmem_limit(2 * TOP_K * tm * D * 4 + 6 * tm * D * 4)),
        name="moe_combine",
    )(idx, idx, x, route, yb)


def moe_residual(x, g, rgw, rgb, rew, reb, w_gate, w_up, w_down, layer):
    T, D = x.shape
    R = MOE_ROWS
    assert TOP_K == 2
    pad = V7X_LANES - N_GROUPS - N_EXPERTS
    w_router = jnp.concatenate([rgw, rew, jnp.zeros((D, pad), F32)], axis=1)
    b_router = jnp.concatenate([rgb, reb, jnp.zeros((pad,), F32)]).reshape(1, V7X_LANES)
    hn, route = router(x, g, w_router, b_router)
    expert_id = route[:, :TOP_K].astype(jnp.int32)

    n_assign = T * TOP_K
    flat_e = expert_id.reshape(-1)
    onehot = (flat_e[:, None] == jnp.arange(N_EXPERTS, dtype=jnp.int32)[None, :]).astype(jnp.int32)
    running = jnp.cumsum(onehot, axis=0)
    counts = running[-1]
    rank = jnp.take_along_axis(running, flat_e[:, None], axis=1)[:, 0] - 1
    padded = (counts + R - 1) // R * R
    pends = jnp.cumsum(padded)
    pstarts = pends - padded
    dest = pstarts[flat_e] + rank
    n_blocks = (n_assign + N_EXPERTS * (R - 1) + R - 1) // R
    flat_tok = jnp.repeat(jnp.arange(T, dtype=jnp.int32), TOP_K)
    row_tok = jnp.zeros((n_blocks * R,), jnp.int32).at[dest].set(flat_tok)

    blk_start = jnp.arange(n_blocks, dtype=jnp.int32) * R
    active = blk_start < pends[-1]
    block_e = jnp.minimum(jnp.sum(blk_start[:, None] >= pends[None, :], axis=1), N_EXPERTS - 1).astype(jnp.int32)
    prev_e = jnp.concatenate([jnp.full((1,), -1, jnp.int32), block_e[:-1]])
    first = jnp.logical_and(active, block_e != prev_e)
    later = lax.cummin(jnp.where(first, block_e, N_EXPERTS)[::-1])[::-1]
    next_e = jnp.concatenate([later[1:], jnp.full((1,), N_EXPERTS, jnp.int32)])
    next_e = jnp.where(next_e >= N_EXPERTS, -1, next_e).astype(jnp.int32)

    yb = expert_blocks(block_e, first.astype(jnp.int32), next_e, active.astype(jnp.int32),
                       row_tok.reshape(n_blocks, 1, R), hn, w_gate, w_up, w_down, layer)
    return combine_residual(x, route, dest.reshape(T, TOP_K), yb)


def fox_mlstm_residual(xt, norm_g, w, fox_f_bias, fox_q_gain, fox_k_gain, mlstm_i_bias, mlstm_f_bias,
                       mlstm_out_gain, w_out):
    S, D = xt.shape
    H = fox_f_bias.shape[0]
    assert mlstm_i_bias.shape[0] == H and 3 * H <= 32
    fw = H * HEAD_DIM
    o_ff = 3 * fw
    o_mq = o_ff + H
    o_mi = o_mq + 3 * fw
    o_mo = o_mi + 2 * H
    w_main = jnp.concatenate([w[:, :o_ff], w[:, o_mq:o_mi]], axis=1).astype(BF16)
    gate_pad = V7X_LANES - 3 * H
    w_aux = jnp.concatenate([w[:, o_mo:], w[:, o_ff:o_mq], w[:, o_mi:o_mo],
                             jnp.zeros((D, gate_pad), F32)], axis=1).astype(BF16)
    gains = jnp.concatenate([fox_q_gain[None] * (LOG2E * HEAD_DIM ** -0.5), fox_k_gain[None],
                             jnp.zeros((6, HEAD_DIM), F32)], axis=0)
    bias_row = jnp.concatenate([fox_f_bias, mlstm_i_bias, mlstm_f_bias,
                                jnp.zeros((gate_pad,), F32)]).reshape(1, V7X_LANES)

    hn = rmsnorm_bf16(xt, norm_g)
    proj = inproj_main(hn, w_main, gains)
    mo, gates_pre = inproj_aux(hn, w_aux, fw)
    gcol = gate_activations(gates_pre, bias_row, H)
    grow = gcol[:, :32].T
    cq = jnp.broadcast_to(grow[:H, :, None], (H, S, V7X_LANES))
    ck = grow[:H].reshape(H, 1, S)
    y_fox = fox_attention(proj, cq, ck, H)
    y_mlstm = mlstm_mixer(proj, mo, gcol, grow, mlstm_out_gain, H, 3)
    return outproj_residual(y_fox, y_mlstm, w_out.astype(BF16), xt)


def kernel(x, norm_mix, norm_ffn, w_in, fox_f_bias, fox_q_gain, fox_k_gain, mlstm_i_bias, mlstm_f_bias,
           mlstm_out_gain, w_out, pool_w, pool_b, pool_scale, router_group_w, router_group_b,
           router_expert_w, router_expert_b, w_gate, w_up, w_down):
    B, S, D = x.shape
    assert B == 1
    depth = norm_mix.shape[0]
    xt = x.reshape(S, D)

    for layer in range(depth):
        j = layer // 2
        if layer % 2 == 0:
            xt = fox_mlstm_residual(xt, norm_mix[layer], w_in[j], fox_f_bias[j], fox_q_gain[j],
                                    fox_k_gain[j], mlstm_i_bias[j], mlstm_f_bias[j], mlstm_out_gain[j],
                                    w_out[j])
        else:
            xt = pool_mixer_residual(xt, norm_mix[layer], pool_w[j].astype(BF16), pool_b[j], pool_scale[j])
        xt = moe_residual(xt, norm_ffn[layer], router_group_w[layer], router_group_b[layer],
                          router_expert_w[layer], router_expert_b[layer],
                          w_gate, w_up, w_down, layer)
    return xt.reshape(B, S, D)
```

```python
import functools

import jax
import jax.numpy as jnp
from jax import lax
from jax.experimental import pallas as pl
from jax.experimental.pallas import tpu as pltpu

F32 = jnp.float32
BF16 = jnp.bfloat16

HEAD_DIM = 128
GATE_SOFTCAP = 15.0
POOL_WINDOWS = (2, 4, 8, 16)
POOL_HALO = 16
N_GROUPS = 4
EXPERTS_PER_GROUP = 8
N_EXPERTS = N_GROUPS * EXPERTS_PER_GROUP
TOP_K = 2
RMS_EPS = 1e-6

V7X_LANES = 128
V7X_VMEM_BYTES = 64 * 1024 * 1024

NORM_ROWS = 512
MM_ROWS = 1024
MM_COLS = 1024
ATT_Q = 1024
ATT_K = 512
MLSTM_CHUNK = 256
MOE_ROWS = 256
NEG_BIG = -1e30
LOG2E = 1.4426950408889634


def _vmem_limit(nbytes):
    return int(min(max(nbytes * 3 // 2, 16 * 1024 * 1024), V7X_VMEM_BYTES - 8 * 1024 * 1024))


def _rms(x, eps=RMS_EPS):
    return x * lax.rsqrt(jnp.mean(x * x, axis=-1, keepdims=True) + eps)


def _log_sigmoid(x):
    return -(jnp.maximum(-x, 0.0) + jnp.log1p(jnp.exp(-jnp.abs(x))))


def _rmsnorm_kernel(x_ref, g_ref, o_ref):
    o_ref[...] = (_rms(x_ref[...]) * g_ref[...]).astype(o_ref.dtype)


def rmsnorm_bf16(x, g):
    S, D = x.shape
    return pl.pallas_call(
        _rmsnorm_kernel,
        out_shape=jax.ShapeDtypeStruct((S, D), BF16),
        grid=(S // NORM_ROWS,),
        in_specs=[pl.BlockSpec((NORM_ROWS, D), lambda i: (i, 0)),
                  pl.BlockSpec((1, D), lambda i: (0, 0))],
        out_specs=pl.BlockSpec((NORM_ROWS, D), lambda i: (i, 0)),
        compiler_params=pltpu.CompilerParams(
            dimension_semantics=("parallel",),
            vmem_limit_bytes=_vmem_limit(2 * NORM_ROWS * D * 6)),
        name="rmsnorm",
    )(x, g.reshape(1, D))


def _inproj_main_kernel(a_ref, w_ref, gain_ref, o_ref, *, n_heads_per_tile):
    j = pl.program_id(1)
    acc = jnp.dot(a_ref[...], w_ref[...], preferred_element_type=F32)

    @pl.when(j < 2)
    def _():
        g = gain_ref[pl.ds(j, 1), :]
        for h in range(n_heads_per_tile):
            a = acc[:, h * HEAD_DIM:(h + 1) * HEAD_DIM]
            o_ref[:, h * HEAD_DIM:(h + 1) * HEAD_DIM] = (_rms(a) * g).astype(o_ref.dtype)

    @pl.when(j == 4)
    def _():
        o_ref[...] = (acc * (HEAD_DIM ** -0.5)).astype(o_ref.dtype)

    @pl.when(jnp.logical_and(j >= 2, j != 4))
    def _():
        o_ref[...] = acc.astype(o_ref.dtype)


def inproj_main(hn, w_main, gains):
    S, D = hn.shape
    N = w_main.shape[1]
    tm, tn = min(MM_ROWS, S), MM_COLS
    return pl.pallas_call(
        functools.partial(_inproj_main_kernel, n_heads_per_tile=tn // HEAD_DIM),
        out_shape=jax.ShapeDtypeStruct((S, N), BF16),
        grid=(S // tm, N // tn),
        in_specs=[pl.BlockSpec((tm, D), lambda i, j: (i, 0)),
                  pl.BlockSpec((D, tn), lambda i, j: (0, j)),
                  pl.BlockSpec((8, HEAD_DIM), lambda i, j: (0, 0))],
        out_specs=pl.BlockSpec((tm, tn), lambda i, j: (i, j)),
        compiler_params=pltpu.CompilerParams(
            dimension_semantics=("parallel", "parallel"),
            vmem_limit_bytes=_vmem_limit(2 * (tm * D * 2 + D * tn * 2 + tm * tn * 2) + 2 * tm * tn * 4)),
        name="inproj_main",
    )(hn, w_main, gains)


def _inproj_aux_kernel(a_ref, w_ref, mo_ref, gate_ref):
    acc = jnp.dot(a_ref[...], w_ref[...], preferred_element_type=F32)
    n_mo = mo_ref.shape[1]
    mo_ref[...] = acc[:, :n_mo]
    gate_ref[...] = acc[:, n_mo:]


def inproj_aux(hn, w_aux, n_mo):
    S, D = hn.shape
    N = w_aux.shape[1]
    tm = min(NORM_ROWS, S)
    return pl.pallas_call(
        _inproj_aux_kernel,
        out_shape=(jax.ShapeDtypeStruct((S, n_mo), F32),
                   jax.ShapeDtypeStruct((S, N - n_mo), F32)),
        grid=(S // tm,),
        in_specs=[pl.BlockSpec((tm, D), lambda i: (i, 0)),
                  pl.BlockSpec((D, N), lambda i: (0, 0))],
        out_specs=(pl.BlockSpec((tm, n_mo), lambda i: (i, 0)),
                   pl.BlockSpec((tm, N - n_mo), lambda i: (i, 0))),
        compiler_params=pltpu.CompilerParams(
            dimension_semantics=("parallel",),
            vmem_limit_bytes=_vmem_limit(2 * (tm * D * 2 + D * N * 2 + tm * N * 4) + tm * N * 4)),
        name="inproj_aux",
    )(hn, w_aux)


def _split3_dot(tri, val):
    v1 = val.astype(BF16)
    r1 = val - v1.astype(F32)
    v2 = r1.astype(BF16)
    v3 = (r1 - v2.astype(F32)).astype(BF16)
    out = jnp.dot(tri, v1, preferred_element_type=F32)
    out += jnp.dot(tri, v2, preferred_element_type=F32)
    out += jnp.dot(tri, v3, preferred_element_type=F32)
    return out


def _gates_kernel(g_ref, bias_ref, o_ref, carry_ref, *, n_heads):
    @pl.when(pl.program_id(0) == 0)
    def _():
        carry_ref[...] = jnp.zeros_like(carry_ref)

    rows = g_ref.shape[0]
    z = g_ref[...] + bias_ref[...]
    lane = lax.broadcasted_iota(jnp.int32, z.shape, 1)
    capped = GATE_SOFTCAP * jnp.tanh(z / GATE_SOFTCAP)
    is_fox = lane < n_heads
    is_i = jnp.logical_and(lane >= n_heads, lane < 2 * n_heads)
    is_f = jnp.logical_and(lane >= 2 * n_heads, lane < 3 * n_heads)
    logf = jnp.where(is_fox, _log_sigmoid(z), jnp.where(is_f, _log_sigmoid(capped), 0.0))
    r = lax.broadcasted_iota(jnp.int32, (rows, rows), 0)
    c = lax.broadcasted_iota(jnp.int32, (rows, rows), 1)
    tri = jnp.where(r >= c, 1.0, 0.0).astype(BF16)
    cum = _split3_dot(tri, logf)
    glob = cum + carry_ref[...]
    o_ref[...] = jnp.where(is_fox, glob * LOG2E, jnp.where(is_i, capped, cum))
    carry_ref[...] = glob[rows - 1:rows, :]


def gate_activations(gates_pre, bias_row, n_heads):
    S, W = gates_pre.shape
    tb = MLSTM_CHUNK
    return pl.pallas_call(
        functools.partial(_gates_kernel, n_heads=n_heads),
        out_shape=jax.ShapeDtypeStruct((S, W), F32),
        grid=(S // tb,),
        in_specs=[pl.BlockSpec((tb, W), lambda i: (i, 0)),
                  pl.BlockSpec((1, W), lambda i: (0, 0))],
        out_specs=pl.BlockSpec((tb, W), lambda i: (i, 0)),
        scratch_shapes=[pltpu.VMEM((1, W), F32)],
        compiler_params=pltpu.CompilerParams(dimension_semantics=("arbitrary",)),
        name="gate_activations",
    )(gates_pre, bias_row)


def _fox_kernel(first_ref, q_ref, k_ref, v_ref, cq_ref, ck_ref, o_ref, m_ref, acc_ref, s_ref, p_ref, alpha_ref,
                *, n_sub):
    i = pl.program_id(1)
    d = HEAD_DIM
    tk = q_ref.shape[0] // n_sub
    assert n_sub % 2 == 0

    m_ref[...] = jnp.full_like(m_ref, NEG_BIG)
    acc_ref[...] = jnp.zeros_like(acc_ref)
    p_ref[1] = jnp.zeros_like(p_ref[1])
    alpha_ref[1] = jnp.ones_like(alpha_ref[1])
    lane = lax.broadcasted_iota(jnp.int32, (tk, d), 1)
    ones_col = jnp.where(lane == 0, 1.0, 0.0).astype(BF16)
    row = lax.broadcasted_iota(jnp.int32, (tk, tk), 0)
    col = lax.broadcasted_iota(jnp.int32, (tk, tk), 1)
    causal = col <= row

    def qk_stage(sub, j, par):
        start = pl.multiple_of(j * tk, tk)
        s = lax.dot_general(q_ref[pl.ds(sub * tk, tk), :], k_ref[pl.ds(start, tk), :],
                            (((1,), (1,)), ((), ())), preferred_element_type=F32)
        s_ref[par, sub] = s - ck_ref[:, pl.ds(start, tk)]

    def sm_stage(sub, par, masked):
        rows = pl.ds(sub * tk, tk)
        s = s_ref[par, sub]
        if masked:
            s = jnp.where(causal, s, NEG_BIG)
        cq = cq_ref[rows, :]
        m_prev = m_ref[rows, :]
        m_new = jnp.maximum(m_prev, jnp.max(s, axis=-1, keepdims=True) + cq)
        p_ref[par, sub] = jnp.exp2(s - jnp.tile(m_new - cq, (1, tk // V7X_LANES))).astype(BF16)
        alpha_ref[par, rows, :] = jnp.exp2(m_prev - m_new)
        m_ref[rows, :] = m_new

    def pv_stage(sub, j, par):
        rows = pl.ds(sub * tk, tk)
        start = pl.multiple_of(j * tk, tk)
        v_aug = jnp.concatenate([v_ref[pl.ds(start, tk), :], ones_col], axis=1)
        acc_ref[rows, :] = (jnp.tile(alpha_ref[par, rows, :], (1, 2)) * acc_ref[rows, :]
                            + jnp.dot(p_ref[par, sub], v_aug, preferred_element_type=F32))

    n_full = i * n_sub
    first = first_ref[pl.program_id(0) * pl.num_programs(1) + i]
    for sub in range(n_sub):
        qk_stage(sub, first, 0)

    def body(tt, carry):
        for par in (0, 1):
            step = 2 * tt + par
            for sub in range(n_sub):
                qk_stage(sub, step + 1, 1 - par)
                sm_stage(sub, par, masked=False)
                pv_stage(sub, jnp.maximum(step - 1, 0), 1 - par)
        return carry

    lax.fori_loop(first // 2, n_full // 2, body, 0)
    for kk in range(n_sub + 1):
        par = kk % 2
        for sub in range(n_sub):
            if kk + 1 <= sub:
                qk_stage(sub, n_full + kk + 1, 1 - par)
            if kk <= sub:
                sm_stage(sub, par, masked=(kk == sub))
            if kk - 1 <= sub:
                pv_stage(sub, jnp.maximum(n_full + kk - 1, 0), 1 - par)
    acc = acc_ref[...]
    o_ref[...] = (acc[:, :d] / acc[:, d:d + 1]).astype(o_ref.dtype)


UNDERFLOW_LOG2 = 152.0


def _fox_first_blocks(proj, c2, n_heads, tk, n_sub):
    S = proj.shape[0]
    H = n_heads
    nb = S // tk
    qk = proj[:, :2 * H * HEAD_DIM].astype(F32).reshape(nb, tk, 2 * H, HEAD_DIM)
    norm = jnp.sqrt(jnp.max(jnp.sum(qk * qk, axis=-1), axis=1)) * 1.001
    qn, kn = norm[:, :H].T, norm[:, H:].T
    cb = c2.reshape(H, nb, tk)
    c_hi, c_lo = jnp.max(cb, axis=-1), jnp.min(cb, axis=-1)
    upper = qn[:, :, None] * kn[:, None, :] + c_hi[:, :, None] - c_lo[:, None, :]
    lower = -(qn * kn)[:, :, None]
    j_idx = jnp.arange(nb, dtype=jnp.int32)
    needed = jnp.logical_or(upper >= lower - UNDERFLOW_LOG2, j_idx[None, None, :] >= j_idx[None, :, None])
    first = jnp.min(jnp.where(needed, j_idx[None, None, :], nb), axis=-1)
    first = jnp.min(first.reshape(H, nb // n_sub, n_sub), axis=-1)
    return ((first // 2) * 2).reshape(-1).astype(jnp.int32)


def fox_attention(proj, cq, ck, n_heads):
    S = proj.shape[0]
    tk = min(ATT_K, S)
    n_sub = max(1, min(ATT_Q, S) // tk)
    tq = n_sub * tk
    H = n_heads
    first = _fox_first_blocks(proj, ck.reshape(H, S), H, tk, n_sub)
    grid_spec = pltpu.PrefetchScalarGridSpec(
        num_scalar_prefetch=1,
        grid=(H, S // tq),
        in_specs=[pl.BlockSpec((tq, HEAD_DIM), lambda h, i, *_: (i, h)),
                  pl.BlockSpec((S, HEAD_DIM), lambda h, i, *_: (0, H + h)),
                  pl.BlockSpec((S, HEAD_DIM), lambda h, i, *_: (0, 2 * H + h)),
                  pl.BlockSpec((None, tq, V7X_LANES), lambda h, i, *_: (h, i, 0)),
                  pl.BlockSpec((None, 1, S), lambda h, i, *_: (h, 0, 0))],
        out_specs=pl.BlockSpec((tq, HEAD_DIM), lambda h, i, *_: (i, h)),
        scratch_shapes=[pltpu.VMEM((tq, V7X_LANES), F32), pltpu.VMEM((tq, 2 * HEAD_DIM), F32),
                        pltpu.VMEM((2, n_sub, tk, tk), F32), pltpu.VMEM((2, n_sub, tk, tk), BF16),
                        pltpu.VMEM((2, tq, V7X_LANES), F32)],
    )
    return pl.pallas_call(
        functools.partial(_fox_kernel, n_sub=n_sub),
        out_shape=jax.ShapeDtypeStruct((S, H * HEAD_DIM), BF16),
        grid_spec=grid_spec,
        compiler_params=pltpu.CompilerParams(
            dimension_semantics=("parallel", "arbitrary"),
            vmem_limit_bytes=_vmem_limit(4 * S * HEAD_DIM * 2 + 2 * n_sub * tk * tk * (4 + 2)
                                         + 4 * tk * tk * 4 + 12 * tq * HEAD_DIM * 4 + 16 * S * 4)),
        name="fox_attention",
    )(first, proj, proj, proj, cq, ck)


def _mlstm_kernel(q_ref, k_ref, v_ref, mo_ref, gcol_ref, grow_ref, gain_ref, o_ref,
                  state_ref, m_ref, *, n_heads):
    L = q_ref.shape[0]
    d = HEAD_DIM

    @pl.when(pl.program_id(0) == 0)
    def _():
        state_ref[...] = jnp.zeros_like(state_ref)
        m_ref[...] = jnp.zeros_like(m_ref)

    row = lax.broadcasted_iota(jnp.int32, (L, L), 0)
    col = lax.broadcasted_iota(jnp.int32, (L, L), 1)
    causal = col <= row
    lane = lax.broadcasted_iota(jnp.int32, (L, d), 1)
    ones_col = jnp.where(lane == 0, 1.0, 0.0).astype(BF16)

    gcol = gcol_ref[...]
    grow = grow_ref[...]
    for h in range(n_heads):
        sl = slice(h * d, (h + 1) * d)
        q = q_ref[:, sl]
        k = k_ref[:, sl]
        v = v_ref[:, sl]
        i_col = gcol[:, n_heads + h:n_heads + h + 1]
        b_col = gcol[:, 2 * n_heads + h:2 * n_heads + h + 1]
        i_row = grow[n_heads + h:n_heads + h + 1, :]
        b_row = grow[2 * n_heads + h:2 * n_heads + h + 1, :]
        m_prev = m_ref[h:h + 1, 0:1]
        state = state_ref[h]

        log_intra = jnp.where(causal, b_col - b_row + i_row, NEG_BIG)
        log_inter = b_col + m_prev
        m_t = jnp.maximum(log_inter, jnp.max(log_intra, axis=-1, keepdims=True))
        w_intra = jnp.exp(log_intra - m_t)
        w_inter = jnp.exp(log_inter - m_t)
        qk = lax.dot_general(q, k, (((1,), (1,)), ((), ())), preferred_element_type=F32) * w_intra
        v_aug = jnp.concatenate([v, ones_col], axis=1)
        tot = jnp.dot(qk.astype(BF16), v_aug, preferred_element_type=F32)
        tot = tot + w_inter * jnp.dot(q, state.astype(BF16), preferred_element_type=F32)
        num = tot[:, :d]
        den = tot[:, d:d + 1]
        hval = num / jnp.maximum(jnp.abs(den), jnp.exp(-m_t))

        b_last = b_col[L - 1:L, :]
        log_w_state = b_last - b_col + i_col
        m_new = jnp.maximum(b_last + m_prev, jnp.max(log_w_state, axis=0, keepdims=True))
        decay = jnp.exp(b_last + m_prev - m_new)
        w_s = jnp.exp(log_w_state - m_new)
        wv = (w_s * v_aug.astype(F32)).astype(BF16)
        upd = lax.dot_general(k, wv, (((0,), (0,)), ((), ())), preferred_element_type=F32)
        state_ref[h] = decay * state + upd
        m_ref[h:h + 1, :] = jnp.broadcast_to(m_new, (1, m_ref.shape[1]))

        hn = _rms(hval) * gain_ref[:, sl]
        o_ref[:, sl] = (jax.nn.sigmoid(mo_ref[:, sl]) * hn).astype(o_ref.dtype)


def mlstm_mixer(proj, mo, gcol, grow, out_gain, n_heads, q_block):
    S = proj.shape[0]
    L = min(MLSTM_CHUNK, S)
    W = n_heads * HEAD_DIM
    return pl.pallas_call(
        functools.partial(_mlstm_kernel, n_heads=n_heads),
        out_shape=jax.ShapeDtypeStruct((S, W), BF16),
        grid=(S // L,),
        in_specs=[pl.BlockSpec((L, W), lambda c: (c, q_block)),
                  pl.BlockSpec((L, W), lambda c: (c, q_block + 1)),
                  pl.BlockSpec((L, W), lambda c: (c, q_block + 2)),
                  pl.BlockSpec((L, W), lambda c: (c, 0)),
                  pl.BlockSpec((L, gcol.shape[1]), lambda c: (c, 0)),
                  pl.BlockSpec((grow.shape[0], L), lambda c: (0, c)),
                  pl.BlockSpec((1, W), lambda c: (0, 0))],
        out_specs=pl.BlockSpec((L, W), lambda c: (c, 0)),
        scratch_shapes=[pltpu.VMEM((n_heads, HEAD_DIM, 2 * HEAD_DIM), F32),
                        pltpu.VMEM((n_heads, V7X_LANES), F32)],
        compiler_params=pltpu.CompilerParams(
            dimension_semantics=("arbitrary",),
            vmem_limit_bytes=_vmem_limit(2 * L * W * (3 * 2 + 4 + 2) + 16 * L * L * 4)),
        name="mlstm",
    )(proj, proj, proj, mo, gcol, grow, out_gain.reshape(1, W))


def _outproj_kernel(a1_ref, a2_ref, w_ref, x_ref, o_ref):
    k1 = a1_ref.shape[1]
    acc = jnp.dot(a1_ref[...], w_ref[:k1, :], preferred_element_type=F32)
    acc += jnp.dot(a2_ref[...], w_ref[k1:, :], preferred_element_type=F32)
    o_ref[...] = x_ref[...] + acc


def outproj_residual(a1, a2, w, x):
    S, K1 = a1.shape
    K2 = a2.shape[1]
    N = w.shape[1]
    tm, tn = min(MM_ROWS, S), MM_COLS
    return pl.pallas_call(
        _outproj_kernel,
        out_shape=jax.ShapeDtypeStruct((S, N), F32),
        grid=(S // tm, N // tn),
        in_specs=[pl.BlockSpec((tm, K1), lambda i, j: (i, 0)),
                  pl.BlockSpec((tm, K2), lambda i, j: (i, 0)),
                  pl.BlockSpec((K1 + K2, tn), lambda i, j: (0, j)),
                  pl.BlockSpec((tm, tn), lambda i, j: (i, j))],
        out_specs=pl.BlockSpec((tm, tn), lambda i, j: (i, j)),
        compiler_params=pltpu.CompilerParams(
            dimension_semantics=("parallel", "parallel"),
            vmem_limit_bytes=_vmem_limit(2 * (tm * (K1 + K2) * 2 + (K1 + K2) * tn * 2 + 2 * tm * tn * 4)
                                         + tm * tn * 4)),
        name="outproj",
    )(a1, a2, w, x)


def _pool_kernel(x_ref, g_ref, w_ref, b_ref, scale_ref, o_ref, carry_ref):
    i = pl.program_id(0)
    tm = x_ref.shape[0]
    gw = w_ref.shape[1]

    @pl.when(i == 0)
    def _():
        carry_ref[...] = jnp.zeros_like(carry_ref)

    x = x_ref[...]
    hn = _rms(x) * g_ref[...]
    t = i * tm + lax.broadcasted_iota(jnp.int32, (tm, 1), 0)
    for g, w in enumerate(POOL_WINDOWS):
        sl = slice(g * gw, (g + 1) * gw)
        hg = hn[:, sl]
        cur = jnp.concatenate([carry_ref[:, sl], hg], axis=0)
        k = 1
        while k < w:
            cur = cur + pltpu.roll(cur, k, axis=0)
            k *= 2
        window_sum = cur[POOL_HALO:, :]
        count = jnp.minimum(t + 1, w).astype(F32)
        pooled = window_sum / count - hg
        y = jnp.dot(pooled.astype(BF16), w_ref[g], preferred_element_type=F32) + b_ref[:, sl]
        o_ref[:, sl] = x[:, sl] + y * scale_ref[:, sl]
    carry_ref[...] = hn[tm - POOL_HALO:, :]


def pool_mixer_residual(x, g, pool_w, pool_b, pool_scale):
    S, D = x.shape
    tm = min(NORM_ROWS, S)
    G, gw, _ = pool_w.shape
    return pl.pallas_call(
        _pool_kernel,
        out_shape=jax.ShapeDtypeStruct((S, D), F32),
        grid=(S // tm,),
        in_specs=[pl.BlockSpec((tm, D), lambda i: (i, 0)),
                  pl.BlockSpec((1, D), lambda i: (0, 0)),
                  pl.BlockSpec((G, gw, gw), lambda i: (0, 0, 0)),
                  pl.BlockSpec((1, D), lambda i: (0, 0)),
                  pl.BlockSpec((1, D), lambda i: (0, 0))],
        out_specs=pl.BlockSpec((tm, D), lambda i: (i, 0)),
        scratch_shapes=[pltpu.VMEM((POOL_HALO, D), F32)],
        compiler_params=pltpu.CompilerParams(
            dimension_semantics=("arbitrary",),
            vmem_limit_bytes=_vmem_limit(4 * tm * D * 4 + 2 * G * gw * gw * 2 + 6 * tm * D * 4)),
        name="pool_mixer",
    )(x, g.reshape(1, D), pool_w, pool_b.reshape(1, D), pool_scale.reshape(1, D))


def _store_slabs(ref, val):
    n = val.shape[0]
    n_chunks = val.shape[1] // V7X_LANES
    for c in range(n_chunks):
        ref[pl.ds(c, n, stride=n_chunks), :] = val[:, c * V7X_LANES:(c + 1) * V7X_LANES]


def _load_slab_chunk(ref, lead, c, n, n_chunks):
    return ref[lead + (pl.ds(c, n, stride=n_chunks), slice(None))]


def _dot_bf16x3(a, b):
    a_hi = a.astype(BF16)
    a_lo = (a - a_hi.astype(F32)).astype(BF16)
    b_hi = b.astype(BF16)
    b_lo = (b - b_hi.astype(F32)).astype(BF16)
    out = jnp.dot(a_hi, b_hi, preferred_element_type=F32)
    out += jnp.dot(a_hi, b_lo, preferred_element_type=F32)
    out += jnp.dot(a_lo, b_hi, preferred_element_type=F32)
    return out


def _router_kernel(x_ref, g_ref, w_ref, b_ref, hn_ref, route_ref):
    hn = _rms(x_ref[...]) * g_ref[...]
    _store_slabs(hn_ref, hn)
    logits = _dot_bf16x3(hn, w_ref[...]) + b_ref[...]
    lane = lax.broadcasted_iota(jnp.int32, logits.shape, 1).astype(F32)
    n_lanes = float(logits.shape[1])

    def first_argmax(vals):
        top = jnp.max(vals, axis=-1, keepdims=True)
        return top, jnp.min(jnp.where(vals == top, lane, n_lanes), axis=-1, keepdims=True)

    is_group = lane < N_GROUPS
    g_top, g_sel = first_argmax(jnp.where(is_group, logits, NEG_BIG))
    g_w = 1.0 / jnp.sum(jnp.where(is_group, jnp.exp(logits - g_top), 0.0), axis=-1, keepdims=True)
    lo = N_GROUPS + EXPERTS_PER_GROUP * g_sel
    e_logits = jnp.where(jnp.logical_and(lane >= lo, lane < lo + EXPERTS_PER_GROUP), logits, NEG_BIG)
    v1, i1 = first_argmax(e_logits)
    v2, i2 = first_argmax(jnp.where(lane == i1, NEG_BIG, e_logits))
    e21 = jnp.exp(v2 - v1)
    w1 = g_w / (1.0 + e21)
    w2 = g_w * e21 / (1.0 + e21)
    route_ref[...] = jnp.where(lane == 0, i1 - N_GROUPS,
                               jnp.where(lane == 1, i2 - N_GROUPS,
                                         jnp.where(lane == 2, w1, jnp.where(lane == 3, w2, 0.0))))


def router(x, g, w_router, b_router):
    S, D = x.shape
    W = w_router.shape[1]
    tm = min(NORM_ROWS, S)
    n_chunks = D // V7X_LANES
    return pl.pallas_call(
        _router_kernel,
        out_shape=(jax.ShapeDtypeStruct((S * n_chunks, V7X_LANES), F32), jax.ShapeDtypeStruct((S, W), F32)),
        grid=(S // tm,),
        in_specs=[pl.BlockSpec((tm, D), lambda i: (i, 0)),
                  pl.BlockSpec((1, D), lambda i: (0, 0)),
                  pl.BlockSpec((D, W), lambda i: (0, 0)),
                  pl.BlockSpec((1, W), lambda i: (0, 0))],
        out_specs=(pl.BlockSpec((tm * n_chunks, V7X_LANES), lambda i: (i, 0)),
                   pl.BlockSpec((tm, W), lambda i: (i, 0))),
        compiler_params=pltpu.CompilerParams(
            dimension_semantics=("parallel",),
            vmem_limit_bytes=_vmem_limit(2 * tm * D * 8 + 2 * D * W * 4 + 4 * tm * D * 4)),
        name="router",
    )(x, g.reshape(1, D), w_router, b_router)


CAST_ROWS = 128
GATHER_UNROLL = 8
WEIGHT_DMA_PRIORITY = 1


def _expert_kernel(be_ref, first_ref, next_ref, active_ref, tok_ref, tok_next_ref, hn_hbm, wg_hbm, wu_hbm,
                   wd_hbm, o_ref, xbuf, stage_g, stage_u, stage_d, wg_ref, wu_ref, wd_ref, wsem, gsem,
                   *, layer):
    b = pl.program_id(0)
    n_blocks = pl.num_programs(0)
    n_chunks = wg_ref.shape[0] // V7X_LANES
    R = o_ref.shape[0] // n_chunks
    slot = lax.rem(b, 2)

    def row_copy(idx_ref, s, r):
        src = pl.multiple_of(idx_ref[0, r] * n_chunks, n_chunks)
        return pltpu.make_async_copy(hn_hbm.at[pl.ds(src, n_chunks), :],
                                     xbuf.at[s, pl.ds(r * n_chunks, n_chunks), :], gsem.at[s])

    def start_rows_loop(idx_ref, s):
        def issue(r, carry):
            row_copy(idx_ref, s, r).start()
            return carry
        lax.fori_loop(0, R, issue, 0, unroll=GATHER_UNROLL)

    def wait_rows(s):
        pltpu.make_async_copy(hn_hbm.at[pl.ds(0, R * n_chunks), :], xbuf.at[s], gsem.at[s]).wait()

    def weight_copies(e):
        return (pltpu.make_async_copy(wg_hbm.at[layer, e], stage_g, wsem.at[0]),
                pltpu.make_async_copy(wu_hbm.at[layer, e], stage_u, wsem.at[1]),
                pltpu.make_async_copy(wd_hbm.at[layer, e], stage_d, wsem.at[2]))

    @pl.when(b == 0)
    def _():
        start_rows_loop(tok_ref, 0)
        for cp in weight_copies(be_ref[0]):
            cp.start(priority=WEIGHT_DMA_PRIORITY)

    @pl.when(first_ref[b] == 1)
    def _():
        for cp in weight_copies(be_ref[b]):
            cp.wait()
        for stage, dst in ((stage_g, wg_ref), (stage_u, wu_ref), (stage_d, wd_ref)):
            def cast_rows(r, carry, stage=stage, dst=dst):
                rows = pl.ds(pl.multiple_of(r * CAST_ROWS, CAST_ROWS), CAST_ROWS)
                dst[rows, :] = stage[rows, :].astype(BF16)
                return carry
            lax.fori_loop(0, stage.shape[0] // CAST_ROWS, cast_rows, 0)

        @pl.when(next_ref[b] >= 0)
        def _():
            for cp in weight_copies(next_ref[b]):
                cp.start(priority=WEIGHT_DMA_PRIORITY)

    @pl.when(active_ref[b] == 1)
    def _():
        wait_rows(slot)
        x = jnp.concatenate([_load_slab_chunk(xbuf, (slot,), c, R, n_chunks).astype(BF16)
                             for c in range(n_chunks)], axis=1)
        for r in range(R):
            row_copy(tok_next_ref, 1 - slot, r).start()
        a = jnp.dot(x, wg_ref[...], preferred_element_type=F32)
        u = jnp.dot(x, wu_ref[...], preferred_element_type=F32)
        hmid = (a * jax.nn.sigmoid(a) * u).astype(BF16)
        _store_slabs(o_ref, jnp.dot(hmid, wd_ref[...], preferred_element_type=F32))

    @pl.when(active_ref[b] == 0)
    def _():
        wait_rows(slot)
        start_rows_loop(tok_next_ref, 1 - slot)
        o_ref[...] = jnp.zeros_like(o_ref)

    @pl.when(b == n_blocks - 1)
    def _():
        wait_rows(1 - slot)


def expert_blocks(block_e, first, next_e, active, row_tok, hn, w_gate, w_up, w_down, layer):
    n_blocks, _, R = row_tok.shape
    D, Dh = w_gate.shape[2:]
    n_chunks = D // V7X_LANES
    hbm = pl.BlockSpec(memory_space=pl.ANY)
    grid_spec = pltpu.PrefetchScalarGridSpec(
        num_scalar_prefetch=4,
        grid=(n_blocks,),
        in_specs=[pl.BlockSpec((None, 1, R), lambda b, *_: (b, 0, 0), memory_space=pltpu.SMEM),
                  pl.BlockSpec((None, 1, R), lambda b, *_: (jnp.minimum(b + 1, n_blocks - 1), 0, 0),
                               memory_space=pltpu.SMEM),
                  hbm, hbm, hbm, hbm],
        out_specs=pl.BlockSpec((R * n_chunks, V7X_LANES), lambda b, *_: (b, 0)),
        scratch_shapes=[pltpu.VMEM((2, R * n_chunks, V7X_LANES), F32),
                        pltpu.VMEM((D, Dh), F32), pltpu.VMEM((D, Dh), F32), pltpu.VMEM((Dh, D), F32),
                        pltpu.VMEM((D, Dh), BF16), pltpu.VMEM((D, Dh), BF16), pltpu.VMEM((Dh, D), BF16),
                        pltpu.SemaphoreType.DMA((3,)), pltpu.SemaphoreType.DMA((2,))],
    )
    return pl.pallas_call(
        functools.partial(_expert_kernel, layer=layer),
        out_shape=jax.ShapeDtypeStruct((n_blocks * R * n_chunks, V7X_LANES), F32),
        grid_spec=grid_spec,
        compiler_params=pltpu.CompilerParams(
            dimension_semantics=("arbitrary",),
            vmem_limit_bytes=_vmem_limit(3 * D * Dh * (4 + 2) + 4 * R * D * 4 + R * D * 2 + 6 * R * Dh * 4)),
        name="moe_experts",
    )(block_e, first, next_e, active, row_tok, row_tok, hn, w_gate, w_up, w_down)


COMBINE_ROWS = 256


def _combine_kernel(idx_ref, idx_next_ref, x_ref, route_ref, yb_hbm, o_ref, cbuf, gsem):
    i = pl.program_id(0)
    n_tiles = pl.num_programs(0)
    tm, D = x_ref.shape
    n_chunks = D // V7X_LANES
    slot = lax.rem(i, 2)

    def row_copy(idx, s, r, k):
        src = pl.multiple_of(idx[0, TOP_K * r + k] * n_chunks, n_chunks)
        return pltpu.make_async_copy(yb_hbm.at[pl.ds(src, n_chunks), :],
                                     cbuf.at[s, k, pl.ds(r * n_chunks, n_chunks), :], gsem.at[s])

    def wait_rows(s):
        for k in range(TOP_K):
            pltpu.make_async_copy(yb_hbm.at[pl.ds(0, tm * n_chunks), :], cbuf.at[s, k], gsem.at[s]).wait()

    @pl.when(i == 0)
    def _():
        def issue(r, carry):
            for k in range(TOP_K):
                row_copy(idx_ref, 0, r, k).start()
            return carry
        lax.fori_loop(0, tm, issue, 0, unroll=GATHER_UNROLL)

    wait_rows(slot)
    for r in range(tm):
        for k in range(TOP_K):
            row_copy(idx_next_ref, 1 - slot, r, k).start(priority=k % 2)
    for c in range(n_chunks):
        cols = slice(c * V7X_LANES, (c + 1) * V7X_LANES)
        acc = x_ref[:, cols]
        for k in range(TOP_K):
            acc = acc + route_ref[:, TOP_K + k:TOP_K + k + 1] * _load_slab_chunk(cbuf, (slot, k), c, tm, n_chunks)
        o_ref[:, cols] = acc

    @pl.when(i == n_tiles - 1)
    def _():
        wait_rows(1 - slot)


def combine_residual(x, route, dest, yb):
    T, D = x.shape
    tm = min(COMBINE_ROWS, T)
    n_tiles = T // tm
    idx = dest.reshape(n_tiles, 1, tm * TOP_K)
    return pl.pallas_call(
        _combine_kernel,
        out_shape=jax.ShapeDtypeStruct((T, D), F32),
        grid=(n_tiles,),
        in_specs=[pl.BlockSpec((None, 1, tm * TOP_K), lambda i: (i, 0, 0), memory_space=pltpu.SMEM),
                  pl.BlockSpec((None, 1, tm * TOP_K), lambda i: (jnp.minimum(i + 1, n_tiles - 1), 0, 0),
                               memory_space=pltpu.SMEM),
                  pl.BlockSpec((tm, D), lambda i: (i, 0)),
                  pl.BlockSpec((tm, route.shape[1]), lambda i: (i, 0)),
                  pl.BlockSpec(memory_space=pl.ANY)],
        out_specs=pl.BlockSpec((tm, D), lambda i: (i, 0)),
        scratch_shapes=[pltpu.VMEM((2, TOP_K, tm * (D // V7X_LANES), V7X_LANES), F32),
                        pltpu.SemaphoreType.DMA((2,))],
        compiler_params=pltpu.CompilerParams(
            dimension_semantics=("arbitrary",),
            vmem_limit_bytes=_vmem_limit(2 * TOP_K * tm * D * 4 + 6 * tm * D * 4)),
        name="moe_combine",
    )(idx, idx, x, route, yb)


def moe_residual(x, g, rgw, rgb, rew, reb, w_gate, w_up, w_down, layer):
    T, D = x.shape
    R = MOE_ROWS
    assert TOP_K == 2
    pad = V7X_LANES - N_GROUPS - N_EXPERTS
    w_router = jnp.concatenate([rgw, rew, jnp.zeros((D, pad), F32)], axis=1)
    b_router = jnp.concatenate([rgb, reb, jnp.zeros((pad,), F32)]).reshape(1, V7X_LANES)
    hn, route = router(x, g, w_router, b_router)
    expert_id = route[:, :TOP_K].astype(jnp.int32)

    n_assign = T * TOP_K
    flat_e = expert_id.reshape(-1)
    onehot = (flat_e[:, None] == jnp.arange(N_EXPERTS, dtype=jnp.int32)[None, :]).astype(jnp.int32)
    running = jnp.cumsum(onehot, axis=0)
    counts = running[-1]
    rank = jnp.take_along_axis(running, flat_e[:, None], axis=1)[:, 0] - 1
    padded = (counts + R - 1) // R * R
    pends = jnp.cumsum(padded)
    pstarts = pends - padded
    dest = pstarts[flat_e] + rank
    n_blocks = (n_assign + N_EXPERTS * (R - 1) + R - 1) // R
    flat_tok = jnp.repeat(jnp.arange(T, dtype=jnp.int32), TOP_K)
    row_tok = jnp.zeros((n_blocks * R,), jnp.int32).at[dest].set(flat_tok)

    blk_start = jnp.arange(n_blocks, dtype=jnp.int32) * R
    active = blk_start < pends[-1]
    block_e = jnp.minimum(jnp.sum(blk_start[:, None] >= pends[None, :], axis=1), N_EXPERTS - 1).astype(jnp.int32)
    prev_e = jnp.concatenate([jnp.full((1,), -1, jnp.int32), block_e[:-1]])
    first = jnp.logical_and(active, block_e != prev_e)
    later = lax.cummin(jnp.where(first, block_e, N_EXPERTS)[::-1])[::-1]
    next_e = jnp.concatenate([later[1:], jnp.full((1,), N_EXPERTS, jnp.int32)])
    next_e = jnp.where(next_e >= N_EXPERTS, -1, next_e).astype(jnp.int32)

    yb = expert_blocks(block_e, first.astype(jnp.int32), next_e, active.astype(jnp.int32),
                       row_tok.reshape(n_blocks, 1, R), hn, w_gate, w_up, w_down, layer)
    return combine_residual(x, route, dest.reshape(T, TOP_K), yb)


def fox_mlstm_residual(xt, norm_g, w, fox_f_bias, fox_q_gain, fox_k_gain, mlstm_i_bias, mlstm_f_bias,
                       mlstm_out_gain, w_out):
    S, D = xt.shape
    H = fox_f_bias.shape[0]
    assert mlstm_i_bias.shape[0] == H and 3 * H <= 32
    fw = H * HEAD_DIM
    o_ff = 3 * fw
    o_mq = o_ff + H
    o_mi = o_mq + 3 * fw
    o_mo = o_mi + 2 * H
    w_main = jnp.concatenate([w[:, :o_ff], w[:, o_mq:o_mi]], axis=1).astype(BF16)
    gate_pad = V7X_LANES - 3 * H
    w_aux = jnp.concatenate([w[:, o_mo:], w[:, o_ff:o_mq], w[:, o_mi:o_mo],
                             jnp.zeros((D, gate_pad), F32)], axis=1).astype(BF16)
    gains = jnp.concatenate([fox_q_gain[None] * (LOG2E * HEAD_DIM ** -0.5), fox_k_gain[None],
                             jnp.zeros((6, HEAD_DIM), F32)], axis=0)
    bias_row = jnp.concatenate([fox_f_bias, mlstm_i_bias, mlstm_f_bias,
                                jnp.zeros((gate_pad,), F32)]).reshape(1, V7X_LANES)

    hn = rmsnorm_bf16(xt, norm_g)
    proj = inproj_main(hn, w_main, gains)
    mo, gates_pre = inproj_aux(hn, w_aux, fw)
    gcol = gate_activations(gates_pre, bias_row, H)
    grow = gcol[:, :32].T
    cq = jnp.broadcast_to(grow[:H, :, None], (H, S, V7X_LANES))
    ck = grow[:H].reshape(H, 1, S)
    y_fox = fox_attention(proj, cq, ck, H)
    y_mlstm = mlstm_mixer(proj, mo, gcol, grow, mlstm_out_gain, H, 3)
    return outproj_residual(y_fox, y_mlstm, w_out.astype(BF16), xt)


def kernel(x, norm_mix, norm_ffn, w_in, fox_f_bias, fox_q_gain, fox_k_gain, mlstm_i_bias, mlstm_f_bias,
           mlstm_out_gain, w_out, pool_w, pool_b, pool_scale, router_group_w, router_group_b,
           router_expert_w, router_expert_b, w_gate, w_up, w_down):
    B, S, D = x.shape
    assert B == 1
    depth = norm_mix.shape[0]
    xt = x.reshape(S, D)

    for layer in range(depth):
        j = layer // 2
        if layer % 2 == 0:
            xt = fox_mlstm_residual(xt, norm_mix[layer], w_in[j], fox_f_bias[j], fox_q_gain[j],
                                    fox_k_gain[j], mlstm_i_bias[j], mlstm_f_bias[j], mlstm_out_gain[j],
                                    w_out[j])
        else:
            xt = pool_mixer_residual(xt, norm_mix[layer], pool_w[j].astype(BF16), pool_b[j], pool_scale[j])
        xt = moe_residual(xt, norm_ffn[layer], router_group_w[layer], router_group_b[layer],
                          router_expert_w[layer], router_expert_b[layer],
                          w_gate, w_up, w_down, layer)
    return xt.reshape(B, S, D)
```

```python
import functools

import jax
import jax.numpy as jnp
from jax import lax
from jax.experimental import pallas as pl
from jax.experimental.pallas import tpu as pltpu

F32 = jnp.float32
BF16 = jnp.bfloat16

HEAD_DIM = 128
GATE_SOFTCAP = 15.0
POOL_WINDOWS = (2, 4, 8, 16)
POOL_HALO = 16
N_GROUPS = 4
EXPERTS_PER_GROUP = 8
N_EXPERTS = N_GROUPS * EXPERTS_PER_GROUP
TOP_K = 2
RMS_EPS = 1e-6

V7X_LANES = 128
V7X_VMEM_BYTES = 64 * 1024 * 1024

NORM_ROWS = 512
MM_ROWS = 1024
MM_COLS = 1024
ATT_Q = 1024
ATT_K = 512
MLSTM_CHUNK = 256
MOE_ROWS = 256
NEG_BIG = -1e30
LOG2E = 1.4426950408889634


def _vmem_limit(nbytes):
    return int(min(max(nbytes * 3 // 2, 16 * 1024 * 1024), V7X_VMEM_BYTES - 8 * 1024 * 1024))


def _rms(x, eps=RMS_EPS):
    return x * lax.rsqrt(jnp.mean(x * x, axis=-1, keepdims=True) + eps)


def _log_sigmoid(x):
    return -(jnp.maximum(-x, 0.0) + jnp.log1p(jnp.exp(-jnp.abs(x))))


def _rmsnorm_kernel(x_ref, g_ref, o_ref):
    o_ref[...] = (_rms(x_ref[...]) * g_ref[...]).astype(o_ref.dtype)


def rmsnorm_bf16(x, g):
    S, D = x.shape
    return pl.pallas_call(
        _rmsnorm_kernel,
        out_shape=jax.ShapeDtypeStruct((S, D), BF16),
        grid=(S // NORM_ROWS,),
        in_specs=[pl.BlockSpec((NORM_ROWS, D), lambda i: (i, 0)),
                  pl.BlockSpec((1, D), lambda i: (0, 0))],
        out_specs=pl.BlockSpec((NORM_ROWS, D), lambda i: (i, 0)),
        compiler_params=pltpu.CompilerParams(
            dimension_semantics=("parallel",),
            vmem_limit_bytes=_vmem_limit(2 * NORM_ROWS * D * 6)),
        name="rmsnorm",
    )(x, g.reshape(1, D))


def _inproj_main_kernel(a_ref, w_ref, gain_ref, o_ref, *, n_heads_per_tile):
    j = pl.program_id(1)
    acc = jnp.dot(a_ref[...], w_ref[...], preferred_element_type=F32)

    @pl.when(j < 2)
    def _():
        g = gain_ref[pl.ds(j, 1), :]
        for h in range(n_heads_per_tile):
            a = acc[:, h * HEAD_DIM:(h + 1) * HEAD_DIM]
            o_ref[:, h * HEAD_DIM:(h + 1) * HEAD_DIM] = (_rms(a) * g).astype(o_ref.dtype)

    @pl.when(j == 4)
    def _():
        o_ref[...] = (acc * (HEAD_DIM ** -0.5)).astype(o_ref.dtype)

    @pl.when(jnp.logical_and(j >= 2, j != 4))
    def _():
        o_ref[...] = acc.astype(o_ref.dtype)


def inproj_main(hn, w_main, gains):
    S, D = hn.shape
    N = w_main.shape[1]
    tm, tn = min(MM_ROWS, S), MM_COLS
    return pl.pallas_call(
        functools.partial(_inproj_main_kernel, n_heads_per_tile=tn // HEAD_DIM),
        out_shape=jax.ShapeDtypeStruct((S, N), BF16),
        grid=(S // tm, N // tn),
        in_specs=[pl.BlockSpec((tm, D), lambda i, j: (i, 0)),
                  pl.BlockSpec((D, tn), lambda i, j: (0, j)),
                  pl.BlockSpec((8, HEAD_DIM), lambda i, j: (0, 0))],
        out_specs=pl.BlockSpec((tm, tn), lambda i, j: (i, j)),
        compiler_params=pltpu.CompilerParams(
            dimension_semantics=("parallel", "parallel"),
            vmem_limit_bytes=_vmem_limit(2 * (tm * D * 2 + D * tn * 2 + tm * tn * 2) + 2 * tm * tn * 4)),
        name="inproj_main",
    )(hn, w_main, gains)


def _inproj_aux_kernel(a_ref, w_ref, mo_ref, gate_ref):
    acc = jnp.dot(a_ref[...], w_ref[...], preferred_element_type=F32)
    n_mo = mo_ref.shape[1]
    mo_ref[...] = acc[:, :n_mo]
    gate_ref[...] = acc[:, n_mo:]


def inproj_aux(hn, w_aux, n_mo):
    S, D = hn.shape
    N = w_aux.shape[1]
    tm = min(NORM_ROWS, S)
    return pl.pallas_call(
        _inproj_aux_kernel,
        out_shape=(jax.ShapeDtypeStruct((S, n_mo), F32),
                   jax.ShapeDtypeStruct((S, N - n_mo), F32)),
        grid=(S // tm,),
        in_specs=[pl.BlockSpec((tm, D), lambda i: (i, 0)),
                  pl.BlockSpec((D, N), lambda i: (0, 0))],
        out_specs=(pl.BlockSpec((tm, n_mo), lambda i: (i, 0)),
                   pl.BlockSpec((tm, N - n_mo), lambda i: (i, 0))),
        compiler_params=pltpu.CompilerParams(
            dimension_semantics=("parallel",),
            vmem_limit_bytes=_vmem_limit(2 * (tm * D * 2 + D * N * 2 + tm * N * 4) + tm * N * 4)),
        name="inproj_aux",
    )(hn, w_aux)


def _split3_dot(tri, val):
    v1 = val.astype(BF16)
    r1 = val - v1.astype(F32)
    v2 = r1.astype(BF16)
    v3 = (r1 - v2.astype(F32)).astype(BF16)
    out = jnp.dot(tri, v1, preferred_element_type=F32)
    out += jnp.dot(tri, v2, preferred_element_type=F32)
    out += jnp.dot(tri, v3, preferred_element_type=F32)
    return out


def _gates_kernel(g_ref, bias_ref, o_ref, carry_ref, *, n_heads):
    @pl.when(pl.program_id(0) == 0)
    def _():
        carry_ref[...] = jnp.zeros_like(carry_ref)

    rows = g_ref.shape[0]
    z = g_ref[...] + bias_ref[...]
    lane = lax.broadcasted_iota(jnp.int32, z.shape, 1)
    capped = GATE_SOFTCAP * jnp.tanh(z / GATE_SOFTCAP)
    is_fox = lane < n_heads
    is_i = jnp.logical_and(lane >= n_heads, lane < 2 * n_heads)
    is_f = jnp.logical_and(lane >= 2 * n_heads, lane < 3 * n_heads)
    logf = jnp.where(is_fox, _log_sigmoid(z), jnp.where(is_f, _log_sigmoid(capped), 0.0))
    r = lax.broadcasted_iota(jnp.int32, (rows, rows), 0)
    c = lax.broadcasted_iota(jnp.int32, (rows, rows), 1)
    tri = jnp.where(r >= c, 1.0, 0.0).astype(BF16)
    cum = _split3_dot(tri, logf)
    glob = cum + carry_ref[...]
    o_ref[...] = jnp.where(is_fox, glob * LOG2E, jnp.where(is_i, capped, cum))
    carry_ref[...] = glob[rows - 1:rows, :]


def gate_activations(gates_pre, bias_row, n_heads):
    S, W = gates_pre.shape
    tb = MLSTM_CHUNK
    return pl.pallas_call(
        functools.partial(_gates_kernel, n_heads=n_heads),
        out_shape=jax.ShapeDtypeStruct((S, W), F32),
        grid=(S // tb,),
        in_specs=[pl.BlockSpec((tb, W), lambda i: (i, 0)),
                  pl.BlockSpec((1, W), lambda i: (0, 0))],
        out_specs=pl.BlockSpec((tb, W), lambda i: (i, 0)),
        scratch_shapes=[pltpu.VMEM((1, W), F32)],
        compiler_params=pltpu.CompilerParams(dimension_semantics=("arbitrary",)),
        name="gate_activations",
    )(gates_pre, bias_row)


def _fox_kernel(first_ref, q_ref, k_ref, v_ref, cq_ref, ck_ref, o_ref, m_ref, acc_ref, s_ref, p_ref, alpha_ref,
                *, n_sub):
    i = pl.program_id(1)
    d = HEAD_DIM
    tk = q_ref.shape[0] // n_sub
    assert n_sub % 2 == 0

    m_ref[...] = jnp.full_like(m_ref, NEG_BIG)
    acc_ref[...] = jnp.zeros_like(acc_ref)
    p_ref[1] = jnp.zeros_like(p_ref[1])
    alpha_ref[1] = jnp.ones_like(alpha_ref[1])
    lane = lax.broadcasted_iota(jnp.int32, (tk, d), 1)
    ones_col = jnp.where(lane == 0, 1.0, 0.0).astype(BF16)
    row = lax.broadcasted_iota(jnp.int32, (tk, tk), 0)
    col = lax.broadcasted_iota(jnp.int32, (tk, tk), 1)
    causal = col <= row

    def qk_stage(sub, j, par):
        start = pl.multiple_of(j * tk, tk)
        s = lax.dot_general(q_ref[pl.ds(sub * tk, tk), :], k_ref[pl.ds(start, tk), :],
                            (((1,), (1,)), ((), ())), preferred_element_type=F32)
        s_ref[par, sub] = s - ck_ref[:, pl.ds(start, tk)]

    def sm_stage(sub, par, masked):
        rows = pl.ds(sub * tk, tk)
        s = s_ref[par, sub]
        if masked:
            s = jnp.where(causal, s, NEG_BIG)
        cq = cq_ref[rows, :]
        m_prev = m_ref[rows, :]
        m_new = jnp.maximum(m_prev, jnp.max(s, axis=-1, keepdims=True) + cq)
        p_ref[par, sub] = jnp.exp2(s - jnp.tile(m_new - cq, (1, tk // V7X_LANES))).astype(BF16)
        alpha_ref[par, rows, :] = jnp.exp2(m_prev - m_new)
        m_ref[rows, :] = m_new

    def pv_stage(sub, j, par):
        rows = pl.ds(sub * tk, tk)
        start = pl.multiple_of(j * tk, tk)
        v_aug = jnp.concatenate([v_ref[pl.ds(start, tk), :], ones_col], axis=1)
        acc_ref[rows, :] = (jnp.tile(alpha_ref[par, rows, :], (1, 2)) * acc_ref[rows, :]
                            + jnp.dot(p_ref[par, sub], v_aug, preferred_element_type=F32))

    n_full = i * n_sub
    first = first_ref[pl.program_id(0) * pl.num_programs(1) + i]
    for sub in range(n_sub):
        qk_stage(sub, first, 0)

    def body(tt, carry):
        for par in (0, 1):
            step = 2 * tt + par
            for sub in range(n_sub):
                qk_stage(sub, step + 1, 1 - par)
                sm_stage(sub, par, masked=False)
                pv_stage(sub, jnp.maximum(step - 1, 0), 1 - par)
        return carry

    lax.fori_loop(first // 2, n_full // 2, body, 0)
    for kk in range(n_sub + 1):
        par = kk % 2
        for sub in range(n_sub):
            if kk + 1 <= sub:
                qk_stage(sub, n_full + kk + 1, 1 - par)
            if kk <= sub:
                sm_stage(sub, par, masked=(kk == sub))
            if kk - 1 <= sub:
                pv_stage(sub, jnp.maximum(n_full + kk - 1, 0), 1 - par)
    acc = acc_ref[...]
    o_ref[...] = (acc[:, :d] / acc[:, d:d + 1]).astype(o_ref.dtype)


UNDERFLOW_LOG2 = 152.0


def _fox_first_blocks(q_gain, k_gain, c2, tk, n_sub):
    H, S = c2.shape
    nb = S // tk
    slack = 1.0 + 2.0 ** -6
    qk_bound = HEAD_DIM * jnp.max(jnp.abs(q_gain)) * jnp.max(jnp.abs(k_gain)) * slack
    cb = c2.reshape(H, nb, tk)
    c_hi, c_lo = jnp.max(cb, axis=-1), jnp.min(cb, axis=-1)
    upper = qk_bound + c_hi[:, :, None] - c_lo[:, None, :]
    lower = -qk_bound
    j_idx = jnp.arange(nb, dtype=jnp.int32)
    needed = jnp.logical_or(upper >= lower - UNDERFLOW_LOG2, j_idx[None, None, :] >= j_idx[None, :, None])
    first = jnp.min(jnp.where(needed, j_idx[None, None, :], nb), axis=-1)
    first = jnp.min(first.reshape(H, nb // n_sub, n_sub), axis=-1)
    return ((first // 2) * 2).reshape(-1).astype(jnp.int32)


def fox_attention(proj, cq, ck, n_heads, q_gain, k_gain):
    S = proj.shape[0]
    tk = min(ATT_K, S)
    n_sub = max(1, min(ATT_Q, S) // tk)
    tq = n_sub * tk
    H = n_heads
    first = _fox_first_blocks(q_gain, k_gain, ck.reshape(H, S), tk, n_sub)
    grid_spec = pltpu.PrefetchScalarGridSpec(
        num_scalar_prefetch=1,
        grid=(H, S // tq),
        in_specs=[pl.BlockSpec((tq, HEAD_DIM), lambda h, i, *_: (i, h)),
                  pl.BlockSpec((S, HEAD_DIM), lambda h, i, *_: (0, H + h)),
                  pl.BlockSpec((S, HEAD_DIM), lambda h, i, *_: (0, 2 * H + h)),
                  pl.BlockSpec((None, tq, V7X_LANES), lambda h, i, *_: (h, i, 0)),
                  pl.BlockSpec((None, 1, S), lambda h, i, *_: (h, 0, 0))],
        out_specs=pl.BlockSpec((tq, HEAD_DIM), lambda h, i, *_: (i, h)),
        scratch_shapes=[pltpu.VMEM((tq, V7X_LANES), F32), pltpu.VMEM((tq, 2 * HEAD_DIM), F32),
                        pltpu.VMEM((2, n_sub, tk, tk), F32), pltpu.VMEM((2, n_sub, tk, tk), BF16),
                        pltpu.VMEM((2, tq, V7X_LANES), F32)],
    )
    return pl.pallas_call(
        functools.partial(_fox_kernel, n_sub=n_sub),
        out_shape=jax.ShapeDtypeStruct((S, H * HEAD_DIM), BF16),
        grid_spec=grid_spec,
        compiler_params=pltpu.CompilerParams(
            dimension_semantics=("parallel", "arbitrary"),
            vmem_limit_bytes=_vmem_limit(4 * S * HEAD_DIM * 2 + 2 * n_sub * tk * tk * (4 + 2)
                                         + 4 * tk * tk * 4 + 12 * tq * HEAD_DIM * 4 + 16 * S * 4)),
        name="fox_attention",
    )(first, proj, proj, proj, cq, ck)


def _mlstm_kernel(q_ref, k_ref, v_ref, mo_ref, gcol_ref, grow_ref, gain_ref, o_ref,
                  state_ref, m_ref, *, n_heads):
    L = q_ref.shape[0]
    d = HEAD_DIM

    @pl.when(pl.program_id(0) == 0)
    def _():
        state_ref[...] = jnp.zeros_like(state_ref)
        m_ref[...] = jnp.zeros_like(m_ref)

    row = lax.broadcasted_iota(jnp.int32, (L, L), 0)
    col = lax.broadcasted_iota(jnp.int32, (L, L), 1)
    causal = col <= row
    lane = lax.broadcasted_iota(jnp.int32, (L, d), 1)
    ones_col = jnp.where(lane == 0, 1.0, 0.0).astype(BF16)

    gcol = gcol_ref[...]
    grow = grow_ref[...]
    for h in range(n_heads):
        sl = slice(h * d, (h + 1) * d)
        q = q_ref[:, sl]
        k = k_ref[:, sl]
        v = v_ref[:, sl]
        i_col = gcol[:, n_heads + h:n_heads + h + 1]
        b_col = gcol[:, 2 * n_heads + h:2 * n_heads + h + 1]
        i_row = grow[n_heads + h:n_heads + h + 1, :]
        b_row = grow[2 * n_heads + h:2 * n_heads + h + 1, :]
        m_prev = m_ref[h:h + 1, 0:1]
        state = state_ref[h]

        log_intra = jnp.where(causal, b_col - b_row + i_row, NEG_BIG)
        log_inter = b_col + m_prev
        m_t = jnp.maximum(log_inter, jnp.max(log_intra, axis=-1, keepdims=True))
        w_intra = jnp.exp(log_intra - m_t)
        w_inter = jnp.exp(log_inter - m_t)
        qk = lax.dot_general(q, k, (((1,), (1,)), ((), ())), preferred_element_type=F32) * w_intra
        v_aug = jnp.concatenate([v, ones_col], axis=1)
        tot = jnp.dot(qk.astype(BF16), v_aug, preferred_element_type=F32)
        tot = tot + w_inter * jnp.dot(q, state.astype(BF16), preferred_element_type=F32)
        num = tot[:, :d]
        den = tot[:, d:d + 1]
        hval = num / jnp.maximum(jnp.abs(den), jnp.exp(-m_t))

        b_last = b_col[L - 1:L, :]
        log_w_state = b_last - b_col + i_col
        m_new = jnp.maximum(b_last + m_prev, jnp.max(log_w_state, axis=0, keepdims=True))
        decay = jnp.exp(b_last + m_prev - m_new)
        w_s = jnp.exp(log_w_state - m_new)
        wv = (w_s * v_aug.astype(F32)).astype(BF16)
        upd = lax.dot_general(k, wv, (((0,), (0,)), ((), ())), preferred_element_type=F32)
        state_ref[h] = decay * state + upd
        m_ref[h:h + 1, :] = jnp.broadcast_to(m_new, (1, m_ref.shape[1]))

        hn = _rms(hval) * gain_ref[:, sl]
        o_ref[:, sl] = (jax.nn.sigmoid(mo_ref[:, sl]) * hn).astype(o_ref.dtype)


def mlstm_mixer(proj, mo, gcol, grow, out_gain, n_heads, q_block):
    S = proj.shape[0]
    L = min(MLSTM_CHUNK, S)
    W = n_heads * HEAD_DIM
    return pl.pallas_call(
        functools.partial(_mlstm_kernel, n_heads=n_heads),
        out_shape=jax.ShapeDtypeStruct((S, W), BF16),
        grid=(S // L,),
        in_specs=[pl.BlockSpec((L, W), lambda c: (c, q_block)),
                  pl.BlockSpec((L, W), lambda c: (c, q_block + 1)),
                  pl.BlockSpec((L, W), lambda c: (c, q_block + 2)),
                  pl.BlockSpec((L, W), lambda c: (c, 0)),
                  pl.BlockSpec((L, gcol.shape[1]), lambda c: (c, 0)),
                  pl.BlockSpec((grow.shape[0], L), lambda c: (0, c)),
                  pl.BlockSpec((1, W), lambda c: (0, 0))],
        out_specs=pl.BlockSpec((L, W), lambda c: (c, 0)),
        scratch_shapes=[pltpu.VMEM((n_heads, HEAD_DIM, 2 * HEAD_DIM), F32),
                        pltpu.VMEM((n_heads, V7X_LANES), F32)],
        compiler_params=pltpu.CompilerParams(
            dimension_semantics=("arbitrary",),
            vmem_limit_bytes=_vmem_limit(2 * L * W * (3 * 2 + 4 + 2) + 16 * L * L * 4)),
        name="mlstm",
    )(proj, proj, proj, mo, gcol, grow, out_gain.reshape(1, W))


def _outproj_kernel(a1_ref, a2_ref, w_ref, x_ref, o_ref):
    k1 = a1_ref.shape[1]
    acc = jnp.dot(a1_ref[...], w_ref[:k1, :], preferred_element_type=F32)
    acc += jnp.dot(a2_ref[...], w_ref[k1:, :], preferred_element_type=F32)
    o_ref[...] = x_ref[...] + acc


def outproj_residual(a1, a2, w, x):
    S, K1 = a1.shape
    K2 = a2.shape[1]
    N = w.shape[1]
    tm, tn = min(MM_ROWS, S), MM_COLS
    return pl.pallas_call(
        _outproj_kernel,
        out_shape=jax.ShapeDtypeStruct((S, N), F32),
        grid=(S // tm, N // tn),
        in_specs=[pl.BlockSpec((tm, K1), lambda i, j: (i, 0)),
                  pl.BlockSpec((tm, K2), lambda i, j: (i, 0)),
                  pl.BlockSpec((K1 + K2, tn), lambda i, j: (0, j)),
                  pl.BlockSpec((tm, tn), lambda i, j: (i, j))],
        out_specs=pl.BlockSpec((tm, tn), lambda i, j: (i, j)),
        compiler_params=pltpu.CompilerParams(
            dimension_semantics=("parallel", "parallel"),
            vmem_limit_bytes=_vmem_limit(2 * (tm * (K1 + K2) * 2 + (K1 + K2) * tn * 2 + 2 * tm * tn * 4)
                                         + tm * tn * 4)),
        name="outproj",
    )(a1, a2, w, x)


def _pool_kernel(x_ref, g_ref, w_ref, b_ref, scale_ref, o_ref, carry_ref):
    i = pl.program_id(0)
    tm = x_ref.shape[0]
    gw = w_ref.shape[1]

    @pl.when(i == 0)
    def _():
        carry_ref[...] = jnp.zeros_like(carry_ref)

    x = x_ref[...]
    hn = _rms(x) * g_ref[...]
    t = i * tm + lax.broadcasted_iota(jnp.int32, (tm, 1), 0)
    for g, w in enumerate(POOL_WINDOWS):
        sl = slice(g * gw, (g + 1) * gw)
        hg = hn[:, sl]
        cur = jnp.concatenate([carry_ref[:, sl], hg], axis=0)
        k = 1
        while k < w:
            cur = cur + pltpu.roll(cur, k, axis=0)
            k *= 2
        window_sum = cur[POOL_HALO:, :]
        count = jnp.minimum(t + 1, w).astype(F32)
        pooled = window_sum / count - hg
        y = jnp.dot(pooled.astype(BF16), w_ref[g], preferred_element_type=F32) + b_ref[:, sl]
        o_ref[:, sl] = x[:, sl] + y * scale_ref[:, sl]
    carry_ref[...] = hn[tm - POOL_HALO:, :]


def pool_mixer_residual(x, g, pool_w, pool_b, pool_scale):
    S, D = x.shape
    tm = min(NORM_ROWS, S)
    G, gw, _ = pool_w.shape
    return pl.pallas_call(
        _pool_kernel,
        out_shape=jax.ShapeDtypeStruct((S, D), F32),
        grid=(S // tm,),
        in_specs=[pl.BlockSpec((tm, D), lambda i: (i, 0)),
                  pl.BlockSpec((1, D), lambda i: (0, 0)),
                  pl.BlockSpec((G, gw, gw), lambda i: (0, 0, 0)),
                  pl.BlockSpec((1, D), lambda i: (0, 0)),
                  pl.BlockSpec((1, D), lambda i: (0, 0))],
        out_specs=pl.BlockSpec((tm, D), lambda i: (i, 0)),
        scratch_shapes=[pltpu.VMEM((POOL_HALO, D), F32)],
        compiler_params=pltpu.CompilerParams(
            dimension_semantics=("arbitrary",),
            vmem_limit_bytes=_vmem_limit(4 * tm * D * 4 + 2 * G * gw * gw * 2 + 6 * tm * D * 4)),
        name="pool_mixer",
    )(x, g.reshape(1, D), pool_w, pool_b.reshape(1, D), pool_scale.reshape(1, D))


def _store_slabs(ref, val):
    n = val.shape[0]
    n_chunks = val.shape[1] // V7X_LANES
    for c in range(n_chunks):
        ref[pl.ds(c, n, stride=n_chunks), :] = val[:, c * V7X_LANES:(c + 1) * V7X_LANES]


def _load_slab_chunk(ref, lead, c, n, n_chunks):
    return ref[lead + (pl.ds(c, n, stride=n_chunks), slice(None))]


def _dot_bf16x3(a, b):
    a_hi = a.astype(BF16)
    a_lo = (a - a_hi.astype(F32)).astype(BF16)
    b_hi = b.astype(BF16)
    b_lo = (b - b_hi.astype(F32)).astype(BF16)
    out = jnp.dot(a_hi, b_hi, preferred_element_type=F32)
    out += jnp.dot(a_hi, b_lo, preferred_element_type=F32)
    out += jnp.dot(a_lo, b_hi, preferred_element_type=F32)
    return out


def _router_kernel(x_ref, g_ref, w_ref, b_ref, hn_ref, route_ref):
    hn = _rms(x_ref[...]) * g_ref[...]
    _store_slabs(hn_ref, hn)
    logits = _dot_bf16x3(hn, w_ref[...]) + b_ref[...]
    lane = lax.broadcasted_iota(jnp.int32, logits.shape, 1).astype(F32)
    n_lanes = float(logits.shape[1])

    def first_argmax(vals):
        top = jnp.max(vals, axis=-1, keepdims=True)
        return top, jnp.min(jnp.where(vals == top, lane, n_lanes), axis=-1, keepdims=True)

    is_group = lane < N_GROUPS
    g_top, g_sel = first_argmax(jnp.where(is_group, logits, NEG_BIG))
    g_w = 1.0 / jnp.sum(jnp.where(is_group, jnp.exp(logits - g_top), 0.0), axis=-1, keepdims=True)
    lo = N_GROUPS + EXPERTS_PER_GROUP * g_sel
    e_logits = jnp.where(jnp.logical_and(lane >= lo, lane < lo + EXPERTS_PER_GROUP), logits, NEG_BIG)
    v1, i1 = first_argmax(e_logits)
    v2, i2 = first_argmax(jnp.where(lane == i1, NEG_BIG, e_logits))
    e21 = jnp.exp(v2 - v1)
    w1 = g_w / (1.0 + e21)
    w2 = g_w * e21 / (1.0 + e21)
    route_ref[...] = jnp.where(lane == 0, i1 - N_GROUPS,
                               jnp.where(lane == 1, i2 - N_GROUPS,
                                         jnp.where(lane == 2, w1, jnp.where(lane == 3, w2, 0.0))))


def router(x, g, w_router, b_router):
    S, D = x.shape
    W = w_router.shape[1]
    tm = min(NORM_ROWS, S)
    n_chunks = D // V7X_LANES
    return pl.pallas_call(
        _router_kernel,
        out_shape=(jax.ShapeDtypeStruct((S * n_chunks, V7X_LANES), F32), jax.ShapeDtypeStruct((S, W), F32)),
        grid=(S // tm,),
        in_specs=[pl.BlockSpec((tm, D), lambda i: (i, 0)),
                  pl.BlockSpec((1, D), lambda i: (0, 0)),
                  pl.BlockSpec((D, W), lambda i: (0, 0)),
                  pl.BlockSpec((1, W), lambda i: (0, 0))],
        out_specs=(pl.BlockSpec((tm * n_chunks, V7X_LANES), lambda i: (i, 0)),
                   pl.BlockSpec((tm, W), lambda i: (i, 0))),
        compiler_params=pltpu.CompilerParams(
            dimension_semantics=("parallel",),
            vmem_limit_bytes=_vmem_limit(2 * tm * D * 8 + 2 * D * W * 4 + 4 * tm * D * 4)),
        name="router",
    )(x, g.reshape(1, D), w_router, b_router)


CAST_ROWS = 128
GATHER_UNROLL = 8
WEIGHT_DMA_PRIORITY = 1


ROW_SLOTS = 3


def _expert_kernel(be_ref, first_ref, next_ref, active_ref, tok_ref, tok1_ref, tok2_ref, hn_hbm, wg_hbm, wu_hbm,
                   wd_hbm, o_ref, xbuf, stage_g, stage_u, stage_d, wg_ref, wu_ref, wd_ref, wsem, gsem,
                   *, layer):
    b = pl.program_id(0)
    n_blocks = pl.num_programs(0)
    n_chunks = wg_ref.shape[0] // V7X_LANES
    R = o_ref.shape[0] // n_chunks
    slot = lax.rem(b, ROW_SLOTS)
    slot2 = lax.rem(b + 2, ROW_SLOTS)

    def row_copy(idx_ref, s, r):
        src = pl.multiple_of(idx_ref[0, r], n_chunks)
        return pltpu.make_async_copy(hn_hbm.at[pl.ds(src, n_chunks), :],
                                     xbuf.at[s, pl.ds(r * n_chunks, n_chunks), :], gsem.at[s])

    def start_rows_loop(idx_ref, s):
        def issue(r, carry):
            row_copy(idx_ref, s, r).start()
            return carry
        lax.fori_loop(0, R, issue, 0, unroll=GATHER_UNROLL)

    def wait_rows(s):
        pltpu.make_async_copy(hn_hbm.at[pl.ds(0, R * n_chunks), :], xbuf.at[s], gsem.at[s]).wait()

    def weight_copies(e):
        return (pltpu.make_async_copy(wg_hbm.at[layer, e], stage_g, wsem.at[0]),
                pltpu.make_async_copy(wu_hbm.at[layer, e], stage_u, wsem.at[1]),
                pltpu.make_async_copy(wd_hbm.at[layer, e], stage_d, wsem.at[2]))

    @pl.when(b == 0)
    def _():
        start_rows_loop(tok_ref, 0)
        start_rows_loop(tok1_ref, 1)
        for cp in weight_copies(be_ref[0]):
            cp.start(priority=WEIGHT_DMA_PRIORITY)

    @pl.when(first_ref[b] == 1)
    def _():
        for cp in weight_copies(be_ref[b]):
            cp.wait()
        for stage, dst in ((stage_g, wg_ref), (stage_u, wu_ref), (stage_d, wd_ref)):
            def cast_rows(r, carry, stage=stage, dst=dst):
                rows = pl.ds(pl.multiple_of(r * CAST_ROWS, CAST_ROWS), CAST_ROWS)
                dst[rows, :] = stage[rows, :].astype(BF16)
                return carry
            lax.fori_loop(0, stage.shape[0] // CAST_ROWS, cast_rows, 0)

        @pl.when(next_ref[b] >= 0)
        def _():
            for cp in weight_copies(next_ref[b]):
                cp.start(priority=WEIGHT_DMA_PRIORITY)

    @pl.when(active_ref[b] == 1)
    def _():
        wait_rows(slot)
        x = jnp.concatenate([_load_slab_chunk(xbuf, (slot,), c, R, n_chunks).astype(BF16)
                             for c in range(n_chunks)], axis=1)
        for r in range(R):
            row_copy(tok2_ref, slot2, r).start()
        a = jnp.dot(x, wg_ref[...], preferred_element_type=F32)
        u = jnp.dot(x, wu_ref[...], preferred_element_type=F32)
        hmid = (a * jax.nn.sigmoid(a) * u).astype(BF16)
        _store_slabs(o_ref, jnp.dot(hmid, wd_ref[...], preferred_element_type=F32))

    @pl.when(active_ref[b] == 0)
    def _():
        wait_rows(slot)
        start_rows_loop(tok2_ref, slot2)
        o_ref[...] = jnp.zeros_like(o_ref)

    @pl.when(b == n_blocks - 1)
    def _():
        wait_rows(lax.rem(b + 1, ROW_SLOTS))
        wait_rows(slot2)


def expert_blocks(block_e, first, next_e, active, row_tok, hn, w_gate, w_up, w_down, layer):
    n_blocks, _, R = row_tok.shape
    assert n_blocks >= 2
    D, Dh = w_gate.shape[2:]
    n_chunks = D // V7X_LANES
    hbm = pl.BlockSpec(memory_space=pl.ANY)

    def tok_spec(ahead):
        return pl.BlockSpec((None, 1, R), lambda b, *_: (jnp.minimum(b + ahead, n_blocks - 1), 0, 0),
                            memory_space=pltpu.SMEM)

    grid_spec = pltpu.PrefetchScalarGridSpec(
        num_scalar_prefetch=4,
        grid=(n_blocks,),
        in_specs=[tok_spec(0), tok_spec(1), tok_spec(2), hbm, hbm, hbm, hbm],
        out_specs=pl.BlockSpec((R * n_chunks, V7X_LANES), lambda b, *_: (b, 0)),
        scratch_shapes=[pltpu.VMEM((ROW_SLOTS, R * n_chunks, V7X_LANES), F32),
                        pltpu.VMEM((D, Dh), F32), pltpu.VMEM((D, Dh), F32), pltpu.VMEM((Dh, D), F32),
                        pltpu.VMEM((D, Dh), BF16), pltpu.VMEM((D, Dh), BF16), pltpu.VMEM((Dh, D), BF16),
                        pltpu.SemaphoreType.DMA((3,)), pltpu.SemaphoreType.DMA((ROW_SLOTS,))],
    )
    return pl.pallas_call(
        functools.partial(_expert_kernel, layer=layer),
        out_shape=jax.ShapeDtypeStruct((n_blocks * R * n_chunks, V7X_LANES), F32),
        grid_spec=grid_spec,
        compiler_params=pltpu.CompilerParams(
            dimension_semantics=("arbitrary",),
            vmem_limit_bytes=_vmem_limit(3 * D * Dh * (4 + 2) + (ROW_SLOTS + 2) * R * D * 4 + R * D * 2
                                         + 6 * R * Dh * 4)),
        name="moe_experts",
    )(block_e, first, next_e, active, row_tok, row_tok, row_tok, hn, w_gate, w_up, w_down)


COMBINE_ROWS = 256


def _combine_kernel(idx_ref, idx_next_ref, x_ref, route_ref, yb_hbm, o_ref, cbuf, gsem):
    i = pl.program_id(0)
    n_tiles = pl.num_programs(0)
    tm, D = x_ref.shape
    n_chunks = D // V7X_LANES
    slot = lax.rem(i, 2)

    def row_copy(idx, s, r, k):
        src = pl.multiple_of(idx[0, TOP_K * r + k], n_chunks)
        return pltpu.make_async_copy(yb_hbm.at[pl.ds(src, n_chunks), :],
                                     cbuf.at[s, k, pl.ds(r * n_chunks, n_chunks), :], gsem.at[s])

    def wait_rows(s):
        for k in range(TOP_K):
            pltpu.make_async_copy(yb_hbm.at[pl.ds(0, tm * n_chunks), :], cbuf.at[s, k], gsem.at[s]).wait()

    @pl.when(i == 0)
    def _():
        def issue(r, carry):
            for k in range(TOP_K):
                row_copy(idx_ref, 0, r, k).start()
            return carry
        lax.fori_loop(0, tm, issue, 0, unroll=GATHER_UNROLL)

    wait_rows(slot)
    for r in range(tm):
        for k in range(TOP_K):
            row_copy(idx_next_ref, 1 - slot, r, k).start(priority=k % 2)
    for c in range(n_chunks):
        cols = slice(c * V7X_LANES, (c + 1) * V7X_LANES)
        acc = x_ref[:, cols]
        for k in range(TOP_K):
            acc = acc + route_ref[:, TOP_K + k:TOP_K + k + 1] * _load_slab_chunk(cbuf, (slot, k), c, tm, n_chunks)
        o_ref[:, cols] = acc

    @pl.when(i == n_tiles - 1)
    def _():
        wait_rows(1 - slot)


def combine_residual(x, route, dest, yb):
    T, D = x.shape
    tm = min(COMBINE_ROWS, T)
    n_tiles = T // tm
    idx = (dest * (D // V7X_LANES)).reshape(n_tiles, 1, tm * TOP_K)
    return pl.pallas_call(
        _combine_kernel,
        out_shape=jax.ShapeDtypeStruct((T, D), F32),
        grid=(n_tiles,),
        in_specs=[pl.BlockSpec((None, 1, tm * TOP_K), lambda i: (i, 0, 0), memory_space=pltpu.SMEM),
                  pl.BlockSpec((None, 1, tm * TOP_K), lambda i: (jnp.minimum(i + 1, n_tiles - 1), 0, 0),
                               memory_space=pltpu.SMEM),
                  pl.BlockSpec((tm, D), lambda i: (i, 0)),
                  pl.BlockSpec((tm, route.shape[1]), lambda i: (i, 0)),
                  pl.BlockSpec(memory_space=pl.ANY)],
        out_specs=pl.BlockSpec((tm, D), lambda i: (i, 0)),
        scratch_shapes=[pltpu.VMEM((2, TOP_K, tm * (D // V7X_LANES), V7X_LANES), F32),
                        pltpu.SemaphoreType.DMA((2,))],
        compiler_params=pltpu.CompilerParams(
            dimension_semantics=("arbitrary",),
            vmem_limit_bytes=_vmem_limit(2 * TOP_K * tm * D * 4 + 6 * tm * D * 4)),
        name="moe_combine",
    )(idx, idx, x, route, yb)


def moe_residual(x, g, rgw, rgb, rew, reb, w_gate, w_up, w_down, layer):
    T, D = x.shape
    R = MOE_ROWS
    assert TOP_K == 2
    pad = V7X_LANES - N_GROUPS - N_EXPERTS
    w_router = jnp.concatenate([rgw, rew, jnp.zeros((D, pad), F32)], axis=1)
    b_router = jnp.concatenate([rgb, reb, jnp.zeros((pad,), F32)]).reshape(1, V7X_LANES)
    hn, route = router(x, g, w_router, b_router)
    expert_id = route[:, :TOP_K].astype(jnp.int32)

    n_assign = T * TOP_K
    flat_e = expert_id.reshape(-1)
    onehot = (flat_e[:, None] == jnp.arange(N_EXPERTS, dtype=jnp.int32)[None, :]).astype(jnp.int32)
    running = jnp.cumsum(onehot, axis=0)
    counts = running[-1]
    rank = jnp.take_along_axis(running, flat_e[:, None], axis=1)[:, 0] - 1
    padded = (counts + R - 1) // R * R
    pends = jnp.cumsum(padded)
    pstarts = pends - padded
    dest = pstarts[flat_e] + rank
    n_blocks = (n_assign + N_EXPERTS * (R - 1) + R - 1) // R
    flat_tok = jnp.repeat(jnp.arange(T, dtype=jnp.int32), TOP_K)
    row_tok = jnp.zeros((n_blocks * R,), jnp.int32).at[dest].set(flat_tok)

    blk_start = jnp.arange(n_blocks, dtype=jnp.int32) * R
    active = blk_start < pends[-1]
    block_e = jnp.minimum(jnp.sum(blk_start[:, None] >= pends[None, :], axis=1), N_EXPERTS - 1).astype(jnp.int32)
    prev_e = jnp.concatenate([jnp.full((1,), -1, jnp.int32), block_e[:-1]])
    first = jnp.logical_and(active, block_e != prev_e)
    later = lax.cummin(jnp.where(first, block_e, N_EXPERTS)[::-1])[::-1]
    next_e = jnp.concatenate([later[1:], jnp.full((1,), N_EXPERTS, jnp.int32)])
    next_e = jnp.where(next_e >= N_EXPERTS, -1, next_e).astype(jnp.int32)

    yb = expert_blocks(block_e, first.astype(jnp.int32), next_e, active.astype(jnp.int32),
                       (row_tok * (D // V7X_LANES)).reshape(n_blocks, 1, R), hn, w_gate, w_up, w_down, layer)
    return combine_residual(x, route, dest.reshape(T, TOP_K), yb)


def fox_mlstm_residual(xt, norm_g, w, fox_f_bias, fox_q_gain, fox_k_gain, mlstm_i_bias, mlstm_f_bias,
                       mlstm_out_gain, w_out):
    S, D = xt.shape
    H = fox_f_bias.shape[0]
    assert mlstm_i_bias.shape[0] == H and 3 * H <= 32
    fw = H * HEAD_DIM
    o_ff = 3 * fw
    o_mq = o_ff + H
    o_mi = o_mq + 3 * fw
    o_mo = o_mi + 2 * H
    w_main = jnp.concatenate([w[:, :o_ff], w[:, o_mq:o_mi]], axis=1).astype(BF16)
    gate_pad = V7X_LANES - 3 * H
    w_aux = jnp.concatenate([w[:, o_mo:], w[:, o_ff:o_mq], w[:, o_mi:o_mo],
                             jnp.zeros((D, gate_pad), F32)], axis=1).astype(BF16)
    gains = jnp.concatenate([fox_q_gain[None] * (LOG2E * HEAD_DIM ** -0.5), fox_k_gain[None],
                             jnp.zeros((6, HEAD_DIM), F32)], axis=0)
    bias_row = jnp.concatenate([fox_f_bias, mlstm_i_bias, mlstm_f_bias,
                                jnp.zeros((gate_pad,), F32)]).reshape(1, V7X_LANES)

    hn = rmsnorm_bf16(xt, norm_g)
    proj = inproj_main(hn, w_main, gains)
    mo, gates_pre = inproj_aux(hn, w_aux, fw)
    gcol = gate_activations(gates_pre, bias_row, H)
    grow = gcol[:, :32].T
    cq = jnp.broadcast_to(grow[:H, :, None], (H, S, V7X_LANES))
    ck = grow[:H].reshape(H, 1, S)
    y_fox = fox_attention(proj, cq, ck, H, gains[0], gains[1])
    y_mlstm = mlstm_mixer(proj, mo, gcol, grow, mlstm_out_gain, H, 3)
    return outproj_residual(y_fox, y_mlstm, w_out.astype(BF16), xt)


def kernel(x, norm_mix, norm_ffn, w_in, fox_f_bias, fox_q_gain, fox_k_gain, mlstm_i_bias, mlstm_f_bias,
           mlstm_out_gain, w_out, pool_w, pool_b, pool_scale, router_group_w, router_group_b,
           router_expert_w, router_expert_b, w_gate, w_up, w_down):
    B, S, D = x.shape
    assert B == 1
    depth = norm_mix.shape[0]
    xt = x.reshape(S, D)

    for layer in range(depth):
        j = layer // 2
        if layer % 2 == 0:
            xt = fox_mlstm_residual(xt, norm_mix[layer], w_in[j], fox_f_bias[j], fox_q_gain[j],
                                    fox_k_gain[j], mlstm_i_bias[j], mlstm_f_bias[j], mlstm_out_gain[j],
                                    w_out[j])
        else:
            xt = pool_mixer_residual(xt, norm_mix[layer], pool_w[j].astype(BF16), pool_b[j], pool_scale[j])
        xt = moe_residual(xt, norm_ffn[layer], router_group_w[layer], router_group_b[layer],
                          router_expert_w[layer], router_expert_b[layer],
                          w_gate, w_up, w_down, layer)
    return xt.reshape(B, S, D)
```

```python
import functools

import jax
import jax.numpy as jnp
from jax import lax
from jax.experimental import pallas as pl
from jax.experimental.pallas import tpu as pltpu

F32 = jnp.float32
BF16 = jnp.bfloat16

HEAD_DIM = 128
GATE_SOFTCAP = 15.0
POOL_WINDOWS = (2, 4, 8, 16)
POOL_HALO = 16
N_GROUPS = 4
EXPERTS_PER_GROUP = 8
N_EXPERTS = N_GROUPS * EXPERTS_PER_GROUP
TOP_K = 2
RMS_EPS = 1e-6

V7X_LANES = 128
V7X_VMEM_BYTES = 64 * 1024 * 1024

NORM_ROWS = 512
MM_ROWS = 1024
MM_COLS = 1024
ATT_Q = 1024
ATT_K = 512
MLSTM_CHUNK = 256
MOE_ROWS = 256
NEG_BIG = -1e30
LOG2E = 1.4426950408889634


def _vmem_limit(nbytes):
    return int(min(max(nbytes * 3 // 2, 16 * 1024 * 1024), V7X_VMEM_BYTES - 8 * 1024 * 1024))


def _rms(x, eps=RMS_EPS):
    return x * lax.rsqrt(jnp.mean(x * x, axis=-1, keepdims=True) + eps)


def _log_sigmoid(x):
    return -(jnp.maximum(-x, 0.0) + jnp.log1p(jnp.exp(-jnp.abs(x))))


def _rmsnorm_kernel(x_ref, g_ref, o_ref):
    o_ref[...] = (_rms(x_ref[...]) * g_ref[...]).astype(o_ref.dtype)


def rmsnorm_bf16(x, g):
    S, D = x.shape
    return pl.pallas_call(
        _rmsnorm_kernel,
        out_shape=jax.ShapeDtypeStruct((S, D), BF16),
        grid=(S // NORM_ROWS,),
        in_specs=[pl.BlockSpec((NORM_ROWS, D), lambda i: (i, 0)),
                  pl.BlockSpec((1, D), lambda i: (0, 0))],
        out_specs=pl.BlockSpec((NORM_ROWS, D), lambda i: (i, 0)),
        compiler_params=pltpu.CompilerParams(
            dimension_semantics=("parallel",),
            vmem_limit_bytes=_vmem_limit(2 * NORM_ROWS * D * 6)),
        name="rmsnorm",
    )(x, g.reshape(1, D))


def _inproj_main_kernel(a_ref, w_ref, gain_ref, o_ref, *, n_heads_per_tile):
    j = pl.program_id(1)
    acc = jnp.dot(a_ref[...], w_ref[...], preferred_element_type=F32)

    @pl.when(j < 2)
    def _():
        g = gain_ref[pl.ds(j, 1), :]
        for h in range(n_heads_per_tile):
            a = acc[:, h * HEAD_DIM:(h + 1) * HEAD_DIM]
            o_ref[:, h * HEAD_DIM:(h + 1) * HEAD_DIM] = (_rms(a) * g).astype(o_ref.dtype)

    @pl.when(j == 4)
    def _():
        o_ref[...] = (acc * (HEAD_DIM ** -0.5)).astype(o_ref.dtype)

    @pl.when(jnp.logical_and(j >= 2, j != 4))
    def _():
        o_ref[...] = acc.astype(o_ref.dtype)


def inproj_main(hn, w_main, gains):
    S, D = hn.shape
    N = w_main.shape[1]
    tm, tn = min(MM_ROWS, S), MM_COLS
    return pl.pallas_call(
        functools.partial(_inproj_main_kernel, n_heads_per_tile=tn // HEAD_DIM),
        out_shape=jax.ShapeDtypeStruct((S, N), BF16),
        grid=(S // tm, N // tn),
        in_specs=[pl.BlockSpec((tm, D), lambda i, j: (i, 0)),
                  pl.BlockSpec((D, tn), lambda i, j: (0, j)),
                  pl.BlockSpec((8, HEAD_DIM), lambda i, j: (0, 0))],
        out_specs=pl.BlockSpec((tm, tn), lambda i, j: (i, j)),
        compiler_params=pltpu.CompilerParams(
            dimension_semantics=("parallel", "parallel"),
            vmem_limit_bytes=_vmem_limit(2 * (tm * D * 2 + D * tn * 2 + tm * tn * 2) + 2 * tm * tn * 4)),
        name="inproj_main",
    )(hn, w_main, gains)


def _inproj_aux_kernel(a_ref, w_ref, mo_ref, gate_ref):
    acc = jnp.dot(a_ref[...], w_ref[...], preferred_element_type=F32)
    n_mo = mo_ref.shape[1]
    mo_ref[...] = acc[:, :n_mo]
    gate_ref[...] = acc[:, n_mo:]


def inproj_aux(hn, w_aux, n_mo):
    S, D = hn.shape
    N = w_aux.shape[1]
    tm = min(NORM_ROWS, S)
    return pl.pallas_call(
        _inproj_aux_kernel,
        out_shape=(jax.ShapeDtypeStruct((S, n_mo), F32),
                   jax.ShapeDtypeStruct((S, N - n_mo), F32)),
        grid=(S // tm,),
        in_specs=[pl.BlockSpec((tm, D), lambda i: (i, 0)),
                  pl.BlockSpec((D, N), lambda i: (0, 0))],
        out_specs=(pl.BlockSpec((tm, n_mo), lambda i: (i, 0)),
                   pl.BlockSpec((tm, N - n_mo), lambda i: (i, 0))),
        compiler_params=pltpu.CompilerParams(
            dimension_semantics=("parallel",),
            vmem_limit_bytes=_vmem_limit(2 * (tm * D * 2 + D * N * 2 + tm * N * 4) + tm * N * 4)),
        name="inproj_aux",
    )(hn, w_aux)


def _split3_dot(tri, val):
    v1 = val.astype(BF16)
    r1 = val - v1.astype(F32)
    v2 = r1.astype(BF16)
    v3 = (r1 - v2.astype(F32)).astype(BF16)
    out = jnp.dot(tri, v1, preferred_element_type=F32)
    out += jnp.dot(tri, v2, preferred_element_type=F32)
    out += jnp.dot(tri, v3, preferred_element_type=F32)
    return out


def _gates_kernel(g_ref, bias_ref, o_ref, carry_ref, *, n_heads):
    @pl.when(pl.program_id(0) == 0)
    def _():
        carry_ref[...] = jnp.zeros_like(carry_ref)

    rows = g_ref.shape[0]
    z = g_ref[...] + bias_ref[...]
    lane = lax.broadcasted_iota(jnp.int32, z.shape, 1)
    capped = GATE_SOFTCAP * jnp.tanh(z / GATE_SOFTCAP)
    is_fox = lane < n_heads
    is_i = jnp.logical_and(lane >= n_heads, lane < 2 * n_heads)
    is_f = jnp.logical_and(lane >= 2 * n_heads, lane < 3 * n_heads)
    logf = jnp.where(is_fox, _log_sigmoid(z), jnp.where(is_f, _log_sigmoid(capped), 0.0))
    r = lax.broadcasted_iota(jnp.int32, (rows, rows), 0)
    c = lax.broadcasted_iota(jnp.int32, (rows, rows), 1)
    tri = jnp.where(r >= c, 1.0, 0.0).astype(BF16)
    cum = _split3_dot(tri, logf)
    glob = cum + carry_ref[...]
    o_ref[...] = jnp.where(is_fox, glob * LOG2E, jnp.where(is_i, capped, cum))
    carry_ref[...] = glob[rows - 1:rows, :]


def gate_activations(gates_pre, bias_row, n_heads):
    S, W = gates_pre.shape
    tb = MLSTM_CHUNK
    return pl.pallas_call(
        functools.partial(_gates_kernel, n_heads=n_heads),
        out_shape=jax.ShapeDtypeStruct((S, W), F32),
        grid=(S // tb,),
        in_specs=[pl.BlockSpec((tb, W), lambda i: (i, 0)),
                  pl.BlockSpec((1, W), lambda i: (0, 0))],
        out_specs=pl.BlockSpec((tb, W), lambda i: (i, 0)),
        scratch_shapes=[pltpu.VMEM((1, W), F32)],
        compiler_params=pltpu.CompilerParams(dimension_semantics=("arbitrary",)),
        name="gate_activations",
    )(gates_pre, bias_row)


def _fox_kernel(first_ref, q_ref, k_ref, v_ref, cq_ref, ck_ref, o_ref, m_ref, acc_ref, s_ref, p_ref, alpha_ref,
                *, n_sub):
    i = pl.program_id(1)
    d = HEAD_DIM
    tk = q_ref.shape[0] // n_sub
    assert n_sub % 2 == 0

    m_ref[...] = jnp.full_like(m_ref, NEG_BIG)
    acc_ref[...] = jnp.zeros_like(acc_ref)
    p_ref[1] = jnp.zeros_like(p_ref[1])
    alpha_ref[1] = jnp.ones_like(alpha_ref[1])
    lane = lax.broadcasted_iota(jnp.int32, (tk, d), 1)
    ones_col = jnp.where(lane == 0, 1.0, 0.0).astype(BF16)
    row = lax.broadcasted_iota(jnp.int32, (tk, tk), 0)
    col = lax.broadcasted_iota(jnp.int32, (tk, tk), 1)
    causal = col <= row

    def qk_stage(sub, j, par):
        start = pl.multiple_of(j * tk, tk)
        s = lax.dot_general(q_ref[pl.ds(sub * tk, tk), :], k_ref[pl.ds(start, tk), :],
                            (((1,), (1,)), ((), ())), preferred_element_type=F32)
        s_ref[par, sub] = s - ck_ref[:, pl.ds(start, tk)]

    def sm_stage(sub, par, masked):
        rows = pl.ds(sub * tk, tk)
        s = s_ref[par, sub]
        if masked:
            s = jnp.where(causal, s, NEG_BIG)
        cq = cq_ref[rows, :]
        m_prev = m_ref[rows, :]
        m_new = jnp.maximum(m_prev, jnp.max(s, axis=-1, keepdims=True) + cq)
        p_ref[par, sub] = jnp.exp2(s - jnp.tile(m_new - cq, (1, tk // V7X_LANES))).astype(BF16)
        alpha_ref[par, rows, :] = jnp.exp2(m_prev - m_new)
        m_ref[rows, :] = m_new

    def pv_stage(sub, j, par):
        rows = pl.ds(sub * tk, tk)
        start = pl.multiple_of(j * tk, tk)
        v_aug = jnp.concatenate([v_ref[pl.ds(start, tk), :], ones_col], axis=1)
        acc_ref[rows, :] = (jnp.tile(alpha_ref[par, rows, :], (1, 2)) * acc_ref[rows, :]
                            + jnp.dot(p_ref[par, sub], v_aug, preferred_element_type=F32))

    n_full = i * n_sub
    first = first_ref[pl.program_id(0) * pl.num_programs(1) + i]
    for sub in range(n_sub):
        qk_stage(sub, first, 0)

    def body(tt, carry):
        for par in (0, 1):
            step = 2 * tt + par
            for sub in range(n_sub):
                qk_stage(sub, step + 1, 1 - par)
                sm_stage(sub, par, masked=False)
                pv_stage(sub, jnp.maximum(step - 1, 0), 1 - par)
        return carry

    lax.fori_loop(first // 2, n_full // 2, body, 0)
    for kk in range(n_sub + 1):
        par = kk % 2
        for sub in range(n_sub):
            if kk + 1 <= sub:
                qk_stage(sub, n_full + kk + 1, 1 - par)
            if kk <= sub:
                sm_stage(sub, par, masked=(kk == sub))
            if kk - 1 <= sub:
                pv_stage(sub, jnp.maximum(n_full + kk - 1, 0), 1 - par)
    acc = acc_ref[...]
    o_ref[...] = (acc[:, :d] / acc[:, d:d + 1]).astype(o_ref.dtype)


UNDERFLOW_LOG2 = 152.0


def _fox_first_blocks(q_gain, k_gain, c2, tk, n_sub):
    H, S = c2.shape
    nb = S // tk
    slack = 1.0 + 2.0 ** -6
    qk_bound = HEAD_DIM * jnp.max(jnp.abs(q_gain)) * jnp.max(jnp.abs(k_gain)) * slack
    cb = c2.reshape(H, nb, tk)
    c_hi, c_lo = jnp.max(cb, axis=-1), jnp.min(cb, axis=-1)
    upper = qk_bound + c_hi[:, :, None] - c_lo[:, None, :]
    lower = -qk_bound
    j_idx = jnp.arange(nb, dtype=jnp.int32)
    needed = jnp.logical_or(upper >= lower - UNDERFLOW_LOG2, j_idx[None, None, :] >= j_idx[None, :, None])
    first = jnp.min(jnp.where(needed, j_idx[None, None, :], nb), axis=-1)
    first = jnp.min(first.reshape(H, nb // n_sub, n_sub), axis=-1)
    return ((first // 2) * 2).reshape(-1).astype(jnp.int32)


def fox_attention(proj, cq, ck, n_heads, q_gain, k_gain):
    S = proj.shape[0]
    tk = min(ATT_K, S)
    n_sub = max(1, min(ATT_Q, S) // tk)
    tq = n_sub * tk
    H = n_heads
    first = _fox_first_blocks(q_gain, k_gain, ck.reshape(H, S), tk, n_sub)
    grid_spec = pltpu.PrefetchScalarGridSpec(
        num_scalar_prefetch=1,
        grid=(H, S // tq),
        in_specs=[pl.BlockSpec((tq, HEAD_DIM), lambda h, i, *_: (i, h)),
                  pl.BlockSpec((S, HEAD_DIM), lambda h, i, *_: (0, H + h)),
                  pl.BlockSpec((S, HEAD_DIM), lambda h, i, *_: (0, 2 * H + h)),
                  pl.BlockSpec((None, tq, V7X_LANES), lambda h, i, *_: (h, i, 0)),
                  pl.BlockSpec((None, 1, S), lambda h, i, *_: (h, 0, 0))],
        out_specs=pl.BlockSpec((tq, HEAD_DIM), lambda h, i, *_: (i, h)),
        scratch_shapes=[pltpu.VMEM((tq, V7X_LANES), F32), pltpu.VMEM((tq, 2 * HEAD_DIM), F32),
                        pltpu.VMEM((2, n_sub, tk, tk), F32), pltpu.VMEM((2, n_sub, tk, tk), BF16),
                        pltpu.VMEM((2, tq, V7X_LANES), F32)],
    )
    return pl.pallas_call(
        functools.partial(_fox_kernel, n_sub=n_sub),
        out_shape=jax.ShapeDtypeStruct((S, H * HEAD_DIM), BF16),
        grid_spec=grid_spec,
        compiler_params=pltpu.CompilerParams(
            dimension_semantics=("parallel", "arbitrary"),
            vmem_limit_bytes=_vmem_limit(4 * S * HEAD_DIM * 2 + 2 * n_sub * tk * tk * (4 + 2)
                                         + 4 * tk * tk * 4 + 12 * tq * HEAD_DIM * 4 + 16 * S * 4)),
        name="fox_attention",
    )(first, proj, proj, proj, cq, ck)


def _mlstm_kernel(q_ref, k_ref, v_ref, mo_ref, gcol_ref, grow_ref, gain_ref, o_ref,
                  state_ref, m_ref, *, n_heads):
    L = q_ref.shape[0]
    d = HEAD_DIM

    @pl.when(pl.program_id(0) == 0)
    def _():
        state_ref[...] = jnp.zeros_like(state_ref)
        m_ref[...] = jnp.zeros_like(m_ref)

    row = lax.broadcasted_iota(jnp.int32, (L, L), 0)
    col = lax.broadcasted_iota(jnp.int32, (L, L), 1)
    causal = col <= row
    lane = lax.broadcasted_iota(jnp.int32, (L, d), 1)
    ones_col = jnp.where(lane == 0, 1.0, 0.0).astype(BF16)

    gcol = gcol_ref[...]
    grow = grow_ref[...]
    for h in range(n_heads):
        sl = slice(h * d, (h + 1) * d)
        q = q_ref[:, sl]
        k = k_ref[:, sl]
        v = v_ref[:, sl]
        i_col = gcol[:, n_heads + h:n_heads + h + 1]
        b_col = gcol[:, 2 * n_heads + h:2 * n_heads + h + 1]
        i_row = grow[n_heads + h:n_heads + h + 1, :]
        b_row = grow[2 * n_heads + h:2 * n_heads + h + 1, :]
        m_prev = m_ref[h:h + 1, 0:1]
        state = state_ref[h]

        log_intra = jnp.where(causal, b_col - b_row + i_row, NEG_BIG)
        log_inter = b_col + m_prev
        m_t = jnp.maximum(log_inter, jnp.max(log_intra, axis=-1, keepdims=True))
        w_intra = jnp.exp(log_intra - m_t)
        w_inter = jnp.exp(log_inter - m_t)
        qk = lax.dot_general(q, k, (((1,), (1,)), ((), ())), preferred_element_type=F32) * w_intra
        v_aug = jnp.concatenate([v, ones_col], axis=1)
        tot = jnp.dot(qk.astype(BF16), v_aug, preferred_element_type=F32)
        tot = tot + w_inter * jnp.dot(q, state.astype(BF16), preferred_element_type=F32)
        num = tot[:, :d]
        den = tot[:, d:d + 1]
        hval = num / jnp.maximum(jnp.abs(den), jnp.exp(-m_t))

        b_last = b_col[L - 1:L, :]
        log_w_state = b_last - b_col + i_col
        m_new = jnp.maximum(b_last + m_prev, jnp.max(log_w_state, axis=0, keepdims=True))
        decay = jnp.exp(b_last + m_prev - m_new)
        w_s = jnp.exp(log_w_state - m_new)
        wv = (w_s * v_aug.astype(F32)).astype(BF16)
        upd = lax.dot_general(k, wv, (((0,), (0,)), ((), ())), preferred_element_type=F32)
        state_ref[h] = decay * state + upd
        m_ref[h:h + 1, :] = jnp.broadcast_to(m_new, (1, m_ref.shape[1]))

        hn = _rms(hval) * gain_ref[:, sl]
        o_ref[:, sl] = (jax.nn.sigmoid(mo_ref[:, sl]) * hn).astype(o_ref.dtype)


def mlstm_mixer(proj, mo, gcol, grow, out_gain, n_heads, q_block):
    S = proj.shape[0]
    L = min(MLSTM_CHUNK, S)
    W = n_heads * HEAD_DIM
    return pl.pallas_call(
        functools.partial(_mlstm_kernel, n_heads=n_heads),
        out_shape=jax.ShapeDtypeStruct((S, W), BF16),
        grid=(S // L,),
        in_specs=[pl.BlockSpec((L, W), lambda c: (c, q_block)),
                  pl.BlockSpec((L, W), lambda c: (c, q_block + 1)),
                  pl.BlockSpec((L, W), lambda c: (c, q_block + 2)),
                  pl.BlockSpec((L, W), lambda c: (c, 0)),
                  pl.BlockSpec((L, gcol.shape[1]), lambda c: (c, 0)),
                  pl.BlockSpec((grow.shape[0], L), lambda c: (0, c)),
                  pl.BlockSpec((1, W), lambda c: (0, 0))],
        out_specs=pl.BlockSpec((L, W), lambda c: (c, 0)),
        scratch_shapes=[pltpu.VMEM((n_heads, HEAD_DIM, 2 * HEAD_DIM), F32),
                        pltpu.VMEM((n_heads, V7X_LANES), F32)],
        compiler_params=pltpu.CompilerParams(
            dimension_semantics=("arbitrary",),
            vmem_limit_bytes=_vmem_limit(2 * L * W * (3 * 2 + 4 + 2) + 16 * L * L * 4)),
        name="mlstm",
    )(proj, proj, proj, mo, gcol, grow, out_gain.reshape(1, W))


def _outproj_kernel(a1_ref, a2_ref, w_ref, x_ref, o_ref):
    k1 = a1_ref.shape[1]
    acc = jnp.dot(a1_ref[...], w_ref[:k1, :], preferred_element_type=F32)
    acc += jnp.dot(a2_ref[...], w_ref[k1:, :], preferred_element_type=F32)
    o_ref[...] = x_ref[...] + acc


def outproj_residual(a1, a2, w, x):
    S, K1 = a1.shape
    K2 = a2.shape[1]
    N = w.shape[1]
    tm, tn = min(MM_ROWS, S), MM_COLS
    return pl.pallas_call(
        _outproj_kernel,
        out_shape=jax.ShapeDtypeStruct((S, N), F32),
        grid=(S // tm, N // tn),
        in_specs=[pl.BlockSpec((tm, K1), lambda i, j: (i, 0)),
                  pl.BlockSpec((tm, K2), lambda i, j: (i, 0)),
                  pl.BlockSpec((K1 + K2, tn), lambda i, j: (0, j)),
                  pl.BlockSpec((tm, tn), lambda i, j: (i, j))],
        out_specs=pl.BlockSpec((tm, tn), lambda i, j: (i, j)),
        compiler_params=pltpu.CompilerParams(
            dimension_semantics=("parallel", "parallel"),
            vmem_limit_bytes=_vmem_limit(2 * (tm * (K1 + K2) * 2 + (K1 + K2) * tn * 2 + 2 * tm * tn * 4)
                                         + tm * tn * 4)),
        name="outproj",
    )(a1, a2, w, x)


def _pool_kernel(x_ref, g_ref, w_ref, b_ref, scale_ref, o_ref, carry_ref):
    i = pl.program_id(0)
    tm = x_ref.shape[0]
    gw = w_ref.shape[1]

    @pl.when(i == 0)
    def _():
        carry_ref[...] = jnp.zeros_like(carry_ref)

    x = x_ref[...]
    hn = _rms(x) * g_ref[...]
    t = i * tm + lax.broadcasted_iota(jnp.int32, (tm, 1), 0)
    for g, w in enumerate(POOL_WINDOWS):
        sl = slice(g * gw, (g + 1) * gw)
        hg = hn[:, sl]
        cur = jnp.concatenate([carry_ref[:, sl], hg], axis=0)
        k = 1
        while k < w:
            cur = cur + pltpu.roll(cur, k, axis=0)
            k *= 2
        window_sum = cur[POOL_HALO:, :]
        count = jnp.minimum(t + 1, w).astype(F32)
        pooled = window_sum / count - hg
        y = jnp.dot(pooled.astype(BF16), w_ref[g], preferred_element_type=F32) + b_ref[:, sl]
        o_ref[:, sl] = x[:, sl] + y * scale_ref[:, sl]
    carry_ref[...] = hn[tm - POOL_HALO:, :]


def pool_mixer_residual(x, g, pool_w, pool_b, pool_scale):
    S, D = x.shape
    tm = min(NORM_ROWS, S)
    G, gw, _ = pool_w.shape
    return pl.pallas_call(
        _pool_kernel,
        out_shape=jax.ShapeDtypeStruct((S, D), F32),
        grid=(S // tm,),
        in_specs=[pl.BlockSpec((tm, D), lambda i: (i, 0)),
                  pl.BlockSpec((1, D), lambda i: (0, 0)),
                  pl.BlockSpec((G, gw, gw), lambda i: (0, 0, 0)),
                  pl.BlockSpec((1, D), lambda i: (0, 0)),
                  pl.BlockSpec((1, D), lambda i: (0, 0))],
        out_specs=pl.BlockSpec((tm, D), lambda i: (i, 0)),
        scratch_shapes=[pltpu.VMEM((POOL_HALO, D), F32)],
        compiler_params=pltpu.CompilerParams(
            dimension_semantics=("arbitrary",),
            vmem_limit_bytes=_vmem_limit(4 * tm * D * 4 + 2 * G * gw * gw * 2 + 6 * tm * D * 4)),
        name="pool_mixer",
    )(x, g.reshape(1, D), pool_w, pool_b.reshape(1, D), pool_scale.reshape(1, D))


def _store_slabs(ref, val):
    n = val.shape[0]
    n_chunks = val.shape[1] // V7X_LANES
    for c in range(n_chunks):
        ref[pl.ds(c, n, stride=n_chunks), :] = val[:, c * V7X_LANES:(c + 1) * V7X_LANES]


def _load_slab_chunk(ref, lead, c, n, pitch):
    return ref[lead + (pl.ds(c, n, stride=pitch), slice(None))]


def _gather_pitch(n_chunks):
    return n_chunks + 4 if n_chunks % 8 == 0 else n_chunks


def _dot_bf16x3(a, b):
    a_hi = a.astype(BF16)
    a_lo = (a - a_hi.astype(F32)).astype(BF16)
    b_hi = b.astype(BF16)
    b_lo = (b - b_hi.astype(F32)).astype(BF16)
    out = jnp.dot(a_hi, b_hi, preferred_element_type=F32)
    out += jnp.dot(a_hi, b_lo, preferred_element_type=F32)
    out += jnp.dot(a_lo, b_hi, preferred_element_type=F32)
    return out


def _router_kernel(x_ref, g_ref, w_ref, b_ref, hn_ref, route_ref, wb_ref):
    hn = _rms(x_ref[...]) * g_ref[...]
    _store_slabs(hn_ref, hn)
    logits = _dot_bf16x3(hn, w_ref[...]) + b_ref[...]
    lane = lax.broadcasted_iota(jnp.int32, logits.shape, 1).astype(F32)
    n_lanes = float(logits.shape[1])

    def first_argmax(vals):
        top = jnp.max(vals, axis=-1, keepdims=True)
        return top, jnp.min(jnp.where(vals == top, lane, n_lanes), axis=-1, keepdims=True)

    is_group = lane < N_GROUPS
    g_top, g_sel = first_argmax(jnp.where(is_group, logits, NEG_BIG))
    g_w = 1.0 / jnp.sum(jnp.where(is_group, jnp.exp(logits - g_top), 0.0), axis=-1, keepdims=True)
    lo = N_GROUPS + EXPERTS_PER_GROUP * g_sel
    e_logits = jnp.where(jnp.logical_and(lane >= lo, lane < lo + EXPERTS_PER_GROUP), logits, NEG_BIG)
    v1, i1 = first_argmax(e_logits)
    v2, i2 = first_argmax(jnp.where(lane == i1, NEG_BIG, e_logits))
    e21 = jnp.exp(v2 - v1)
    w1 = g_w / (1.0 + e21)
    w2 = g_w * e21 / (1.0 + e21)
    route_ref[...] = jnp.where(lane == 0, i1 - N_GROUPS,
                               jnp.where(lane == 1, i2 - N_GROUPS,
                                         jnp.where(lane == 2, w1, jnp.where(lane == 3, w2, 0.0))))
    wb_ref[:, :V7X_LANES] = jnp.broadcast_to(w1, (w1.shape[0], V7X_LANES))
    wb_ref[:, V7X_LANES:] = jnp.broadcast_to(w2, (w2.shape[0], V7X_LANES))


def router(x, g, w_router, b_router):
    S, D = x.shape
    W = w_router.shape[1]
    tm = min(NORM_ROWS, S)
    n_chunks = D // V7X_LANES
    return pl.pallas_call(
        _router_kernel,
        out_shape=(jax.ShapeDtypeStruct((S * n_chunks, V7X_LANES), F32), jax.ShapeDtypeStruct((S, W), F32),
                   jax.ShapeDtypeStruct((S, TOP_K * V7X_LANES), F32)),
        grid=(S // tm,),
        in_specs=[pl.BlockSpec((tm, D), lambda i: (i, 0)),
                  pl.BlockSpec((1, D), lambda i: (0, 0)),
                  pl.BlockSpec((D, W), lambda i: (0, 0)),
                  pl.BlockSpec((1, W), lambda i: (0, 0))],
        out_specs=(pl.BlockSpec((tm * n_chunks, V7X_LANES), lambda i: (i, 0)),
                   pl.BlockSpec((tm, W), lambda i: (i, 0)),
                   pl.BlockSpec((tm, TOP_K * V7X_LANES), lambda i: (i, 0))),
        compiler_params=pltpu.CompilerParams(
            dimension_semantics=("parallel",),
            vmem_limit_bytes=_vmem_limit(2 * tm * D * 8 + 2 * D * W * 4 + 4 * tm * D * 4)),
        name="router",
    )(x, g.reshape(1, D), w_router, b_router)


CAST_ROWS = 128
GATHER_UNROLL = 8
WEIGHT_DMA_PRIORITY = 1


ROW_SLOTS = 3


def _expert_kernel(be_ref, first_ref, next_ref, active_ref, tok_ref, tok1_ref, tok2_ref, hn_hbm, wg_hbm, wu_hbm,
                   wd_hbm, o_ref, xbuf, stage_g, stage_u, stage_d, wg_ref, wu_ref, wd_ref, wsem, gsem,
                   *, layer):
    b = pl.program_id(0)
    n_blocks = pl.num_programs(0)
    n_chunks = wg_ref.shape[0] // V7X_LANES
    R = o_ref.shape[0] // n_chunks
    pitch = _gather_pitch(n_chunks)
    slot = lax.rem(b, ROW_SLOTS)
    slot2 = lax.rem(b + 2, ROW_SLOTS)

    def row_copy(idx_ref, s, r):
        src = pl.multiple_of(idx_ref[0, r], n_chunks)
        return pltpu.make_async_copy(hn_hbm.at[pl.ds(src, n_chunks), :],
                                     xbuf.at[s, pl.ds(r * pitch, n_chunks), :], gsem.at[s])

    def start_rows_loop(idx_ref, s):
        def issue(r, carry):
            row_copy(idx_ref, s, r).start()
            return carry
        lax.fori_loop(0, R, issue, 0, unroll=GATHER_UNROLL)

    def wait_rows(s):
        pltpu.make_async_copy(hn_hbm.at[pl.ds(0, R * n_chunks), :], xbuf.at[s, pl.ds(0, R * n_chunks), :],
                              gsem.at[s]).wait()

    def weight_copies(e):
        return (pltpu.make_async_copy(wg_hbm.at[layer, e], stage_g, wsem.at[0]),
                pltpu.make_async_copy(wu_hbm.at[layer, e], stage_u, wsem.at[1]),
                pltpu.make_async_copy(wd_hbm.at[layer, e], stage_d, wsem.at[2]))

    @pl.when(b == 0)
    def _():
        start_rows_loop(tok_ref, 0)
        start_rows_loop(tok1_ref, 1)
        for cp in weight_copies(be_ref[0]):
            cp.start(priority=WEIGHT_DMA_PRIORITY)

    @pl.when(first_ref[b] == 1)
    def _():
        for cp in weight_copies(be_ref[b]):
            cp.wait()
        for stage, dst in ((stage_g, wg_ref), (stage_u, wu_ref), (stage_d, wd_ref)):
            def cast_rows(r, carry, stage=stage, dst=dst):
                rows = pl.ds(pl.multiple_of(r * CAST_ROWS, CAST_ROWS), CAST_ROWS)
                dst[rows, :] = stage[rows, :].astype(BF16)
                return carry
            lax.fori_loop(0, stage.shape[0] // CAST_ROWS, cast_rows, 0)

        @pl.when(next_ref[b] >= 0)
        def _():
            for cp in weight_copies(next_ref[b]):
                cp.start(priority=WEIGHT_DMA_PRIORITY)

    @pl.when(active_ref[b] == 1)
    def _():
        wait_rows(slot)
        x = jnp.concatenate([_load_slab_chunk(xbuf, (slot,), c, R, pitch).astype(BF16)
                             for c in range(n_chunks)], axis=1)
        for r in range(R):
            row_copy(tok2_ref, slot2, r).start()
        a = jnp.dot(x, wg_ref[...], preferred_element_type=F32)
        u = jnp.dot(x, wu_ref[...], preferred_element_type=F32)
        hmid = (a * jax.nn.sigmoid(a) * u).astype(BF16)
        _store_slabs(o_ref, jnp.dot(hmid, wd_ref[...], preferred_element_type=F32))

    @pl.when(active_ref[b] == 0)
    def _():
        wait_rows(slot)
        start_rows_loop(tok2_ref, slot2)
        o_ref[...] = jnp.zeros_like(o_ref)

    @pl.when(b == n_blocks - 1)
    def _():
        wait_rows(lax.rem(b + 1, ROW_SLOTS))
        wait_rows(slot2)


def expert_blocks(block_e, first, next_e, active, row_tok, hn, w_gate, w_up, w_down, layer):
    n_blocks, _, R = row_tok.shape
    assert n_blocks >= 2
    D, Dh = w_gate.shape[2:]
    n_chunks = D // V7X_LANES
    hbm = pl.BlockSpec(memory_space=pl.ANY)

    def tok_spec(ahead):
        return pl.BlockSpec((None, 1, R), lambda b, *_: (jnp.minimum(b + ahead, n_blocks - 1), 0, 0),
                            memory_space=pltpu.SMEM)

    grid_spec = pltpu.PrefetchScalarGridSpec(
        num_scalar_prefetch=4,
        grid=(n_blocks,),
        in_specs=[tok_spec(0), tok_spec(1), tok_spec(2), hbm, hbm, hbm, hbm],
        out_specs=pl.BlockSpec((R * n_chunks, V7X_LANES), lambda b, *_: (b, 0)),
        scratch_shapes=[pltpu.VMEM((ROW_SLOTS, R * _gather_pitch(n_chunks), V7X_LANES), F32),
                        pltpu.VMEM((D, Dh), F32), pltpu.VMEM((D, Dh), F32), pltpu.VMEM((Dh, D), F32),
                        pltpu.VMEM((D, Dh), BF16), pltpu.VMEM((D, Dh), BF16), pltpu.VMEM((Dh, D), BF16),
                        pltpu.SemaphoreType.DMA((3,)), pltpu.SemaphoreType.DMA((ROW_SLOTS,))],
    )
    return pl.pallas_call(
        functools.partial(_expert_kernel, layer=layer),
        out_shape=jax.ShapeDtypeStruct((n_blocks * R * n_chunks, V7X_LANES), F32),
        grid_spec=grid_spec,
        compiler_params=pltpu.CompilerParams(
            dimension_semantics=("arbitrary",),
            vmem_limit_bytes=_vmem_limit(3 * D * Dh * (4 + 2) + (ROW_SLOTS + 2) * R * D * 4 + R * D * 2
                                         + 6 * R * Dh * 4)),
        name="moe_experts",
    )(block_e, first, next_e, active, row_tok, row_tok, row_tok, hn, w_gate, w_up, w_down)


COMBINE_ROWS = 256


def _combine_kernel(idx_ref, idx_next_ref, x_ref, wb_ref, yb_hbm, o_ref, cbuf, gsem):
    i = pl.program_id(0)
    n_tiles = pl.num_programs(0)
    tm, D = x_ref.shape
    n_chunks = D // V7X_LANES
    pitch = _gather_pitch(n_chunks)
    slot = lax.rem(i, 2)

    def row_copy(idx, s, r, k):
        src = pl.multiple_of(idx[0, TOP_K * r + k], n_chunks)
        return pltpu.make_async_copy(yb_hbm.at[pl.ds(src, n_chunks), :],
                                     cbuf.at[s, k, pl.ds(r * pitch, n_chunks), :], gsem.at[s])

    def wait_rows(s):
        for k in range(TOP_K):
            pltpu.make_async_copy(yb_hbm.at[pl.ds(0, tm * n_chunks), :],
                                  cbuf.at[s, k, pl.ds(0, tm * n_chunks), :], gsem.at[s]).wait()

    @pl.when(i == 0)
    def _():
        def issue(r, carry):
            for k in range(TOP_K):
                row_copy(idx_ref, 0, r, k).start()
            return carry
        lax.fori_loop(0, tm, issue, 0, unroll=GATHER_UNROLL)

    wait_rows(slot)
    for r in range(tm):
        for k in range(TOP_K):
            row_copy(idx_next_ref, 1 - slot, r, k).start(priority=k % 2)
    for c in range(n_chunks):
        cols = slice(c * V7X_LANES, (c + 1) * V7X_LANES)
        acc = x_ref[:, cols]
        for k in range(TOP_K):
            acc = acc + wb_ref[:, k * V7X_LANES:(k + 1) * V7X_LANES] * _load_slab_chunk(cbuf, (slot, k), c, tm, pitch)
        o_ref[:, cols] = acc

    @pl.when(i == n_tiles - 1)
    def _():
        wait_rows(1 - slot)


def combine_residual(x, wb, dest, yb):
    T, D = x.shape
    tm = min(COMBINE_ROWS, T)
    n_tiles = T // tm
    idx = (dest * (D // V7X_LANES)).reshape(n_tiles, 1, tm * TOP_K)
    return pl.pallas_call(
        _combine_kernel,
        out_shape=jax.ShapeDtypeStruct((T, D), F32),
        grid=(n_tiles,),
        in_specs=[pl.BlockSpec((None, 1, tm * TOP_K), lambda i: (i, 0, 0), memory_space=pltpu.SMEM),
                  pl.BlockSpec((None, 1, tm * TOP_K), lambda i: (jnp.minimum(i + 1, n_tiles - 1), 0, 0),
                               memory_space=pltpu.SMEM),
                  pl.BlockSpec((tm, D), lambda i: (i, 0)),
                  pl.BlockSpec((tm, wb.shape[1]), lambda i: (i, 0)),
                  pl.BlockSpec(memory_space=pl.ANY)],
        out_specs=pl.BlockSpec((tm, D), lambda i: (i, 0)),
        scratch_shapes=[pltpu.VMEM((2, TOP_K, tm * _gather_pitch(D // V7X_LANES), V7X_LANES), F32),
                        pltpu.SemaphoreType.DMA((2,))],
        compiler_params=pltpu.CompilerParams(
            dimension_semantics=("arbitrary",),
            vmem_limit_bytes=_vmem_limit(2 * TOP_K * tm * D * 4 + 6 * tm * D * 4)),
        name="moe_combine",
    )(idx, idx, x, wb, yb)


def moe_residual(x, g, rgw, rgb, rew, reb, w_gate, w_up, w_down, layer):
    T, D = x.shape
    R = MOE_ROWS
    assert TOP_K == 2
    pad = V7X_LANES - N_GROUPS - N_EXPERTS
    w_router = jnp.concatenate([rgw, rew, jnp.zeros((D, pad), F32)], axis=1)
    b_router = jnp.concatenate([rgb, reb, jnp.zeros((pad,), F32)]).reshape(1, V7X_LANES)
    hn, route, wb = router(x, g, w_router, b_router)
    expert_id = route[:, :TOP_K].astype(jnp.int32)

    n_assign = T * TOP_K
    flat_e = expert_id.reshape(-1)
    onehot = (flat_e[:, None] == jnp.arange(N_EXPERTS, dtype=jnp.int32)[None, :]).astype(jnp.int32)
    running = jnp.cumsum(onehot, axis=0)
    counts = running[-1]
    rank = jnp.take_along_axis(running, flat_e[:, None], axis=1)[:, 0] - 1
    padded = (counts + R - 1) // R * R
    pends = jnp.cumsum(padded)
    pstarts = pends - padded
    dest = pstarts[flat_e] + rank
    n_blocks = (n_assign + N_EXPERTS * (R - 1) + R - 1) // R
    flat_tok = jnp.repeat(jnp.arange(T, dtype=jnp.int32), TOP_K)
    row_tok = jnp.zeros((n_blocks * R,), jnp.int32).at[dest].set(flat_tok)

    blk_start = jnp.arange(n_blocks, dtype=jnp.int32) * R
    active = blk_start < pends[-1]
    block_e = jnp.minimum(jnp.sum(blk_start[:, None] >= pends[None, :], axis=1), N_EXPERTS - 1).astype(jnp.int32)
    prev_e = jnp.concatenate([jnp.full((1,), -1, jnp.int32), block_e[:-1]])
    first = jnp.logical_and(active, block_e != prev_e)
    later = lax.cummin(jnp.where(first, block_e, N_EXPERTS)[::-1])[::-1]
    next_e = jnp.concatenate([later[1:], jnp.full((1,), N_EXPERTS, jnp.int32)])
    next_e = jnp.where(next_e >= N_EXPERTS, -1, next_e).astype(jnp.int32)

    yb = expert_blocks(block_e, first.astype(jnp.int32), next_e, active.astype(jnp.int32),
                       (row_tok * (D // V7X_LANES)).reshape(n_blocks, 1, R), hn, w_gate, w_up, w_down, layer)
    return combine_residual(x, wb, dest.reshape(T, TOP_K), yb)


def fox_mlstm_residual(xt, norm_g, w, fox_f_bias, fox_q_gain, fox_k_gain, mlstm_i_bias, mlstm_f_bias,
                       mlstm_out_gain, w_out):
    S, D = xt.shape
    H = fox_f_bias.shape[0]
    assert mlstm_i_bias.shape[0] == H and 3 * H <= 32
    fw = H * HEAD_DIM
    o_ff = 3 * fw
    o_mq = o_ff + H
    o_mi = o_mq + 3 * fw
    o_mo = o_mi + 2 * H
    w_main = jnp.concatenate([w[:, :o_ff], w[:, o_mq:o_mi]], axis=1).astype(BF16)
    gate_pad = V7X_LANES - 3 * H
    w_aux = jnp.concatenate([w[:, o_mo:], w[:, o_ff:o_mq], w[:, o_mi:o_mo],
                             jnp.zeros((D, gate_pad), F32)], axis=1).astype(BF16)
    gains = jnp.concatenate([fox_q_gain[None] * (LOG2E * HEAD_DIM ** -0.5), fox_k_gain[None],
                             jnp.zeros((6, HEAD_DIM), F32)], axis=0)
    bias_row = jnp.concatenate([fox_f_bias, mlstm_i_bias, mlstm_f_bias,
                                jnp.zeros((gate_pad,), F32)]).reshape(1, V7X_LANES)

    hn = rmsnorm_bf16(xt, norm_g)
    proj = inproj_main(hn, w_main, gains)
    mo, gates_pre = inproj_aux(hn, w_aux, fw)
    gcol = gate_activations(gates_pre, bias_row, H)
    grow = gcol[:, :32].T
    cq = jnp.broadcast_to(grow[:H, :, None], (H, S, V7X_LANES))
    ck = grow[:H].reshape(H, 1, S)
    y_fox = fox_attention(proj, cq, ck, H, gains[0], gains[1])
    y_mlstm = mlstm_mixer(proj, mo, gcol, grow, mlstm_out_gain, H, 3)
    return outproj_residual(y_fox, y_mlstm, w_out.astype(BF16), xt)


def kernel(x, norm_mix, norm_ffn, w_in, fox_f_bias, fox_q_gain, fox_k_gain, mlstm_i_bias, mlstm_f_bias,
           mlstm_out_gain, w_out, pool_w, pool_b, pool_scale, router_group_w, router_group_b,
           router_expert_w, router_expert_b, w_gate, w_up, w_down):
    B, S, D = x.shape
    assert B == 1
    depth = norm_mix.shape[0]
    xt = x.reshape(S, D)

    for layer in range(depth):
        j = layer // 2
        if layer % 2 == 0:
            xt = fox_mlstm_residual(xt, norm_mix[layer], w_in[j], fox_f_bias[j], fox_q_gain[j],
                                    fox_k_gain[j], mlstm_i_bias[j], mlstm_f_bias[j], mlstm_out_gain[j],
                                    w_out[j])
        else:
            xt = pool_mixer_residual(xt, norm_mix[layer], pool_w[j].astype(BF16), pool_b[j], pool_scale[j])
        xt = moe_residual(xt, norm_ffn[layer], router_group_w[layer], router_group_b[layer],
                          router_expert_w[layer], router_expert_b[layer],
                          w_gate, w_up, w_down, layer)
    return xt.reshape(B, S, D)
```

```python
import functools

import jax
import jax.numpy as jnp
from jax import lax
from jax.experimental import pallas as pl
from jax.experimental.pallas import tpu as pltpu

F32 = jnp.float32
BF16 = jnp.bfloat16

HEAD_DIM = 128
GATE_SOFTCAP = 15.0
POOL_WINDOWS = (2, 4, 8, 16)
POOL_HALO = 16
N_GROUPS = 4
EXPERTS_PER_GROUP = 8
N_EXPERTS = N_GROUPS * EXPERTS_PER_GROUP
TOP_K = 2
RMS_EPS = 1e-6

V7X_LANES = 128
V7X_VMEM_BYTES = 64 * 1024 * 1024

NORM_ROWS = 512
MM_ROWS = 1024
MM_COLS = 1024
ATT_Q = 1024
ATT_K = 512
MLSTM_CHUNK = 256
MOE_ROWS = 256
NEG_BIG = -1e30
LOG2E = 1.4426950408889634


def _vmem_limit(nbytes):
    return int(min(max(nbytes * 3 // 2, 16 * 1024 * 1024), V7X_VMEM_BYTES - 8 * 1024 * 1024))


def _rms(x, eps=RMS_EPS):
    return x * lax.rsqrt(jnp.mean(x * x, axis=-1, keepdims=True) + eps)


def _log_sigmoid(x):
    return -(jnp.maximum(-x, 0.0) + jnp.log1p(jnp.exp(-jnp.abs(x))))


def _inproj_main_kernel(a_ref, w_ref, gain_ref, o_ref, *, n_heads_per_tile):
    j = pl.program_id(1)
    acc = jnp.dot(a_ref[...], w_ref[...], preferred_element_type=F32)

    @pl.when(j < 2)
    def _():
        g = gain_ref[pl.ds(j, 1), :]
        for h in range(n_heads_per_tile):
            a = acc[:, h * HEAD_DIM:(h + 1) * HEAD_DIM]
            o_ref[:, h * HEAD_DIM:(h + 1) * HEAD_DIM] = (_rms(a) * g).astype(o_ref.dtype)

    @pl.when(j == 4)
    def _():
        o_ref[...] = (acc * (HEAD_DIM ** -0.5)).astype(o_ref.dtype)

    @pl.when(jnp.logical_and(j >= 2, j != 4))
    def _():
        o_ref[...] = acc.astype(o_ref.dtype)


def inproj_main(hn, w_main, gains):
    S, D = hn.shape
    N = w_main.shape[1]
    tm, tn = min(MM_ROWS, S), MM_COLS
    return pl.pallas_call(
        functools.partial(_inproj_main_kernel, n_heads_per_tile=tn // HEAD_DIM),
        out_shape=jax.ShapeDtypeStruct((S, N), BF16),
        grid=(S // tm, N // tn),
        in_specs=[pl.BlockSpec((tm, D), lambda i, j: (i, 0)),
                  pl.BlockSpec((D, tn), lambda i, j: (0, j)),
                  pl.BlockSpec((8, HEAD_DIM), lambda i, j: (0, 0))],
        out_specs=pl.BlockSpec((tm, tn), lambda i, j: (i, j)),
        compiler_params=pltpu.CompilerParams(
            dimension_semantics=("parallel", "parallel"),
            vmem_limit_bytes=_vmem_limit(2 * (tm * D * 2 + D * tn * 2 + tm * tn * 2) + 2 * tm * tn * 4)),
        name="inproj_main",
    )(hn, w_main, gains)


def _norm_inproj_aux_kernel(x_ref, g_ref, w_ref, hn_ref, mo_ref, gate_ref):
    hn = (_rms(x_ref[...]) * g_ref[...]).astype(BF16)
    hn_ref[...] = hn
    acc = jnp.dot(hn, w_ref[...], preferred_element_type=F32)
    n_mo = mo_ref.shape[1]
    mo_ref[...] = acc[:, :n_mo]
    gate_ref[...] = acc[:, n_mo:]


def norm_inproj_aux(x, g, w_aux, n_mo):
    S, D = x.shape
    N = w_aux.shape[1]
    tm = min(NORM_ROWS, S)
    return pl.pallas_call(
        _norm_inproj_aux_kernel,
        out_shape=(jax.ShapeDtypeStruct((S, D), BF16),
                   jax.ShapeDtypeStruct((S, n_mo), F32),
                   jax.ShapeDtypeStruct((S, N - n_mo), F32)),
        grid=(S // tm,),
        in_specs=[pl.BlockSpec((tm, D), lambda i: (i, 0)),
                  pl.BlockSpec((1, D), lambda i: (0, 0)),
                  pl.BlockSpec((D, N), lambda i: (0, 0))],
        out_specs=(pl.BlockSpec((tm, D), lambda i: (i, 0)),
                   pl.BlockSpec((tm, n_mo), lambda i: (i, 0)),
                   pl.BlockSpec((tm, N - n_mo), lambda i: (i, 0))),
        compiler_params=pltpu.CompilerParams(
            dimension_semantics=("parallel",),
            vmem_limit_bytes=_vmem_limit(2 * (tm * D * 6 + D * N * 2 + tm * N * 4) + tm * N * 4 + 2 * tm * D * 4)),
        name="norm_inproj_aux",
    )(x, g.reshape(1, D), w_aux)


def _split3_dot(tri, val):
    v1 = val.astype(BF16)
    r1 = val - v1.astype(F32)
    v2 = r1.astype(BF16)
    v3 = (r1 - v2.astype(F32)).astype(BF16)
    out = jnp.dot(tri, v1, preferred_element_type=F32)
    out += jnp.dot(tri, v2, preferred_element_type=F32)
    out += jnp.dot(tri, v3, preferred_element_type=F32)
    return out


def _gates_kernel(g_ref, bias_ref, o_ref, carry_ref, *, n_heads):
    @pl.when(pl.program_id(0) == 0)
    def _():
        carry_ref[...] = jnp.zeros_like(carry_ref)

    rows = g_ref.shape[0]
    z = g_ref[...] + bias_ref[...]
    lane = lax.broadcasted_iota(jnp.int32, z.shape, 1)
    capped = GATE_SOFTCAP * jnp.tanh(z / GATE_SOFTCAP)
    is_fox = lane < n_heads
    is_i = jnp.logical_and(lane >= n_heads, lane < 2 * n_heads)
    is_f = jnp.logical_and(lane >= 2 * n_heads, lane < 3 * n_heads)
    logf = jnp.where(is_fox, _log_sigmoid(z), jnp.where(is_f, _log_sigmoid(capped), 0.0))
    r = lax.broadcasted_iota(jnp.int32, (rows, rows), 0)
    c = lax.broadcasted_iota(jnp.int32, (rows, rows), 1)
    tri = jnp.where(r >= c, 1.0, 0.0).astype(BF16)
    cum = _split3_dot(tri, logf)
    glob = cum + carry_ref[...]
    o_ref[...] = jnp.where(is_fox, glob * LOG2E, jnp.where(is_i, capped, cum))
    carry_ref[...] = glob[rows - 1:rows, :]


def gate_activations(gates_pre, bias_row, n_heads):
    S, W = gates_pre.shape
    tb = MLSTM_CHUNK
    return pl.pallas_call(
        functools.partial(_gates_kernel, n_heads=n_heads),
        out_shape=jax.ShapeDtypeStruct((S, W), F32),
        grid=(S // tb,),
        in_specs=[pl.BlockSpec((tb, W), lambda i: (i, 0)),
                  pl.BlockSpec((1, W), lambda i: (0, 0))],
        out_specs=pl.BlockSpec((tb, W), lambda i: (i, 0)),
        scratch_shapes=[pltpu.VMEM((1, W), F32)],
        compiler_params=pltpu.CompilerParams(dimension_semantics=("arbitrary",)),
        name="gate_activations",
    )(gates_pre, bias_row)


def _fox_kernel(first_ref, q_ref, k_ref, v_ref, cq_ref, ck_ref, o_ref, m_ref, acc_ref, s_ref, p_ref, alpha_ref,
                *, n_sub):
    i = pl.program_id(1)
    d = HEAD_DIM
    tk = q_ref.shape[0] // n_sub
    assert n_sub % 2 == 0

    m_ref[...] = jnp.full_like(m_ref, NEG_BIG)
    acc_ref[...] = jnp.zeros_like(acc_ref)
    p_ref[1] = jnp.zeros_like(p_ref[1])
    alpha_ref[1] = jnp.ones_like(alpha_ref[1])
    lane = lax.broadcasted_iota(jnp.int32, (tk, d), 1)
    ones_col = jnp.where(lane == 0, 1.0, 0.0).astype(BF16)
    row = lax.broadcasted_iota(jnp.int32, (tk, tk), 0)
    col = lax.broadcasted_iota(jnp.int32, (tk, tk), 1)
    causal = col <= row

    def qk_stage(sub, j, par):
        start = pl.multiple_of(j * tk, tk)
        s = lax.dot_general(q_ref[pl.ds(sub * tk, tk), :], k_ref[pl.ds(start, tk), :],
                            (((1,), (1,)), ((), ())), preferred_element_type=F32)
        s_ref[par, sub] = s - ck_ref[:, pl.ds(start, tk)]

    def sm_stage(sub, par, masked):
        rows = pl.ds(sub * tk, tk)
        s = s_ref[par, sub]
        if masked:
            s = jnp.where(causal, s, NEG_BIG)
        cq = cq_ref[rows, :]
        m_prev = m_ref[rows, :]
        m_new = jnp.maximum(m_prev, jnp.max(s, axis=-1, keepdims=True) + cq)
        p_ref[par, sub] = jnp.exp2(s - jnp.tile(m_new - cq, (1, tk // V7X_LANES))).astype(BF16)
        alpha_ref[par, rows, :] = jnp.exp2(m_prev - m_new)
        m_ref[rows, :] = m_new

    def pv_stage(sub, j, par):
        rows = pl.ds(sub * tk, tk)
        start = pl.multiple_of(j * tk, tk)
        v_aug = jnp.concatenate([v_ref[pl.ds(start, tk), :], ones_col], axis=1)
        acc_ref[rows, :] = (jnp.tile(alpha_ref[par, rows, :], (1, 2)) * acc_ref[rows, :]
                            + jnp.dot(p_ref[par, sub], v_aug, preferred_element_type=F32))

    n_full = i * n_sub
    first = first_ref[pl.program_id(0) * pl.num_programs(1) + i]
    for sub in range(n_sub):
        qk_stage(sub, first, 0)

    def body(tt, carry):
        for par in (0, 1):
            step = 2 * tt + par
            for sub in range(n_sub):
                qk_stage(sub, step + 1, 1 - par)
                sm_stage(sub, par, masked=False)
                pv_stage(sub, jnp.maximum(step - 1, 0), 1 - par)
        return carry

    lax.fori_loop(first // 2, n_full // 2, body, 0)
    for kk in range(n_sub + 1):
        par = kk % 2
        for sub in range(n_sub):
            if kk + 1 <= sub:
                qk_stage(sub, n_full + kk + 1, 1 - par)
            if kk <= sub:
                sm_stage(sub, par, masked=(kk == sub))
            if kk - 1 <= sub:
                pv_stage(sub, jnp.maximum(n_full + kk - 1, 0), 1 - par)
    acc = acc_ref[...]
    o_ref[...] = (acc[:, :d] / acc[:, d:d + 1]).astype(o_ref.dtype)


UNDERFLOW_LOG2 = 152.0


def _fox_first_blocks(q_gain, k_gain, c2, tk, n_sub):
    H, S = c2.shape
    nb = S // tk
    slack = 1.0 + 2.0 ** -6
    qk_bound = HEAD_DIM * jnp.max(jnp.abs(q_gain)) * jnp.max(jnp.abs(k_gain)) * slack
    cb = c2.reshape(H, nb, tk)
    c_hi, c_lo = jnp.max(cb, axis=-1), jnp.min(cb, axis=-1)
    upper = qk_bound + c_hi[:, :, None] - c_lo[:, None, :]
    lower = -qk_bound
    j_idx = jnp.arange(nb, dtype=jnp.int32)
    needed = jnp.logical_or(upper >= lower - UNDERFLOW_LOG2, j_idx[None, None, :] >= j_idx[None, :, None])
    first = jnp.min(jnp.where(needed, j_idx[None, None, :], nb), axis=-1)
    first = jnp.min(first.reshape(H, nb // n_sub, n_sub), axis=-1)
    return ((first // 2) * 2).reshape(-1).astype(jnp.int32)


def fox_attention(proj, cq, ck, n_heads, q_gain, k_gain):
    S = proj.shape[0]
    tk = min(ATT_K, S)
    n_sub = max(1, min(ATT_Q, S) // tk)
    tq = n_sub * tk
    H = n_heads
    first = _fox_first_blocks(q_gain, k_gain, ck.reshape(H, S), tk, n_sub)
    grid_spec = pltpu.PrefetchScalarGridSpec(
        num_scalar_prefetch=1,
        grid=(H, S // tq),
        in_specs=[pl.BlockSpec((tq, HEAD_DIM), lambda h, i, *_: (i, h)),
                  pl.BlockSpec((S, HEAD_DIM), lambda h, i, *_: (0, H + h)),
                  pl.BlockSpec((S, HEAD_DIM), lambda h, i, *_: (0, 2 * H + h)),
                  pl.BlockSpec((None, tq, V7X_LANES), lambda h, i, *_: (h, i, 0)),
                  pl.BlockSpec((None, 1, S), lambda h, i, *_: (h, 0, 0))],
        out_specs=pl.BlockSpec((tq, HEAD_DIM), lambda h, i, *_: (i, h)),
        scratch_shapes=[pltpu.VMEM((tq, V7X_LANES), F32), pltpu.VMEM((tq, 2 * HEAD_DIM), F32),
                        pltpu.VMEM((2, n_sub, tk, tk), F32), pltpu.VMEM((2, n_sub, tk, tk), BF16),
                        pltpu.VMEM((2, tq, V7X_LANES), F32)],
    )
    return pl.pallas_call(
        functools.partial(_fox_kernel, n_sub=n_sub),
        out_shape=jax.ShapeDtypeStruct((S, H * HEAD_DIM), BF16),
        grid_spec=grid_spec,
        compiler_params=pltpu.CompilerParams(
            dimension_semantics=("parallel", "arbitrary"),
            vmem_limit_bytes=_vmem_limit(4 * S * HEAD_DIM * 2 + 2 * n_sub * tk * tk * (4 + 2)
                                         + 4 * tk * tk * 4 + 12 * tq * HEAD_DIM * 4 + 16 * S * 4)),
        name="fox_attention",
    )(first, proj, proj, proj, cq, ck)


def _mlstm_kernel(q_ref, k_ref, v_ref, mo_ref, gcol_ref, grow_ref, gain_ref, o_ref,
                  state_ref, m_ref, *, n_heads):
    L = q_ref.shape[0]
    d = HEAD_DIM

    @pl.when(pl.program_id(0) == 0)
    def _():
        state_ref[...] = jnp.zeros_like(state_ref)
        m_ref[...] = jnp.zeros_like(m_ref)

    row = lax.broadcasted_iota(jnp.int32, (L, L), 0)
    col = lax.broadcasted_iota(jnp.int32, (L, L), 1)
    causal = col <= row
    lane = lax.broadcasted_iota(jnp.int32, (L, d), 1)
    ones_col = jnp.where(lane == 0, 1.0, 0.0).astype(BF16)

    gcol = gcol_ref[...]
    grow = grow_ref[...]
    for h in range(n_heads):
        sl = slice(h * d, (h + 1) * d)
        q = q_ref[:, sl]
        k = k_ref[:, sl]
        v = v_ref[:, sl]
        i_col = gcol[:, n_heads + h:n_heads + h + 1]
        b_col = gcol[:, 2 * n_heads + h:2 * n_heads + h + 1]
        i_row = grow[n_heads + h:n_heads + h + 1, :]
        b_row = grow[2 * n_heads + h:2 * n_heads + h + 1, :]
        m_prev = m_ref[h:h + 1, 0:1]
        state = state_ref[h]

        log_intra = jnp.where(causal, b_col - b_row + i_row, NEG_BIG)
        log_inter = b_col + m_prev
        m_t = jnp.maximum(log_inter, jnp.max(log_intra, axis=-1, keepdims=True))
        w_intra = jnp.exp(log_intra - m_t)
        w_inter = jnp.exp(log_inter - m_t)
        qk = lax.dot_general(q, k, (((1,), (1,)), ((), ())), preferred_element_type=F32) * w_intra
        v_aug = jnp.concatenate([v, ones_col], axis=1)
        tot = jnp.dot(qk.astype(BF16), v_aug, preferred_element_type=F32)
        tot = tot + w_inter * jnp.dot(q, state.astype(BF16), preferred_element_type=F32)
        num = tot[:, :d]
        den = tot[:, d:d + 1]
        hval = num / jnp.maximum(jnp.abs(den), jnp.exp(-m_t))

        b_last = b_col[L - 1:L, :]
        log_w_state = b_last - b_col + i_col
        m_new = jnp.maximum(b_last + m_prev, jnp.max(log_w_state, axis=0, keepdims=True))
        decay = jnp.exp(b_last + m_prev - m_new)
        w_s = jnp.exp(log_w_state - m_new)
        wv = (w_s * v_aug.astype(F32)).astype(BF16)
        upd = lax.dot_general(k, wv, (((0,), (0,)), ((), ())), preferred_element_type=F32)
        state_ref[h] = decay * state + upd
        m_ref[h:h + 1, :] = jnp.broadcast_to(m_new, (1, m_ref.shape[1]))

        hn = _rms(hval) * gain_ref[:, sl]
        o_ref[:, sl] = (jax.nn.sigmoid(mo_ref[:, sl]) * hn).astype(o_ref.dtype)


def mlstm_mixer(proj, mo, gcol, grow, out_gain, n_heads, q_block):
    S = proj.shape[0]
    L = min(MLSTM_CHUNK, S)
    W = n_heads * HEAD_DIM
    return pl.pallas_call(
        functools.partial(_mlstm_kernel, n_heads=n_heads),
        out_shape=jax.ShapeDtypeStruct((S, W), BF16),
        grid=(S // L,),
        in_specs=[pl.BlockSpec((L, W), lambda c: (c, q_block)),
                  pl.BlockSpec((L, W), lambda c: (c, q_block + 1)),
                  pl.BlockSpec((L, W), lambda c: (c, q_block + 2)),
                  pl.BlockSpec((L, W), lambda c: (c, 0)),
                  pl.BlockSpec((L, gcol.shape[1]), lambda c: (c, 0)),
                  pl.BlockSpec((grow.shape[0], L), lambda c: (0, c)),
                  pl.BlockSpec((1, W), lambda c: (0, 0))],
        out_specs=pl.BlockSpec((L, W), lambda c: (c, 0)),
        scratch_shapes=[pltpu.VMEM((n_heads, HEAD_DIM, 2 * HEAD_DIM), F32),
                        pltpu.VMEM((n_heads, V7X_LANES), F32)],
        compiler_params=pltpu.CompilerParams(
            dimension_semantics=("arbitrary",),
            vmem_limit_bytes=_vmem_limit(2 * L * W * (3 * 2 + 4 + 2) + 16 * L * L * 4)),
        name="mlstm",
    )(proj, proj, proj, mo, gcol, grow, out_gain.reshape(1, W))


def _outproj_kernel(a1_ref, a2_ref, w_ref, x_ref, o_ref):
    k1 = a1_ref.shape[1]
    acc = jnp.dot(a1_ref[...], w_ref[:k1, :], preferred_element_type=F32)
    acc += jnp.dot(a2_ref[...], w_ref[k1:, :], preferred_element_type=F32)
    o_ref[...] = x_ref[...] + acc


def outproj_residual(a1, a2, w, x):
    S, K1 = a1.shape
    K2 = a2.shape[1]
    N = w.shape[1]
    tm, tn = min(MM_ROWS, S), MM_COLS
    return pl.pallas_call(
        _outproj_kernel,
        out_shape=jax.ShapeDtypeStruct((S, N), F32),
        grid=(S // tm, N // tn),
        in_specs=[pl.BlockSpec((tm, K1), lambda i, j: (i, 0)),
                  pl.BlockSpec((tm, K2), lambda i, j: (i, 0)),
                  pl.BlockSpec((K1 + K2, tn), lambda i, j: (0, j)),
                  pl.BlockSpec((tm, tn), lambda i, j: (i, j))],
        out_specs=pl.BlockSpec((tm, tn), lambda i, j: (i, j)),
        compiler_params=pltpu.CompilerParams(
            dimension_semantics=("parallel", "parallel"),
            vmem_limit_bytes=_vmem_limit(2 * (tm * (K1 + K2) * 2 + (K1 + K2) * tn * 2 + 2 * tm * tn * 4)
                                         + tm * tn * 4)),
        name="outproj",
    )(a1, a2, w, x)


def _pool_kernel(x_ref, g_ref, w_ref, b_ref, scale_ref, o_ref, carry_ref):
    i = pl.program_id(0)
    tm = x_ref.shape[0]
    gw = w_ref.shape[1]

    @pl.when(i == 0)
    def _():
        carry_ref[...] = jnp.zeros_like(carry_ref)

    x = x_ref[...]
    hn = _rms(x) * g_ref[...]
    t = i * tm + lax.broadcasted_iota(jnp.int32, (tm, 1), 0)
    for g, w in enumerate(POOL_WINDOWS):
        sl = slice(g * gw, (g + 1) * gw)
        hg = hn[:, sl]
        cur = jnp.concatenate([carry_ref[:, sl], hg], axis=0)
        k = 1
        while k < w:
            cur = cur + pltpu.roll(cur, k, axis=0)
            k *= 2
        window_sum = cur[POOL_HALO:, :]
        count = jnp.minimum(t + 1, w).astype(F32)
        pooled = window_sum / count - hg
        y = jnp.dot(pooled.astype(BF16), w_ref[g], preferred_element_type=F32) + b_ref[:, sl]
        o_ref[:, sl] = x[:, sl] + y * scale_ref[:, sl]
    carry_ref[...] = hn[tm - POOL_HALO:, :]


def pool_mixer_residual(x, g, pool_w, pool_b, pool_scale):
    S, D = x.shape
    tm = min(NORM_ROWS, S)
    G, gw, _ = pool_w.shape
    return pl.pallas_call(
        _pool_kernel,
        out_shape=jax.ShapeDtypeStruct((S, D), F32),
        grid=(S // tm,),
        in_specs=[pl.BlockSpec((tm, D), lambda i: (i, 0)),
                  pl.BlockSpec((1, D), lambda i: (0, 0)),
                  pl.BlockSpec((G, gw, gw), lambda i: (0, 0, 0)),
                  pl.BlockSpec((1, D), lambda i: (0, 0)),
                  pl.BlockSpec((1, D), lambda i: (0, 0))],
        out_specs=pl.BlockSpec((tm, D), lambda i: (i, 0)),
        scratch_shapes=[pltpu.VMEM((POOL_HALO, D), F32)],
        compiler_params=pltpu.CompilerParams(
            dimension_semantics=("arbitrary",),
            vmem_limit_bytes=_vmem_limit(4 * tm * D * 4 + 2 * G * gw * gw * 2 + 6 * tm * D * 4)),
        name="pool_mixer",
    )(x, g.reshape(1, D), pool_w, pool_b.reshape(1, D), pool_scale.reshape(1, D))


def _store_slabs(ref, val):
    n = val.shape[0]
    n_chunks = val.shape[1] // V7X_LANES
    for c in range(n_chunks):
        ref[pl.ds(c, n, stride=n_chunks), :] = val[:, c * V7X_LANES:(c + 1) * V7X_LANES]


def _load_slab_chunk(ref, lead, c, n, pitch):
    return ref[lead + (pl.ds(c, n, stride=pitch), slice(None))]


def _gather_pitch(n_chunks):
    return n_chunks + 4 if n_chunks % 8 == 0 else n_chunks


def _dot_bf16x3(a, b):
    a_hi = a.astype(BF16)
    a_lo = (a - a_hi.astype(F32)).astype(BF16)
    b_hi = b.astype(BF16)
    b_lo = (b - b_hi.astype(F32)).astype(BF16)
    out = jnp.dot(a_hi, b_hi, preferred_element_type=F32)
    out += jnp.dot(a_hi, b_lo, preferred_element_type=F32)
    out += jnp.dot(a_lo, b_hi, preferred_element_type=F32)
    return out


def _router_kernel(x_ref, g_ref, w_ref, b_ref, hn_ref, route_ref, wb_ref):
    hn = _rms(x_ref[...]) * g_ref[...]
    _store_slabs(hn_ref, hn)
    logits = _dot_bf16x3(hn, w_ref[...]) + b_ref[...]
    lane = lax.broadcasted_iota(jnp.int32, logits.shape, 1).astype(F32)
    n_lanes = float(logits.shape[1])

    def first_argmax(vals):
        top = jnp.max(vals, axis=-1, keepdims=True)
        return top, jnp.min(jnp.where(vals == top, lane, n_lanes), axis=-1, keepdims=True)

    is_group = lane < N_GROUPS
    g_top, g_sel = first_argmax(jnp.where(is_group, logits, NEG_BIG))
    g_w = 1.0 / jnp.sum(jnp.where(is_group, jnp.exp(logits - g_top), 0.0), axis=-1, keepdims=True)
    lo = N_GROUPS + EXPERTS_PER_GROUP * g_sel
    e_logits = jnp.where(jnp.logical_and(lane >= lo, lane < lo + EXPERTS_PER_GROUP), logits, NEG_BIG)
    v1, i1 = first_argmax(e_logits)
    v2, i2 = first_argmax(jnp.where(lane == i1, NEG_BIG, e_logits))
    e21 = jnp.exp(v2 - v1)
    w1 = g_w / (1.0 + e21)
    w2 = g_w * e21 / (1.0 + e21)
    route_ref[...] = jnp.where(lane == 0, i1 - N_GROUPS,
                               jnp.where(lane == 1, i2 - N_GROUPS,
                                         jnp.where(lane == 2, w1, jnp.where(lane == 3, w2, 0.0))))
    wb_ref[:, :V7X_LANES] = jnp.broadcast_to(w1, (w1.shape[0], V7X_LANES))
    wb_ref[:, V7X_LANES:] = jnp.broadcast_to(w2, (w2.shape[0], V7X_LANES))


def router(x, g, w_router, b_router):
    S, D = x.shape
    W = w_router.shape[1]
    tm = min(NORM_ROWS, S)
    n_chunks = D // V7X_LANES
    return pl.pallas_call(
        _router_kernel,
        out_shape=(jax.ShapeDtypeStruct((S * n_chunks, V7X_LANES), F32), jax.ShapeDtypeStruct((S, W), F32),
                   jax.ShapeDtypeStruct((S, TOP_K * V7X_LANES), F32)),
        grid=(S // tm,),
        in_specs=[pl.BlockSpec((tm, D), lambda i: (i, 0)),
                  pl.BlockSpec((1, D), lambda i: (0, 0)),
                  pl.BlockSpec((D, W), lambda i: (0, 0)),
                  pl.BlockSpec((1, W), lambda i: (0, 0))],
        out_specs=(pl.BlockSpec((tm * n_chunks, V7X_LANES), lambda i: (i, 0)),
                   pl.BlockSpec((tm, W), lambda i: (i, 0)),
                   pl.BlockSpec((tm, TOP_K * V7X_LANES), lambda i: (i, 0))),
        compiler_params=pltpu.CompilerParams(
            dimension_semantics=("parallel",),
            vmem_limit_bytes=_vmem_limit(2 * tm * D * 8 + 2 * D * W * 4 + 4 * tm * D * 4)),
        name="router",
    )(x, g.reshape(1, D), w_router, b_router)


CAST_ROWS = 128
GATHER_UNROLL = 8
WEIGHT_DMA_PRIORITY = 1


ROW_SLOTS = 3


def _expert_kernel(be_ref, first_ref, next_ref, active_ref, tok_ref, tok1_ref, tok2_ref, hn_hbm, wg_hbm, wu_hbm,
                   wd_hbm, o_ref, xbuf, stage_g, stage_u, stage_d, wg_ref, wu_ref, wd_ref, wsem, gsem,
                   *, layer):
    b = pl.program_id(0)
    n_blocks = pl.num_programs(0)
    n_chunks = wg_ref.shape[0] // V7X_LANES
    R = o_ref.shape[0] // n_chunks
    pitch = _gather_pitch(n_chunks)
    slot = lax.rem(b, ROW_SLOTS)
    slot2 = lax.rem(b + 2, ROW_SLOTS)

    def row_copy(idx_ref, s, r):
        src = pl.multiple_of(idx_ref[0, r], n_chunks)
        return pltpu.make_async_copy(hn_hbm.at[pl.ds(src, n_chunks), :],
                                     xbuf.at[s, pl.ds(r * pitch, n_chunks), :], gsem.at[s])

    def start_rows_loop(idx_ref, s):
        def issue(r, carry):
            row_copy(idx_ref, s, r).start()
            return carry
        lax.fori_loop(0, R, issue, 0, unroll=GATHER_UNROLL)

    def wait_rows(s):
        pltpu.make_async_copy(hn_hbm.at[pl.ds(0, R * n_chunks), :], xbuf.at[s, pl.ds(0, R * n_chunks), :],
                              gsem.at[s]).wait()

    def weight_copies(e):
        return (pltpu.make_async_copy(wg_hbm.at[layer, e], stage_g, wsem.at[0]),
                pltpu.make_async_copy(wu_hbm.at[layer, e], stage_u, wsem.at[1]),
                pltpu.make_async_copy(wd_hbm.at[layer, e], stage_d, wsem.at[2]))

    @pl.when(b == 0)
    def _():
        start_rows_loop(tok_ref, 0)
        start_rows_loop(tok1_ref, 1)
        for cp in weight_copies(be_ref[0]):
            cp.start(priority=WEIGHT_DMA_PRIORITY)

    @pl.when(first_ref[b] == 1)
    def _():
        for cp in weight_copies(be_ref[b]):
            cp.wait()
        for stage, dst in ((stage_g, wg_ref), (stage_u, wu_ref), (stage_d, wd_ref)):
            def cast_rows(r, carry, stage=stage, dst=dst):
                rows = pl.ds(pl.multiple_of(r * CAST_ROWS, CAST_ROWS), CAST_ROWS)
                dst[rows, :] = stage[rows, :].astype(BF16)
                return carry
            lax.fori_loop(0, stage.shape[0] // CAST_ROWS, cast_rows, 0)

        @pl.when(next_ref[b] >= 0)
        def _():
            for cp in weight_copies(next_ref[b]):
                cp.start(priority=WEIGHT_DMA_PRIORITY)

    @pl.when(active_ref[b] == 1)
    def _():
        wait_rows(slot)
        x = jnp.concatenate([_load_slab_chunk(xbuf, (slot,), c, R, pitch).astype(BF16)
                             for c in range(n_chunks)], axis=1)
        for r in range(R):
            row_copy(tok2_ref, slot2, r).start()
        a = jnp.dot(x, wg_ref[...], preferred_element_type=F32)
        u = jnp.dot(x, wu_ref[...], preferred_element_type=F32)
        hmid = (a * jax.nn.sigmoid(a) * u).astype(BF16)
        _store_slabs(o_ref, jnp.dot(hmid, wd_ref[...], preferred_element_type=F32))

    @pl.when(active_ref[b] == 0)
    def _():
        wait_rows(slot)
        start_rows_loop(tok2_ref, slot2)
        o_ref[...] = jnp.zeros_like(o_ref)

    @pl.when(b == n_blocks - 1)
    def _():
        wait_rows(lax.rem(b + 1, ROW_SLOTS))
        wait_rows(slot2)


def expert_blocks(block_e, first, next_e, active, row_tok, hn, w_gate, w_up, w_down, layer):
    n_blocks, _, R = row_tok.shape
    assert n_blocks >= 2
    D, Dh = w_gate.shape[2:]
    n_chunks = D // V7X_LANES
    hbm = pl.BlockSpec(memory_space=pl.ANY)

    def tok_spec(ahead):
        return pl.BlockSpec((None, 1, R), lambda b, *_: (jnp.minimum(b + ahead, n_blocks - 1), 0, 0),
                            memory_space=pltpu.SMEM)

    grid_spec = pltpu.PrefetchScalarGridSpec(
        num_scalar_prefetch=4,
        grid=(n_blocks,),
        in_specs=[tok_spec(0), tok_spec(1), tok_spec(2), hbm, hbm, hbm, hbm],
        out_specs=pl.BlockSpec((R * n_chunks, V7X_LANES), lambda b, *_: (b, 0)),
        scratch_shapes=[pltpu.VMEM((ROW_SLOTS, R * _gather_pitch(n_chunks), V7X_LANES), F32),
                        pltpu.VMEM((D, Dh), F32), pltpu.VMEM((D, Dh), F32), pltpu.VMEM((Dh, D), F32),
                        pltpu.VMEM((D, Dh), BF16), pltpu.VMEM((D, Dh), BF16), pltpu.VMEM((Dh, D), BF16),
                        pltpu.SemaphoreType.DMA((3,)), pltpu.SemaphoreType.DMA((ROW_SLOTS,))],
    )
    return pl.pallas_call(
        functools.partial(_expert_kernel, layer=layer),
        out_shape=jax.ShapeDtypeStruct((n_blocks * R * n_chunks, V7X_LANES), F32),
        grid_spec=grid_spec,
        compiler_params=pltpu.CompilerParams(
            dimension_semantics=("arbitrary",),
            vmem_limit_bytes=_vmem_limit(3 * D * Dh * (4 + 2) + (ROW_SLOTS + 2) * R * D * 4 + R * D * 2
                                         + 6 * R * Dh * 4)),
        name="moe_experts",
    )(block_e, first, next_e, active, row_tok, row_tok, row_tok, hn, w_gate, w_up, w_down)


COMBINE_ROWS = 256


def _combine_kernel(idx_ref, idx_next_ref, x_ref, wb_ref, yb_hbm, o_ref, cbuf, gsem):
    i = pl.program_id(0)
    n_tiles = pl.num_programs(0)
    tm, D = x_ref.shape
    n_chunks = D // V7X_LANES
    pitch = _gather_pitch(n_chunks)
    slot = lax.rem(i, 2)

    def row_copy(idx, s, r, k):
        src = pl.multiple_of(idx[0, TOP_K * r + k], n_chunks)
        return pltpu.make_async_copy(yb_hbm.at[pl.ds(src, n_chunks), :],
                                     cbuf.at[s, k, pl.ds(r * pitch, n_chunks), :], gsem.at[s])

    def wait_rows(s):
        for k in range(TOP_K):
            pltpu.make_async_copy(yb_hbm.at[pl.ds(0, tm * n_chunks), :],
                                  cbuf.at[s, k, pl.ds(0, tm * n_chunks), :], gsem.at[s]).wait()

    @pl.when(i == 0)
    def _():
        def issue(r, carry):
            for k in range(TOP_K):
                row_copy(idx_ref, 0, r, k).start()
            return carry
        lax.fori_loop(0, tm, issue, 0, unroll=GATHER_UNROLL)

    wait_rows(slot)
    for r in range(tm):
        for k in range(TOP_K):
            row_copy(idx_next_ref, 1 - slot, r, k).start(priority=k % 2)
    for c in range(n_chunks):
        cols = slice(c * V7X_LANES, (c + 1) * V7X_LANES)
        acc = x_ref[:, cols]
        for k in range(TOP_K):
            acc = acc + wb_ref[:, k * V7X_LANES:(k + 1) * V7X_LANES] * _load_slab_chunk(cbuf, (slot, k), c, tm, pitch)
        o_ref[:, cols] = acc

    @pl.when(i == n_tiles - 1)
    def _():
        wait_rows(1 - slot)


def combine_residual(x, wb, dest, yb):
    T, D = x.shape
    tm = min(COMBINE_ROWS, T)
    n_tiles = T // tm
    idx = (dest * (D // V7X_LANES)).reshape(n_tiles, 1, tm * TOP_K)
    return pl.pallas_call(
        _combine_kernel,
        out_shape=jax.ShapeDtypeStruct((T, D), F32),
        grid=(n_tiles,),
        in_specs=[pl.BlockSpec((None, 1, tm * TOP_K), lambda i: (i, 0, 0), memory_space=pltpu.SMEM),
                  pl.BlockSpec((None, 1, tm * TOP_K), lambda i: (jnp.minimum(i + 1, n_tiles - 1), 0, 0),
                               memory_space=pltpu.SMEM),
                  pl.BlockSpec((tm, D), lambda i: (i, 0)),
                  pl.BlockSpec((tm, wb.shape[1]), lambda i: (i, 0)),
                  pl.BlockSpec(memory_space=pl.ANY)],
        out_specs=pl.BlockSpec((tm, D), lambda i: (i, 0)),
        scratch_shapes=[pltpu.VMEM((2, TOP_K, tm * _gather_pitch(D // V7X_LANES), V7X_LANES), F32),
                        pltpu.SemaphoreType.DMA((2,))],
        compiler_params=pltpu.CompilerParams(
            dimension_semantics=("arbitrary",),
            vmem_limit_bytes=_vmem_limit(2 * TOP_K * tm * D * 4 + 6 * tm * D * 4)),
        name="moe_combine",
    )(idx, idx, x, wb, yb)


def moe_residual(x, g, rgw, rgb, rew, reb, w_gate, w_up, w_down, layer):
    T, D = x.shape
    R = MOE_ROWS
    assert TOP_K == 2
    pad = V7X_LANES - N_GROUPS - N_EXPERTS
    w_router = jnp.concatenate([rgw, rew, jnp.zeros((D, pad), F32)], axis=1)
    b_router = jnp.concatenate([rgb, reb, jnp.zeros((pad,), F32)]).reshape(1, V7X_LANES)
    hn, route, wb = router(x, g, w_router, b_router)
    expert_id = route[:, :TOP_K].astype(jnp.int32)

    n_assign = T * TOP_K
    flat_e = expert_id.reshape(-1)
    onehot = (flat_e[:, None] == jnp.arange(N_EXPERTS, dtype=jnp.int32)[None, :]).astype(jnp.int32)
    running = jnp.cumsum(onehot, axis=0)
    counts = running[-1]
    rank = jnp.take_along_axis(running, flat_e[:, None], axis=1)[:, 0] - 1
    padded = (counts + R - 1) // R * R
    pends = jnp.cumsum(padded)
    pstarts = pends - padded
    dest = pstarts[flat_e] + rank
    n_blocks = (n_assign + N_EXPERTS * (R - 1) + R - 1) // R
    flat_tok = jnp.repeat(jnp.arange(T, dtype=jnp.int32), TOP_K)
    row_tok = jnp.zeros((n_blocks * R,), jnp.int32).at[dest].set(flat_tok)

    blk_start = jnp.arange(n_blocks, dtype=jnp.int32) * R
    active = blk_start < pends[-1]
    block_e = jnp.minimum(jnp.sum(blk_start[:, None] >= pends[None, :], axis=1), N_EXPERTS - 1).astype(jnp.int32)
    prev_e = jnp.concatenate([jnp.full((1,), -1, jnp.int32), block_e[:-1]])
    first = jnp.logical_and(active, block_e != prev_e)
    later = lax.cummin(jnp.where(first, block_e, N_EXPERTS)[::-1])[::-1]
    next_e = jnp.concatenate([later[1:], jnp.full((1,), N_EXPERTS, jnp.int32)])
    next_e = jnp.where(next_e >= N_EXPERTS, -1, next_e).astype(jnp.int32)

    yb = expert_blocks(block_e, first.astype(jnp.int32), next_e, active.astype(jnp.int32),
                       (row_tok * (D // V7X_LANES)).reshape(n_blocks, 1, R), hn, w_gate, w_up, w_down, layer)
    return combine_residual(x, wb, dest.reshape(T, TOP_K), yb)


def fox_mlstm_residual(xt, norm_g, w, fox_f_bias, fox_q_gain, fox_k_gain, mlstm_i_bias, mlstm_f_bias,
                       mlstm_out_gain, w_out):
    S, D = xt.shape
    H = fox_f_bias.shape[0]
    assert mlstm_i_bias.shape[0] == H and 3 * H <= 32
    fw = H * HEAD_DIM
    o_ff = 3 * fw
    o_mq = o_ff + H
    o_mi = o_mq + 3 * fw
    o_mo = o_mi + 2 * H
    w_main = jnp.concatenate([w[:, :o_ff], w[:, o_mq:o_mi]], axis=1).astype(BF16)
    gate_pad = V7X_LANES - 3 * H
    w_aux = jnp.concatenate([w[:, o_mo:], w[:, o_ff:o_mq], w[:, o_mi:o_mo],
                             jnp.zeros((D, gate_pad), F32)], axis=1).astype(BF16)
    gains = jnp.concatenate([fox_q_gain[None] * (LOG2E * HEAD_DIM ** -0.5), fox_k_gain[None],
                             jnp.zeros((6, HEAD_DIM), F32)], axis=0)
    bias_row = jnp.concatenate([fox_f_bias, mlstm_i_bias, mlstm_f_bias,
                                jnp.zeros((gate_pad,), F32)]).reshape(1, V7X_LANES)

    hn, mo, gates_pre = norm_inproj_aux(xt, norm_g, w_aux, fw)
    proj = inproj_main(hn, w_main, gains)
    gcol = gate_activations(gates_pre, bias_row, H)
    grow = gcol[:, :32].T
    cq = jnp.broadcast_to(grow[:H, :, None], (H, S, V7X_LANES))
    ck = grow[:H].reshape(H, 1, S)
    y_fox = fox_attention(proj, cq, ck, H, gains[0], gains[1])
    y_mlstm = mlstm_mixer(proj, mo, gcol, grow, mlstm_out_gain, H, 3)
    return outproj_residual(y_fox, y_mlstm, w_out.astype(BF16), xt)


def kernel(x, norm_mix, norm_ffn, w_in, fox_f_bias, fox_q_gain, fox_k_gain, mlstm_i_bias, mlstm_f_bias,
           mlstm_out_gain, w_out, pool_w, pool_b, pool_scale, router_group_w, router_group_b,
           router_expert_w, router_expert_b, w_gate, w_up, w_down):
    B, S, D = x.shape
    assert B == 1
    depth = norm_mix.shape[0]
    xt = x.reshape(S, D)

    for layer in range(depth):
        j = layer // 2
        if layer % 2 == 0:
            xt = fox_mlstm_residual(xt, norm_mix[layer], w_in[j], fox_f_bias[j], fox_q_gain[j],
                                    fox_k_gain[j], mlstm_i_bias[j], mlstm_f_bias[j], mlstm_out_gain[j],
                                    w_out[j])
        else:
            xt = pool_mixer_residual(xt, norm_mix[layer], pool_w[j].astype(BF16), pool_b[j], pool_scale[j])
        xt = moe_residual(xt, norm_ffn[layer], router_group_w[layer], router_group_b[layer],
                          router_expert_w[layer], router_expert_b[layer],
                          w_gate, w_up, w_down, layer)
    return xt.reshape(B, S, D)
```

```python
import functools

import jax
import jax.numpy as jnp
from jax import lax
from jax.experimental import pallas as pl
from jax.experimental.pallas import tpu as pltpu

F32 = jnp.float32
BF16 = jnp.bfloat16

HEAD_DIM = 128
GATE_SOFTCAP = 15.0
POOL_WINDOWS = (2, 4, 8, 16)
POOL_HALO = 16
N_GROUPS = 4
EXPERTS_PER_GROUP = 8
N_EXPERTS = N_GROUPS * EXPERTS_PER_GROUP
TOP_K = 2
RMS_EPS = 1e-6

V7X_LANES = 128
V7X_VMEM_BYTES = 64 * 1024 * 1024

NORM_ROWS = 512
MM_ROWS = 1024
MM_COLS = 1024
ATT_Q = 1024
ATT_K = 512
MLSTM_CHUNK = 256
MOE_ROWS = 256
NEG_BIG = -1e30
LOG2E = 1.4426950408889634


def _vmem_limit(nbytes):
    return int(min(max(nbytes * 3 // 2, 16 * 1024 * 1024), V7X_VMEM_BYTES - 8 * 1024 * 1024))


def _rms(x, eps=RMS_EPS):
    return x * lax.rsqrt(jnp.mean(x * x, axis=-1, keepdims=True) + eps)


def _log_sigmoid(x):
    return -(jnp.maximum(-x, 0.0) + jnp.log1p(jnp.exp(-jnp.abs(x))))


def _inproj_main_kernel(a_ref, w_ref, gain_ref, o_ref, *, n_heads_per_tile):
    j = pl.program_id(1)
    acc = jnp.dot(a_ref[...], w_ref[...], preferred_element_type=F32)

    @pl.when(j < 2)
    def _():
        g = gain_ref[pl.ds(j, 1), :]
        for h in range(n_heads_per_tile):
            a = acc[:, h * HEAD_DIM:(h + 1) * HEAD_DIM]
            o_ref[:, h * HEAD_DIM:(h + 1) * HEAD_DIM] = (_rms(a) * g).astype(o_ref.dtype)

    @pl.when(j == 4)
    def _():
        o_ref[...] = (acc * (HEAD_DIM ** -0.5)).astype(o_ref.dtype)

    @pl.when(jnp.logical_and(j >= 2, j != 4))
    def _():
        o_ref[...] = acc.astype(o_ref.dtype)


def inproj_main(hn, w_main, gains):
    S, D = hn.shape
    N = w_main.shape[1]
    tm, tn = min(MM_ROWS, S), MM_COLS
    return pl.pallas_call(
        functools.partial(_inproj_main_kernel, n_heads_per_tile=tn // HEAD_DIM),
        out_shape=jax.ShapeDtypeStruct((S, N), BF16),
        grid=(S // tm, N // tn),
        in_specs=[pl.BlockSpec((tm, D), lambda i, j: (i, 0)),
                  pl.BlockSpec((D, tn), lambda i, j: (0, j)),
                  pl.BlockSpec((8, HEAD_DIM), lambda i, j: (0, 0))],
        out_specs=pl.BlockSpec((tm, tn), lambda i, j: (i, j)),
        compiler_params=pltpu.CompilerParams(
            dimension_semantics=("parallel", "parallel"),
            vmem_limit_bytes=_vmem_limit(2 * (tm * D * 2 + D * tn * 2 + tm * tn * 2) + 2 * tm * tn * 4)),
        name="inproj_main",
    )(hn, w_main, gains)


def _norm_inproj_aux_kernel(x_ref, g_ref, w_ref, hn_ref, mo_ref, gate_ref):
    hn = (_rms(x_ref[...]) * g_ref[...]).astype(BF16)
    hn_ref[...] = hn
    acc = jnp.dot(hn, w_ref[...], preferred_element_type=F32)
    n_mo = mo_ref.shape[1]
    mo_ref[...] = acc[:, :n_mo]
    gate_ref[...] = acc[:, n_mo:]


def norm_inproj_aux(x, g, w_aux, n_mo):
    S, D = x.shape
    N = w_aux.shape[1]
    tm = min(NORM_ROWS, S)
    return pl.pallas_call(
        _norm_inproj_aux_kernel,
        out_shape=(jax.ShapeDtypeStruct((S, D), BF16),
                   jax.ShapeDtypeStruct((S, n_mo), F32),
                   jax.ShapeDtypeStruct((S, N - n_mo), F32)),
        grid=(S // tm,),
        in_specs=[pl.BlockSpec((tm, D), lambda i: (i, 0)),
                  pl.BlockSpec((1, D), lambda i: (0, 0)),
                  pl.BlockSpec((D, N), lambda i: (0, 0))],
        out_specs=(pl.BlockSpec((tm, D), lambda i: (i, 0)),
                   pl.BlockSpec((tm, n_mo), lambda i: (i, 0)),
                   pl.BlockSpec((tm, N - n_mo), lambda i: (i, 0))),
        compiler_params=pltpu.CompilerParams(
            dimension_semantics=("parallel",),
            vmem_limit_bytes=_vmem_limit(2 * (tm * D * 6 + D * N * 2 + tm * N * 4) + tm * N * 4 + 2 * tm * D * 4)),
        name="norm_inproj_aux",
    )(x, g.reshape(1, D), w_aux)


def _split3_dot(tri, val):
    v1 = val.astype(BF16)
    r1 = val - v1.astype(F32)
    v2 = r1.astype(BF16)
    v3 = (r1 - v2.astype(F32)).astype(BF16)
    out = jnp.dot(tri, v1, preferred_element_type=F32)
    out += jnp.dot(tri, v2, preferred_element_type=F32)
    out += jnp.dot(tri, v3, preferred_element_type=F32)
    return out


def _gates_kernel(g_ref, bias_ref, o_ref, carry_ref, *, n_heads):
    @pl.when(pl.program_id(0) == 0)
    def _():
        carry_ref[...] = jnp.zeros_like(carry_ref)

    rows = g_ref.shape[0]
    z = g_ref[...] + bias_ref[...]
    lane = lax.broadcasted_iota(jnp.int32, z.shape, 1)
    capped = GATE_SOFTCAP * jnp.tanh(z / GATE_SOFTCAP)
    is_fox = lane < n_heads
    is_i = jnp.logical_and(lane >= n_heads, lane < 2 * n_heads)
    is_f = jnp.logical_and(lane >= 2 * n_heads, lane < 3 * n_heads)
    logf = jnp.where(is_fox, _log_sigmoid(z), jnp.where(is_f, _log_sigmoid(capped), 0.0))
    r = lax.broadcasted_iota(jnp.int32, (rows, rows), 0)
    c = lax.broadcasted_iota(jnp.int32, (rows, rows), 1)
    tri = jnp.where(r >= c, 1.0, 0.0).astype(BF16)
    cum = _split3_dot(tri, logf)
    glob = cum + carry_ref[...]
    o_ref[...] = jnp.where(is_fox, glob * LOG2E, jnp.where(is_i, capped, cum))
    carry_ref[...] = glob[rows - 1:rows, :]


def gate_activations(gates_pre, bias_row, n_heads):
    S, W = gates_pre.shape
    tb = MLSTM_CHUNK
    return pl.pallas_call(
        functools.partial(_gates_kernel, n_heads=n_heads),
        out_shape=jax.ShapeDtypeStruct((S, W), F32),
        grid=(S // tb,),
        in_specs=[pl.BlockSpec((tb, W), lambda i: (i, 0)),
                  pl.BlockSpec((1, W), lambda i: (0, 0))],
        out_specs=pl.BlockSpec((tb, W), lambda i: (i, 0)),
        scratch_shapes=[pltpu.VMEM((1, W), F32)],
        compiler_params=pltpu.CompilerParams(dimension_semantics=("arbitrary",)),
        name="gate_activations",
    )(gates_pre, bias_row)


def _fox_kernel(first_ref, q_ref, k_ref, v_ref, cq_ref, ck_ref, o_ref, m_ref, acc_ref, s_ref, p_ref, alpha_ref,
                *, n_sub):
    i = pl.program_id(1)
    d = HEAD_DIM
    tk = q_ref.shape[0] // n_sub
    assert n_sub % 2 == 0

    m_ref[...] = jnp.full_like(m_ref, NEG_BIG)
    acc_ref[...] = jnp.zeros_like(acc_ref)
    p_ref[1] = jnp.zeros_like(p_ref[1])
    alpha_ref[1] = jnp.ones_like(alpha_ref[1])
    lane = lax.broadcasted_iota(jnp.int32, (tk, d), 1)
    ones_col = jnp.where(lane == 0, 1.0, 0.0).astype(BF16)
    row = lax.broadcasted_iota(jnp.int32, (tk, tk), 0)
    col = lax.broadcasted_iota(jnp.int32, (tk, tk), 1)
    causal = col <= row

    def qk_stage(sub, j, par):
        start = pl.multiple_of(j * tk, tk)
        s = lax.dot_general(q_ref[pl.ds(sub * tk, tk), :], k_ref[pl.ds(start, tk), :],
                            (((1,), (1,)), ((), ())), preferred_element_type=F32)
        s_ref[par, sub] = s - ck_ref[:, pl.ds(start, tk)]

    def sm_stage(sub, par, masked):
        rows = pl.ds(sub * tk, tk)
        s = s_ref[par, sub]
        if masked:
            s = jnp.where(causal, s, NEG_BIG)
        cq = cq_ref[rows, :]
        m_prev = m_ref[rows, :]
        m_new = jnp.maximum(m_prev, jnp.max(s, axis=-1, keepdims=True) + cq)
        p_ref[par, sub] = jnp.exp2(s - jnp.tile(m_new - cq, (1, tk // V7X_LANES))).astype(BF16)
        alpha_ref[par, rows, :] = jnp.exp2(m_prev - m_new)
        m_ref[rows, :] = m_new

    def pv_stage(sub, j, par):
        rows = pl.ds(sub * tk, tk)
        start = pl.multiple_of(j * tk, tk)
        v_aug = jnp.concatenate([v_ref[pl.ds(start, tk), :], ones_col], axis=1)
        acc_ref[rows, :] = (jnp.tile(alpha_ref[par, rows, :], (1, 2)) * acc_ref[rows, :]
                            + jnp.dot(p_ref[par, sub], v_aug, preferred_element_type=F32))

    n_full = i * n_sub
    first = first_ref[pl.program_id(0) * pl.num_programs(1) + i]
    for sub in range(n_sub):
        qk_stage(sub, first, 0)

    def body(tt, carry):
        for par in (0, 1):
            step = 2 * tt + par
            for sub in range(n_sub):
                qk_stage(sub, step + 1, 1 - par)
                sm_stage(sub, par, masked=False)
                pv_stage(sub, jnp.maximum(step - 1, 0), 1 - par)
        return carry

    lax.fori_loop(first // 2, n_full // 2, body, 0)
    for kk in range(n_sub + 1):
        par = kk % 2
        for sub in range(n_sub):
            if kk + 1 <= sub:
                qk_stage(sub, n_full + kk + 1, 1 - par)
            if kk <= sub:
                sm_stage(sub, par, masked=(kk == sub))
            if kk - 1 <= sub:
                pv_stage(sub, jnp.maximum(n_full + kk - 1, 0), 1 - par)
    acc = acc_ref[...]
    o_ref[...] = (acc[:, :d] / acc[:, d:d + 1]).astype(o_ref.dtype)


UNDERFLOW_LOG2 = 152.0


def _fox_first_blocks(q_gain, k_gain, c2, tk, n_sub):
    H, S = c2.shape
    nb = S // tk
    slack = 1.0 + 2.0 ** -6
    qk_bound = HEAD_DIM * jnp.max(jnp.abs(q_gain)) * jnp.max(jnp.abs(k_gain)) * slack
    cb = c2.reshape(H, nb, tk)
    c_hi, c_lo = jnp.max(cb, axis=-1), jnp.min(cb, axis=-1)
    upper = qk_bound + c_hi[:, :, None] - c_lo[:, None, :]
    lower = -qk_bound
    j_idx = jnp.arange(nb, dtype=jnp.int32)
    needed = jnp.logical_or(upper >= lower - UNDERFLOW_LOG2, j_idx[None, None, :] >= j_idx[None, :, None])
    first = jnp.min(jnp.where(needed, j_idx[None, None, :], nb), axis=-1)
    first = jnp.min(first.reshape(H, nb // n_sub, n_sub), axis=-1)
    return ((first // 2) * 2).reshape(-1).astype(jnp.int32)


def fox_attention(proj, cq, ck, n_heads, q_gain, k_gain):
    S = proj.shape[0]
    tk = min(ATT_K, S)
    n_sub = max(1, min(ATT_Q, S) // tk)
    tq = n_sub * tk
    H = n_heads
    first = _fox_first_blocks(q_gain, k_gain, ck.reshape(H, S), tk, n_sub)
    grid_spec = pltpu.PrefetchScalarGridSpec(
        num_scalar_prefetch=1,
        grid=(H, S // tq),
        in_specs=[pl.BlockSpec((tq, HEAD_DIM), lambda h, i, *_: (i, h)),
                  pl.BlockSpec((S, HEAD_DIM), lambda h, i, *_: (0, H + h)),
                  pl.BlockSpec((S, HEAD_DIM), lambda h, i, *_: (0, 2 * H + h)),
                  pl.BlockSpec((None, tq, V7X_LANES), lambda h, i, *_: (h, i, 0)),
                  pl.BlockSpec((None, 1, S), lambda h, i, *_: (h, 0, 0))],
        out_specs=pl.BlockSpec((tq, HEAD_DIM), lambda h, i, *_: (i, h)),
        scratch_shapes=[pltpu.VMEM((tq, V7X_LANES), F32), pltpu.VMEM((tq, 2 * HEAD_DIM), F32),
                        pltpu.VMEM((2, n_sub, tk, tk), F32), pltpu.VMEM((2, n_sub, tk, tk), BF16),
                        pltpu.VMEM((2, tq, V7X_LANES), F32)],
    )
    return pl.pallas_call(
        functools.partial(_fox_kernel, n_sub=n_sub),
        out_shape=jax.ShapeDtypeStruct((S, H * HEAD_DIM), BF16),
        grid_spec=grid_spec,
        compiler_params=pltpu.CompilerParams(
            dimension_semantics=("parallel", "arbitrary"),
            vmem_limit_bytes=_vmem_limit(4 * S * HEAD_DIM * 2 + 2 * n_sub * tk * tk * (4 + 2)
                                         + 4 * tk * tk * 4 + 12 * tq * HEAD_DIM * 4 + 16 * S * 4)),
        name="fox_attention",
    )(first, proj, proj, proj, cq, ck)


def _mlstm_kernel(q_ref, k_ref, v_ref, mo_ref, gcol_ref, grow_ref, gain_ref, o_ref,
                  state_ref, m_ref, *, n_heads):
    L = q_ref.shape[0]
    d = HEAD_DIM

    @pl.when(pl.program_id(0) == 0)
    def _():
        state_ref[...] = jnp.zeros_like(state_ref)
        m_ref[...] = jnp.zeros_like(m_ref)

    row = lax.broadcasted_iota(jnp.int32, (L, L), 0)
    col = lax.broadcasted_iota(jnp.int32, (L, L), 1)
    causal = col <= row
    lane = lax.broadcasted_iota(jnp.int32, (L, d), 1)
    ones_col = jnp.where(lane == 0, 1.0, 0.0).astype(BF16)

    gcol = gcol_ref[...]
    grow = grow_ref[...]
    for h in range(n_heads):
        sl = slice(h * d, (h + 1) * d)
        q = q_ref[:, sl]
        k = k_ref[:, sl]
        v = v_ref[:, sl]
        i_col = gcol[:, n_heads + h:n_heads + h + 1]
        b_col = gcol[:, 2 * n_heads + h:2 * n_heads + h + 1]
        i_row = grow[n_heads + h:n_heads + h + 1, :]
        b_row = grow[2 * n_heads + h:2 * n_heads + h + 1, :]
        m_prev = m_ref[h:h + 1, 0:1]
        state = state_ref[h]

        log_intra = jnp.where(causal, b_col - b_row + i_row, NEG_BIG)
        log_inter = b_col + m_prev
        m_t = jnp.maximum(log_inter, jnp.max(log_intra, axis=-1, keepdims=True))
        w_intra = jnp.exp(log_intra - m_t)
        w_inter = jnp.exp(log_inter - m_t)
        qk = lax.dot_general(q, k, (((1,), (1,)), ((), ())), preferred_element_type=F32) * w_intra
        v_aug = jnp.concatenate([v, ones_col], axis=1)
        tot = jnp.dot(qk.astype(BF16), v_aug, preferred_element_type=F32)
        tot = tot + w_inter * jnp.dot(q, state.astype(BF16), preferred_element_type=F32)
        num = tot[:, :d]
        den = tot[:, d:d + 1]
        hval = num / jnp.maximum(jnp.abs(den), jnp.exp(-m_t))

        b_last = b_col[L - 1:L, :]
        log_w_state = b_last - b_col + i_col
        m_new = jnp.maximum(b_last + m_prev, jnp.max(log_w_state, axis=0, keepdims=True))
        decay = jnp.exp(b_last + m_prev - m_new)
        w_s = jnp.exp(log_w_state - m_new)
        wv = (w_s * v_aug.astype(F32)).astype(BF16)
        upd = lax.dot_general(k, wv, (((0,), (0,)), ((), ())), preferred_element_type=F32)
        state_ref[h] = decay * state + upd
        m_ref[h:h + 1, :] = jnp.broadcast_to(m_new, (1, m_ref.shape[1]))

        hn = _rms(hval) * gain_ref[:, sl]
        o_ref[:, sl] = (jax.nn.sigmoid(mo_ref[:, sl]) * hn).astype(o_ref.dtype)


def mlstm_mixer(proj, mo, gcol, grow, out_gain, n_heads, q_block):
    S = proj.shape[0]
    L = min(MLSTM_CHUNK, S)
    W = n_heads * HEAD_DIM
    return pl.pallas_call(
        functools.partial(_mlstm_kernel, n_heads=n_heads),
        out_shape=jax.ShapeDtypeStruct((S, W), BF16),
        grid=(S // L,),
        in_specs=[pl.BlockSpec((L, W), lambda c: (c, q_block)),
                  pl.BlockSpec((L, W), lambda c: (c, q_block + 1)),
                  pl.BlockSpec((L, W), lambda c: (c, q_block + 2)),
                  pl.BlockSpec((L, W), lambda c: (c, 0)),
                  pl.BlockSpec((L, gcol.shape[1]), lambda c: (c, 0)),
                  pl.BlockSpec((grow.shape[0], L), lambda c: (0, c)),
                  pl.BlockSpec((1, W), lambda c: (0, 0))],
        out_specs=pl.BlockSpec((L, W), lambda c: (c, 0)),
        scratch_shapes=[pltpu.VMEM((n_heads, HEAD_DIM, 2 * HEAD_DIM), F32),
                        pltpu.VMEM((n_heads, V7X_LANES), F32)],
        compiler_params=pltpu.CompilerParams(
            dimension_semantics=("arbitrary",),
            vmem_limit_bytes=_vmem_limit(2 * L * W * (3 * 2 + 4 + 2) + 16 * L * L * 4)),
        name="mlstm",
    )(proj, proj, proj, mo, gcol, grow, out_gain.reshape(1, W))


def _outproj_kernel(a1_ref, a2_ref, w_ref, x_ref, o_ref):
    k1 = a1_ref.shape[1]
    acc = jnp.dot(a1_ref[...], w_ref[:k1, :], preferred_element_type=F32)
    acc += jnp.dot(a2_ref[...], w_ref[k1:, :], preferred_element_type=F32)
    o_ref[...] = x_ref[...] + acc


def outproj_residual(a1, a2, w, x):
    S, K1 = a1.shape
    K2 = a2.shape[1]
    N = w.shape[1]
    tm, tn = min(MM_ROWS, S), MM_COLS
    return pl.pallas_call(
        _outproj_kernel,
        out_shape=jax.ShapeDtypeStruct((S, N), F32),
        grid=(S // tm, N // tn),
        in_specs=[pl.BlockSpec((tm, K1), lambda i, j: (i, 0)),
                  pl.BlockSpec((tm, K2), lambda i, j: (i, 0)),
                  pl.BlockSpec((K1 + K2, tn), lambda i, j: (0, j)),
                  pl.BlockSpec((tm, tn), lambda i, j: (i, j))],
        out_specs=pl.BlockSpec((tm, tn), lambda i, j: (i, j)),
        compiler_params=pltpu.CompilerParams(
            dimension_semantics=("parallel", "parallel"),
            vmem_limit_bytes=_vmem_limit(2 * (tm * (K1 + K2) * 2 + (K1 + K2) * tn * 2 + 2 * tm * tn * 4)
                                         + tm * tn * 4)),
        name="outproj",
    )(a1, a2, w, x)


def _pool_kernel(x_ref, g_ref, w_ref, b_ref, scale_ref, o_ref, carry_ref):
    i = pl.program_id(0)
    tm = x_ref.shape[0]
    gw = w_ref.shape[1]

    @pl.when(i == 0)
    def _():
        carry_ref[...] = jnp.zeros_like(carry_ref)

    x = x_ref[...]
    hn = _rms(x) * g_ref[...]
    t = i * tm + lax.broadcasted_iota(jnp.int32, (tm, 1), 0)
    for g, w in enumerate(POOL_WINDOWS):
        sl = slice(g * gw, (g + 1) * gw)
        hg = hn[:, sl]
        cur = jnp.concatenate([carry_ref[:, sl], hg], axis=0)
        k = 1
        while k < w:
            cur = cur + pltpu.roll(cur, k, axis=0)
            k *= 2
        window_sum = cur[POOL_HALO:, :]
        count = jnp.minimum(t + 1, w).astype(F32)
        pooled = window_sum / count - hg
        y = jnp.dot(pooled.astype(BF16), w_ref[g], preferred_element_type=F32) + b_ref[:, sl]
        o_ref[:, sl] = x[:, sl] + y * scale_ref[:, sl]
    carry_ref[...] = hn[tm - POOL_HALO:, :]


def pool_mixer_residual(x, g, pool_w, pool_b, pool_scale):
    S, D = x.shape
    tm = min(NORM_ROWS, S)
    G, gw, _ = pool_w.shape
    return pl.pallas_call(
        _pool_kernel,
        out_shape=jax.ShapeDtypeStruct((S, D), F32),
        grid=(S // tm,),
        in_specs=[pl.BlockSpec((tm, D), lambda i: (i, 0)),
                  pl.BlockSpec((1, D), lambda i: (0, 0)),
                  pl.BlockSpec((G, gw, gw), lambda i: (0, 0, 0)),
                  pl.BlockSpec((1, D), lambda i: (0, 0)),
                  pl.BlockSpec((1, D), lambda i: (0, 0))],
        out_specs=pl.BlockSpec((tm, D), lambda i: (i, 0)),
        scratch_shapes=[pltpu.VMEM((POOL_HALO, D), F32)],
        compiler_params=pltpu.CompilerParams(
            dimension_semantics=("arbitrary",),
            vmem_limit_bytes=_vmem_limit(4 * tm * D * 4 + 2 * G * gw * gw * 2 + 6 * tm * D * 4)),
        name="pool_mixer",
    )(x, g.reshape(1, D), pool_w, pool_b.reshape(1, D), pool_scale.reshape(1, D))


def _store_slabs(ref, val):
    n = val.shape[0]
    n_chunks = val.shape[1] // V7X_LANES
    for c in range(n_chunks):
        ref[pl.ds(c, n, stride=n_chunks), :] = val[:, c * V7X_LANES:(c + 1) * V7X_LANES]


def _load_slab_chunk(ref, lead, c, n, pitch):
    return ref[lead + (pl.ds(c, n, stride=pitch), slice(None))]


def _gather_pitch(n_chunks):
    return n_chunks + 4 if n_chunks % 8 == 0 else n_chunks


def _dot_bf16x3(a, b):
    a_hi = a.astype(BF16)
    a_lo = (a - a_hi.astype(F32)).astype(BF16)
    b_hi = b.astype(BF16)
    b_lo = (b - b_hi.astype(F32)).astype(BF16)
    out = jnp.dot(a_hi, b_hi, preferred_element_type=F32)
    out += jnp.dot(a_hi, b_lo, preferred_element_type=F32)
    out += jnp.dot(a_lo, b_hi, preferred_element_type=F32)
    return out


def _router_kernel(x_ref, g_ref, w_ref, b_ref, hn_ref, route_ref, wb_ref):
    hn = _rms(x_ref[...]) * g_ref[...]
    _store_slabs(hn_ref, hn)
    logits = _dot_bf16x3(hn, w_ref[...]) + b_ref[...]
    lane = lax.broadcasted_iota(jnp.int32, logits.shape, 1).astype(F32)
    n_lanes = float(logits.shape[1])

    def first_argmax(vals):
        top = jnp.max(vals, axis=-1, keepdims=True)
        return top, jnp.min(jnp.where(vals == top, lane, n_lanes), axis=-1, keepdims=True)

    is_group = lane < N_GROUPS
    g_top, g_sel = first_argmax(jnp.where(is_group, logits, NEG_BIG))
    g_w = 1.0 / jnp.sum(jnp.where(is_group, jnp.exp(logits - g_top), 0.0), axis=-1, keepdims=True)
    lo = N_GROUPS + EXPERTS_PER_GROUP * g_sel
    e_logits = jnp.where(jnp.logical_and(lane >= lo, lane < lo + EXPERTS_PER_GROUP), logits, NEG_BIG)
    v1, i1 = first_argmax(e_logits)
    v2, i2 = first_argmax(jnp.where(lane == i1, NEG_BIG, e_logits))
    e21 = jnp.exp(v2 - v1)
    w1 = g_w / (1.0 + e21)
    w2 = g_w * e21 / (1.0 + e21)
    route_ref[...] = jnp.where(lane == 0, i1 - N_GROUPS,
                               jnp.where(lane == 1, i2 - N_GROUPS,
                                         jnp.where(lane == 2, w1, jnp.where(lane == 3, w2, 0.0))))
    wb_ref[:, :V7X_LANES] = jnp.broadcast_to(w1, (w1.shape[0], V7X_LANES))
    wb_ref[:, V7X_LANES:] = jnp.broadcast_to(w2, (w2.shape[0], V7X_LANES))


def router(x, g, w_router, b_router):
    S, D = x.shape
    W = w_router.shape[1]
    tm = min(NORM_ROWS, S)
    n_chunks = D // V7X_LANES
    return pl.pallas_call(
        _router_kernel,
        out_shape=(jax.ShapeDtypeStruct((S * n_chunks, V7X_LANES), F32), jax.ShapeDtypeStruct((S, W), F32),
                   jax.ShapeDtypeStruct((S, TOP_K * V7X_LANES), F32)),
        grid=(S // tm,),
        in_specs=[pl.BlockSpec((tm, D), lambda i: (i, 0)),
                  pl.BlockSpec((1, D), lambda i: (0, 0)),
                  pl.BlockSpec((D, W), lambda i: (0, 0)),
                  pl.BlockSpec((1, W), lambda i: (0, 0))],
        out_specs=(pl.BlockSpec((tm * n_chunks, V7X_LANES), lambda i: (i, 0)),
                   pl.BlockSpec((tm, W), lambda i: (i, 0)),
                   pl.BlockSpec((tm, TOP_K * V7X_LANES), lambda i: (i, 0))),
        compiler_params=pltpu.CompilerParams(
            dimension_semantics=("parallel",),
            vmem_limit_bytes=_vmem_limit(2 * tm * D * 8 + 2 * D * W * 4 + 4 * tm * D * 4)),
        name="router",
    )(x, g.reshape(1, D), w_router, b_router)


CAST_ROWS = 128
GATHER_UNROLL = 8
WEIGHT_DMA_PRIORITY = 1


ROW_SLOTS = 3


def _expert_kernel(be_ref, first_ref, next_ref, active_ref, tok_ref, tok1_ref, tok2_ref, hn_hbm, wg_hbm, wu_hbm,
                   wd_hbm, o_ref, xbuf, stage_g, stage_u, stage_d, wg_ref, wu_ref, wd_ref, wsem, gsem,
                   *, layer):
    b = pl.program_id(0)
    n_blocks = pl.num_programs(0)
    n_chunks = wg_ref.shape[0] // V7X_LANES
    R = o_ref.shape[0] // n_chunks
    pitch = _gather_pitch(n_chunks)
    slot = lax.rem(b, ROW_SLOTS)
    slot2 = lax.rem(b + 2, ROW_SLOTS)

    def row_copy(idx_ref, s, r):
        src = pl.multiple_of(idx_ref[0, r], n_chunks)
        return pltpu.make_async_copy(hn_hbm.at[pl.ds(src, n_chunks), :],
                                     xbuf.at[s, pl.ds(r * pitch, n_chunks), :], gsem.at[s])

    def start_rows_loop(idx_ref, s):
        def issue(r, carry):
            row_copy(idx_ref, s, r).start()
            return carry
        lax.fori_loop(0, R, issue, 0, unroll=GATHER_UNROLL)

    def wait_rows(s):
        pltpu.make_async_copy(hn_hbm.at[pl.ds(0, R * n_chunks), :], xbuf.at[s, pl.ds(0, R * n_chunks), :],
                              gsem.at[s]).wait()

    def weight_copies(e):
        return (pltpu.make_async_copy(wg_hbm.at[layer, e], stage_g, wsem.at[0]),
                pltpu.make_async_copy(wu_hbm.at[layer, e], stage_u, wsem.at[1]),
                pltpu.make_async_copy(wd_hbm.at[layer, e], stage_d, wsem.at[2]))

    @pl.when(b == 0)
    def _():
        start_rows_loop(tok_ref, 0)
        start_rows_loop(tok1_ref, 1)
        for cp in weight_copies(be_ref[0]):
            cp.start(priority=WEIGHT_DMA_PRIORITY)

    @pl.when(first_ref[b] == 1)
    def _():
        for cp in weight_copies(be_ref[b]):
            cp.wait()
        for stage, dst in ((stage_g, wg_ref), (stage_u, wu_ref), (stage_d, wd_ref)):
            def cast_rows(r, carry, stage=stage, dst=dst):
                rows = pl.ds(pl.multiple_of(r * CAST_ROWS, CAST_ROWS), CAST_ROWS)
                dst[rows, :] = stage[rows, :].astype(BF16)
                return carry
            lax.fori_loop(0, stage.shape[0] // CAST_ROWS, cast_rows, 0)

        @pl.when(next_ref[b] >= 0)
        def _():
            for cp in weight_copies(next_ref[b]):
                cp.start(priority=WEIGHT_DMA_PRIORITY)

    @pl.when(active_ref[b] == 1)
    def _():
        wait_rows(slot)
        x = jnp.concatenate([_load_slab_chunk(xbuf, (slot,), c, R, pitch).astype(BF16)
                             for c in range(n_chunks)], axis=1)
        for r in range(R):
            row_copy(tok2_ref, slot2, r).start()
        a = jnp.dot(x, wg_ref[...], preferred_element_type=F32)
        u = jnp.dot(x, wu_ref[...], preferred_element_type=F32)
        hmid = (a * jax.nn.sigmoid(a) * u).astype(BF16)
        _store_slabs(o_ref, jnp.dot(hmid, wd_ref[...], preferred_element_type=F32))

    @pl.when(active_ref[b] == 0)
    def _():
        @pl.when(jnp.logical_or(b < 2, active_ref[jnp.maximum(b - 2, 0)] == 1))
        def _():
            wait_rows(slot)
        o_ref[...] = jnp.zeros_like(o_ref)

    @pl.when(b == n_blocks - 1)
    def _():
        @pl.when(active_ref[n_blocks - 2] == 1)
        def _():
            wait_rows(lax.rem(b + 1, ROW_SLOTS))

        @pl.when(active_ref[n_blocks - 1] == 1)
        def _():
            wait_rows(slot2)


def expert_blocks(block_e, first, next_e, active, row_tok, hn, w_gate, w_up, w_down, layer):
    n_blocks, _, R = row_tok.shape
    assert n_blocks >= 2
    D, Dh = w_gate.shape[2:]
    n_chunks = D // V7X_LANES
    hbm = pl.BlockSpec(memory_space=pl.ANY)

    def tok_spec(ahead):
        return pl.BlockSpec((None, 1, R), lambda b, *_: (jnp.minimum(b + ahead, n_blocks - 1), 0, 0),
                            memory_space=pltpu.SMEM)

    grid_spec = pltpu.PrefetchScalarGridSpec(
        num_scalar_prefetch=4,
        grid=(n_blocks,),
        in_specs=[tok_spec(0), tok_spec(1), tok_spec(2), hbm, hbm, hbm, hbm],
        out_specs=pl.BlockSpec((R * n_chunks, V7X_LANES), lambda b, *_: (b, 0)),
        scratch_shapes=[pltpu.VMEM((ROW_SLOTS, R * _gather_pitch(n_chunks), V7X_LANES), F32),
                        pltpu.VMEM((D, Dh), F32), pltpu.VMEM((D, Dh), F32), pltpu.VMEM((Dh, D), F32),
                        pltpu.VMEM((D, Dh), BF16), pltpu.VMEM((D, Dh), BF16), pltpu.VMEM((Dh, D), BF16),
                        pltpu.SemaphoreType.DMA((3,)), pltpu.SemaphoreType.DMA((ROW_SLOTS,))],
    )
    return pl.pallas_call(
        functools.partial(_expert_kernel, layer=layer),
        out_shape=jax.ShapeDtypeStruct((n_blocks * R * n_chunks, V7X_LANES), F32),
        grid_spec=grid_spec,
        compiler_params=pltpu.CompilerParams(
            dimension_semantics=("arbitrary",),
            vmem_limit_bytes=_vmem_limit(3 * D * Dh * (4 + 2) + (ROW_SLOTS + 2) * R * D * 4 + R * D * 2
                                         + 6 * R * Dh * 4)),
        name="moe_experts",
    )(block_e, first, next_e, active, row_tok, row_tok, row_tok, hn, w_gate, w_up, w_down)


COMBINE_ROWS = 256


def _combine_kernel(idx_ref, idx_next_ref, x_ref, wb_ref, yb_hbm, o_ref, cbuf, gsem):
    i = pl.program_id(0)
    n_tiles = pl.num_programs(0)
    tm, D = x_ref.shape
    n_chunks = D // V7X_LANES
    pitch = _gather_pitch(n_chunks)
    slot = lax.rem(i, 2)

    def row_copy(idx, s, r, k):
        src = pl.multiple_of(idx[0, TOP_K * r + k], n_chunks)
        return pltpu.make_async_copy(yb_hbm.at[pl.ds(src, n_chunks), :],
                                     cbuf.at[s, k, pl.ds(r * pitch, n_chunks), :], gsem.at[s])

    def wait_rows(s):
        for k in range(TOP_K):
            pltpu.make_async_copy(yb_hbm.at[pl.ds(0, tm * n_chunks), :],
                                  cbuf.at[s, k, pl.ds(0, tm * n_chunks), :], gsem.at[s]).wait()

    @pl.when(i == 0)
    def _():
        def issue(r, carry):
            for k in range(TOP_K):
                row_copy(idx_ref, 0, r, k).start()
            return carry
        lax.fori_loop(0, tm, issue, 0, unroll=GATHER_UNROLL)

    wait_rows(slot)
    for r in range(tm):
        for k in range(TOP_K):
            row_copy(idx_next_ref, 1 - slot, r, k).start(priority=k % 2)
    for c in range(n_chunks):
        cols = slice(c * V7X_LANES, (c + 1) * V7X_LANES)
        acc = x_ref[:, cols]
        for k in range(TOP_K):
            acc = acc + wb_ref[:, k * V7X_LANES:(k + 1) * V7X_LANES] * _load_slab_chunk(cbuf, (slot, k), c, tm, pitch)
        o_ref[:, cols] = acc

    @pl.when(i == n_tiles - 1)
    def _():
        wait_rows(1 - slot)


def combine_residual(x, wb, dest, yb):
    T, D = x.shape
    tm = min(COMBINE_ROWS, T)
    n_tiles = T // tm
    idx = (dest * (D // V7X_LANES)).reshape(n_tiles, 1, tm * TOP_K)
    return pl.pallas_call(
        _combine_kernel,
        out_shape=jax.ShapeDtypeStruct((T, D), F32),
        grid=(n_tiles,),
        in_specs=[pl.BlockSpec((None, 1, tm * TOP_K), lambda i: (i, 0, 0), memory_space=pltpu.SMEM),
                  pl.BlockSpec((None, 1, tm * TOP_K), lambda i: (jnp.minimum(i + 1, n_tiles - 1), 0, 0),
                               memory_space=pltpu.SMEM),
                  pl.BlockSpec((tm, D), lambda i: (i, 0)),
                  pl.BlockSpec((tm, wb.shape[1]), lambda i: (i, 0)),
                  pl.BlockSpec(memory_space=pl.ANY)],
        out_specs=pl.BlockSpec((tm, D), lambda i: (i, 0)),
        scratch_shapes=[pltpu.VMEM((2, TOP_K, tm * _gather_pitch(D // V7X_LANES), V7X_LANES), F32),
                        pltpu.SemaphoreType.DMA((2,))],
        compiler_params=pltpu.CompilerParams(
            dimension_semantics=("arbitrary",),
            vmem_limit_bytes=_vmem_limit(2 * TOP_K * tm * D * 4 + 6 * tm * D * 4)),
        name="moe_combine",
    )(idx, idx, x, wb, yb)


def moe_residual(x, g, rgw, rgb, rew, reb, w_gate, w_up, w_down, layer):
    T, D = x.shape
    R = MOE_ROWS
    assert TOP_K == 2
    pad = V7X_LANES - N_GROUPS - N_EXPERTS
    w_router = jnp.concatenate([rgw, rew, jnp.zeros((D, pad), F32)], axis=1)
    b_router = jnp.concatenate([rgb, reb, jnp.zeros((pad,), F32)]).reshape(1, V7X_LANES)
    hn, route, wb = router(x, g, w_router, b_router)
    expert_id = route[:, :TOP_K].astype(jnp.int32)

    n_assign = T * TOP_K
    flat_e = expert_id.reshape(-1)
    onehot = (flat_e[:, None] == jnp.arange(N_EXPERTS, dtype=jnp.int32)[None, :]).astype(jnp.int32)
    running = jnp.cumsum(onehot, axis=0)
    counts = running[-1]
    rank = jnp.take_along_axis(running, flat_e[:, None], axis=1)[:, 0] - 1
    padded = (counts + R - 1) // R * R
    pends = jnp.cumsum(padded)
    pstarts = pends - padded
    dest = pstarts[flat_e] + rank
    n_blocks = (n_assign + N_EXPERTS * (R - 1) + R - 1) // R
    flat_tok = jnp.repeat(jnp.arange(T, dtype=jnp.int32), TOP_K)
    row_tok = jnp.zeros((n_blocks * R,), jnp.int32).at[dest].set(flat_tok)

    blk_start = jnp.arange(n_blocks, dtype=jnp.int32) * R
    active = blk_start < pends[-1]
    block_e = jnp.minimum(jnp.sum(blk_start[:, None] >= pends[None, :], axis=1), N_EXPERTS - 1).astype(jnp.int32)
    prev_e = jnp.concatenate([jnp.full((1,), -1, jnp.int32), block_e[:-1]])
    first = jnp.logical_and(active, block_e != prev_e)
    later = lax.cummin(jnp.where(first, block_e, N_EXPERTS)[::-1])[::-1]
    next_e = jnp.concatenate([later[1:], jnp.full((1,), N_EXPERTS, jnp.int32)])
    next_e = jnp.where(next_e >= N_EXPERTS, -1, next_e).astype(jnp.int32)

    yb = expert_blocks(block_e, first.astype(jnp.int32), next_e, active.astype(jnp.int32),
                       (row_tok * (D // V7X_LANES)).reshape(n_blocks, 1, R), hn, w_gate, w_up, w_down, layer)
    return combine_residual(x, wb, dest.reshape(T, TOP_K), yb)


def fox_mlstm_residual(xt, norm_g, w, fox_f_bias, fox_q_gain, fox_k_gain, mlstm_i_bias, mlstm_f_bias,
                       mlstm_out_gain, w_out):
    S, D = xt.shape
    H = fox_f_bias.shape[0]
    assert mlstm_i_bias.shape[0] == H and 3 * H <= 32
    fw = H * HEAD_DIM
    o_ff = 3 * fw
    o_mq = o_ff + H
    o_mi = o_mq + 3 * fw
    o_mo = o_mi + 2 * H
    w_main = jnp.concatenate([w[:, :o_ff], w[:, o_mq:o_mi]], axis=1).astype(BF16)
    gate_pad = V7X_LANES - 3 * H
    w_aux = jnp.concatenate([w[:, o_mo:], w[:, o_ff:o_mq], w[:, o_mi:o_mo],
                             jnp.zeros((D, gate_pad), F32)], axis=1).astype(BF16)
    gains = jnp.concatenate([fox_q_gain[None] * (LOG2E * HEAD_DIM ** -0.5), fox_k_gain[None],
                             jnp.zeros((6, HEAD_DIM), F32)], axis=0)
    bias_row = jnp.concatenate([fox_f_bias, mlstm_i_bias, mlstm_f_bias,
                                jnp.zeros((gate_pad,), F32)]).reshape(1, V7X_LANES)

    hn, mo, gates_pre = norm_inproj_aux(xt, norm_g, w_aux, fw)
    proj = inproj_main(hn, w_main, gains)
    gcol = gate_activations(gates_pre, bias_row, H)
    grow = gcol[:, :32].T
    cq = jnp.broadcast_to(grow[:H, :, None], (H, S, V7X_LANES))
    ck = grow[:H].reshape(H, 1, S)
    y_fox = fox_attention(proj, cq, ck, H, gains[0], gains[1])
    y_mlstm = mlstm_mixer(proj, mo, gcol, grow, mlstm_out_gain, H, 3)
    return outproj_residual(y_fox, y_mlstm, w_out.astype(BF16), xt)


def kernel(x, norm_mix, norm_ffn, w_in, fox_f_bias, fox_q_gain, fox_k_gain, mlstm_i_bias, mlstm_f_bias,
           mlstm_out_gain, w_out, pool_w, pool_b, pool_scale, router_group_w, router_group_b,
           router_expert_w, router_expert_b, w_gate, w_up, w_down):
    B, S, D = x.shape
    assert B == 1
    depth = norm_mix.shape[0]
    xt = x.reshape(S, D)

    for layer in range(depth):
        j = layer // 2
        if layer % 2 == 0:
            xt = fox_mlstm_residual(xt, norm_mix[layer], w_in[j], fox_f_bias[j], fox_q_gain[j],
                                    fox_k_gain[j], mlstm_i_bias[j], mlstm_f_bias[j], mlstm_out_gain[j],
                                    w_out[j])
        else:
            xt = pool_mixer_residual(xt, norm_mix[layer], pool_w[j].astype(BF16), pool_b[j], pool_scale[j])
        xt = moe_residual(xt, norm_ffn[layer], router_group_w[layer], router_group_b[layer],
                          router_expert_w[layer], router_expert_b[layer],
                          w_gate, w_up, w_down, layer)
    return xt.reshape(B, S, D)
```

```python
import functools

import jax
import jax.numpy as jnp
from jax import lax
from jax.experimental import pallas as pl
from jax.experimental.pallas import tpu as pltpu

F32 = jnp.float32
BF16 = jnp.bfloat16

HEAD_DIM = 128
GATE_SOFTCAP = 15.0
POOL_WINDOWS = (2, 4, 8, 16)
POOL_HALO = 16
N_GROUPS = 4
EXPERTS_PER_GROUP = 8
N_EXPERTS = N_GROUPS * EXPERTS_PER_GROUP
TOP_K = 2
RMS_EPS = 1e-6

V7X_LANES = 128
V7X_VMEM_BYTES = 64 * 1024 * 1024

NORM_ROWS = 512
MM_ROWS = 1024
MM_COLS = 1024
ATT_Q = 1024
ATT_K = 512
MLSTM_CHUNK = 256
MOE_ROWS = 256
NEG_BIG = -1e30
LOG2E = 1.4426950408889634


def _vmem_limit(nbytes):
    return int(min(max(nbytes * 3 // 2, 16 * 1024 * 1024), V7X_VMEM_BYTES - 8 * 1024 * 1024))


def _rms(x, eps=RMS_EPS):
    return x * lax.rsqrt(jnp.mean(x * x, axis=-1, keepdims=True) + eps)


def _log_sigmoid(x):
    return -(jnp.maximum(-x, 0.0) + jnp.log1p(jnp.exp(-jnp.abs(x))))


def _inproj_main_kernel(a_ref, w_ref, gain_ref, o_ref, *, n_heads_per_tile):
    j = pl.program_id(1)
    acc = jnp.dot(a_ref[...], w_ref[...], preferred_element_type=F32)

    @pl.when(j < 2)
    def _():
        g = gain_ref[pl.ds(j, 1), :]
        for h in range(n_heads_per_tile):
            a = acc[:, h * HEAD_DIM:(h + 1) * HEAD_DIM]
            o_ref[:, h * HEAD_DIM:(h + 1) * HEAD_DIM] = (_rms(a) * g).astype(o_ref.dtype)

    @pl.when(j == 4)
    def _():
        o_ref[...] = (acc * (HEAD_DIM ** -0.5)).astype(o_ref.dtype)

    @pl.when(jnp.logical_and(j >= 2, j != 4))
    def _():
        o_ref[...] = acc.astype(o_ref.dtype)


def inproj_main(hn, w_main, gains):
    S, D = hn.shape
    N = w_main.shape[1]
    tm, tn = min(MM_ROWS, S), MM_COLS
    return pl.pallas_call(
        functools.partial(_inproj_main_kernel, n_heads_per_tile=tn // HEAD_DIM),
        out_shape=jax.ShapeDtypeStruct((S, N), BF16),
        grid=(S // tm, N // tn),
        in_specs=[pl.BlockSpec((tm, D), lambda i, j: (i, 0)),
                  pl.BlockSpec((D, tn), lambda i, j: (0, j)),
                  pl.BlockSpec((8, HEAD_DIM), lambda i, j: (0, 0))],
        out_specs=pl.BlockSpec((tm, tn), lambda i, j: (i, j)),
        compiler_params=pltpu.CompilerParams(
            dimension_semantics=("parallel", "parallel"),
            vmem_limit_bytes=_vmem_limit(2 * (tm * D * 2 + D * tn * 2 + tm * tn * 2) + 2 * tm * tn * 4)),
        name="inproj_main",
    )(hn, w_main, gains)


def _norm_inproj_aux_kernel(x_ref, g_ref, w_ref, hn_ref, mo_ref, gate_ref):
    hn = (_rms(x_ref[...]) * g_ref[...]).astype(BF16)
    hn_ref[...] = hn
    acc = jnp.dot(hn, w_ref[...], preferred_element_type=F32)
    n_mo = mo_ref.shape[1]
    mo_ref[...] = acc[:, :n_mo]
    gate_ref[...] = acc[:, n_mo:]


def norm_inproj_aux(x, g, w_aux, n_mo):
    S, D = x.shape
    N = w_aux.shape[1]
    tm = min(NORM_ROWS, S)
    return pl.pallas_call(
        _norm_inproj_aux_kernel,
        out_shape=(jax.ShapeDtypeStruct((S, D), BF16),
                   jax.ShapeDtypeStruct((S, n_mo), F32),
                   jax.ShapeDtypeStruct((S, N - n_mo), F32)),
        grid=(S // tm,),
        in_specs=[pl.BlockSpec((tm, D), lambda i: (i, 0)),
                  pl.BlockSpec((1, D), lambda i: (0, 0)),
                  pl.BlockSpec((D, N), lambda i: (0, 0))],
        out_specs=(pl.BlockSpec((tm, D), lambda i: (i, 0)),
                   pl.BlockSpec((tm, n_mo), lambda i: (i, 0)),
                   pl.BlockSpec((tm, N - n_mo), lambda i: (i, 0))),
        compiler_params=pltpu.CompilerParams(
            dimension_semantics=("parallel",),
            vmem_limit_bytes=_vmem_limit(2 * (tm * D * 6 + D * N * 2 + tm * N * 4) + tm * N * 4 + 2 * tm * D * 4)),
        name="norm_inproj_aux",
    )(x, g.reshape(1, D), w_aux)


def _split3_dot(tri, val):
    v1 = val.astype(BF16)
    r1 = val - v1.astype(F32)
    v2 = r1.astype(BF16)
    v3 = (r1 - v2.astype(F32)).astype(BF16)
    out = jnp.dot(tri, v1, preferred_element_type=F32)
    out += jnp.dot(tri, v2, preferred_element_type=F32)
    out += jnp.dot(tri, v3, preferred_element_type=F32)
    return out


def _gates_kernel(g_ref, bias_ref, o_ref, carry_ref, *, n_heads):
    @pl.when(pl.program_id(0) == 0)
    def _():
        carry_ref[...] = jnp.zeros_like(carry_ref)

    rows = g_ref.shape[0]
    z = g_ref[...] + bias_ref[...]
    lane = lax.broadcasted_iota(jnp.int32, z.shape, 1)
    capped = GATE_SOFTCAP * jnp.tanh(z / GATE_SOFTCAP)
    is_fox = lane < n_heads
    is_i = jnp.logical_and(lane >= n_heads, lane < 2 * n_heads)
    is_f = jnp.logical_and(lane >= 2 * n_heads, lane < 3 * n_heads)
    logf = jnp.where(is_fox, _log_sigmoid(z), jnp.where(is_f, _log_sigmoid(capped), 0.0))
    r = lax.broadcasted_iota(jnp.int32, (rows, rows), 0)
    c = lax.broadcasted_iota(jnp.int32, (rows, rows), 1)
    tri = jnp.where(r >= c, 1.0, 0.0).astype(BF16)
    cum = _split3_dot(tri, logf)
    glob = cum + carry_ref[...]
    o_ref[...] = jnp.where(is_fox, glob * LOG2E, jnp.where(is_i, capped, cum))
    carry_ref[...] = glob[rows - 1:rows, :]


def gate_activations(gates_pre, bias_row, n_heads):
    S, W = gates_pre.shape
    tb = MLSTM_CHUNK
    return pl.pallas_call(
        functools.partial(_gates_kernel, n_heads=n_heads),
        out_shape=jax.ShapeDtypeStruct((S, W), F32),
        grid=(S // tb,),
        in_specs=[pl.BlockSpec((tb, W), lambda i: (i, 0)),
                  pl.BlockSpec((1, W), lambda i: (0, 0))],
        out_specs=pl.BlockSpec((tb, W), lambda i: (i, 0)),
        scratch_shapes=[pltpu.VMEM((1, W), F32)],
        compiler_params=pltpu.CompilerParams(dimension_semantics=("arbitrary",)),
        name="gate_activations",
    )(gates_pre, bias_row)


def _fox_kernel(first_ref, q_ref, k_ref, v_ref, cq_ref, ck_ref, o_ref, m_ref, acc_ref, s_ref, p_ref, alpha_ref,
                *, n_sub):
    i = pl.program_id(1)
    d = HEAD_DIM
    tk = q_ref.shape[0] // n_sub
    assert n_sub % 2 == 0

    m_ref[...] = jnp.full_like(m_ref, NEG_BIG)
    acc_ref[...] = jnp.zeros_like(acc_ref)
    p_ref[1] = jnp.zeros_like(p_ref[1])
    alpha_ref[1] = jnp.ones_like(alpha_ref[1])
    lane = lax.broadcasted_iota(jnp.int32, (tk, d), 1)
    ones_col = jnp.where(lane == 0, 1.0, 0.0).astype(BF16)
    row = lax.broadcasted_iota(jnp.int32, (tk, tk), 0)
    col = lax.broadcasted_iota(jnp.int32, (tk, tk), 1)
    causal = col <= row

    def qk_stage(sub, j, par):
        start = pl.multiple_of(j * tk, tk)
        s = lax.dot_general(q_ref[pl.ds(sub * tk, tk), :], k_ref[pl.ds(start, tk), :],
                            (((1,), (1,)), ((), ())), preferred_element_type=F32)
        s_ref[par, sub] = s - ck_ref[:, pl.ds(start, tk)]

    def sm_stage(sub, par, masked):
        rows = pl.ds(sub * tk, tk)
        s = s_ref[par, sub]
        if masked:
            s = jnp.where(causal, s, NEG_BIG)
        cq = cq_ref[rows, :]
        m_prev = m_ref[rows, :]
        m_new = jnp.maximum(m_prev, jnp.max(s, axis=-1, keepdims=True) + cq)
        p_ref[par, sub] = jnp.exp2(s - jnp.tile(m_new - cq, (1, tk // V7X_LANES))).astype(BF16)
        alpha_ref[par, rows, :] = jnp.exp2(m_prev - m_new)
        m_ref[rows, :] = m_new

    def pv_stage(sub, j, par):
        rows = pl.ds(sub * tk, tk)
        start = pl.multiple_of(j * tk, tk)
        v_aug = jnp.concatenate([v_ref[pl.ds(start, tk), :], ones_col], axis=1)
        acc_ref[rows, :] = (jnp.tile(alpha_ref[par, rows, :], (1, 2)) * acc_ref[rows, :]
                            + jnp.dot(p_ref[par, sub], v_aug, preferred_element_type=F32))

    n_full = i * n_sub
    first = first_ref[pl.program_id(0) * pl.num_programs(1) + i]
    for sub in range(n_sub):
        qk_stage(sub, first, 0)

    def body(tt, carry):
        for par in (0, 1):
            step = 2 * tt + par
            for sub in range(n_sub):
                qk_stage(sub, step + 1, 1 - par)
                sm_stage(sub, par, masked=False)
                pv_stage(sub, jnp.maximum(step - 1, 0), 1 - par)
        return carry

    lax.fori_loop(first // 2, n_full // 2, body, 0)
    for kk in range(n_sub + 1):
        par = kk % 2
        for sub in range(n_sub):
            if kk + 1 <= sub:
                qk_stage(sub, n_full + kk + 1, 1 - par)
            if kk <= sub:
                sm_stage(sub, par, masked=(kk == sub))
            if kk - 1 <= sub:
                pv_stage(sub, jnp.maximum(n_full + kk - 1, 0), 1 - par)
    acc = acc_ref[...]
    o_ref[...] = (acc[:, :d] / acc[:, d:d + 1]).astype(o_ref.dtype)


UNDERFLOW_LOG2 = 152.0


def _fox_first_blocks(q_gain, k_gain, c2, tk, n_sub):
    H, S = c2.shape
    nb = S // tk
    slack = 1.0 + 2.0 ** -6
    qk_bound = HEAD_DIM * jnp.max(jnp.abs(q_gain)) * jnp.max(jnp.abs(k_gain)) * slack
    cb = c2.reshape(H, nb, tk)
    c_hi, c_lo = jnp.max(cb, axis=-1), jnp.min(cb, axis=-1)
    upper = qk_bound + c_hi[:, :, None] - c_lo[:, None, :]
    lower = -qk_bound
    j_idx = jnp.arange(nb, dtype=jnp.int32)
    needed = jnp.logical_or(upper >= lower - UNDERFLOW_LOG2, j_idx[None, None, :] >= j_idx[None, :, None])
    first = jnp.min(jnp.where(needed, j_idx[None, None, :], nb), axis=-1)
    first = jnp.min(first.reshape(H, nb // n_sub, n_sub), axis=-1)
    return ((first // 2) * 2).reshape(-1).astype(jnp.int32)


def fox_attention(proj, cq, ck, n_heads, q_gain, k_gain):
    S = proj.shape[0]
    tk = min(ATT_K, S)
    n_sub = max(1, min(ATT_Q, S) // tk)
    tq = n_sub * tk
    H = n_heads
    first = _fox_first_blocks(q_gain, k_gain, ck.reshape(H, S), tk, n_sub)
    grid_spec = pltpu.PrefetchScalarGridSpec(
        num_scalar_prefetch=1,
        grid=(H, S // tq),
        in_specs=[pl.BlockSpec((tq, HEAD_DIM), lambda h, i, *_: (i, h)),
                  pl.BlockSpec((S, HEAD_DIM), lambda h, i, *_: (0, H + h)),
                  pl.BlockSpec((S, HEAD_DIM), lambda h, i, *_: (0, 2 * H + h)),
                  pl.BlockSpec((None, tq, V7X_LANES), lambda h, i, *_: (h, i, 0)),
                  pl.BlockSpec((None, 1, S), lambda h, i, *_: (h, 0, 0))],
        out_specs=pl.BlockSpec((tq, HEAD_DIM), lambda h, i, *_: (i, h)),
        scratch_shapes=[pltpu.VMEM((tq, V7X_LANES), F32), pltpu.VMEM((tq, 2 * HEAD_DIM), F32),
                        pltpu.VMEM((2, n_sub, tk, tk), F32), pltpu.VMEM((2, n_sub, tk, tk), BF16),
                        pltpu.VMEM((2, tq, V7X_LANES), F32)],
    )
    return pl.pallas_call(
        functools.partial(_fox_kernel, n_sub=n_sub),
        out_shape=jax.ShapeDtypeStruct((S, H * HEAD_DIM), BF16),
        grid_spec=grid_spec,
        compiler_params=pltpu.CompilerParams(
            dimension_semantics=("parallel", "arbitrary"),
            vmem_limit_bytes=_vmem_limit(4 * S * HEAD_DIM * 2 + 2 * n_sub * tk * tk * (4 + 2)
                                         + 4 * tk * tk * 4 + 12 * tq * HEAD_DIM * 4 + 16 * S * 4)),
        name="fox_attention",
    )(first, proj, proj, proj, cq, ck)


def _mlstm_kernel(q_ref, k_ref, v_ref, mo_ref, gcol_ref, grow_ref, gain_ref, o_ref,
                  state_ref, m_ref, *, n_heads):
    L = q_ref.shape[0]
    d = HEAD_DIM

    @pl.when(pl.program_id(0) == 0)
    def _():
        state_ref[...] = jnp.zeros_like(state_ref)
        m_ref[...] = jnp.zeros_like(m_ref)

    row = lax.broadcasted_iota(jnp.int32, (L, L), 0)
    col = lax.broadcasted_iota(jnp.int32, (L, L), 1)
    causal = col <= row
    lane = lax.broadcasted_iota(jnp.int32, (L, d), 1)
    ones_col = jnp.where(lane == 0, 1.0, 0.0).astype(BF16)

    gcol = gcol_ref[...]
    grow = grow_ref[...]
    for h in range(n_heads):
        sl = slice(h * d, (h + 1) * d)
        q = q_ref[:, sl]
        k = k_ref[:, sl]
        v = v_ref[:, sl]
        i_col = gcol[:, n_heads + h:n_heads + h + 1]
        b_col = gcol[:, 2 * n_heads + h:2 * n_heads + h + 1]
        i_row = grow[n_heads + h:n_heads + h + 1, :]
        b_row = grow[2 * n_heads + h:2 * n_heads + h + 1, :]
        m_prev = m_ref[h:h + 1, 0:1]
        state = state_ref[h]

        log_intra = jnp.where(causal, b_col - b_row + i_row, NEG_BIG)
        log_inter = b_col + m_prev
        m_t = jnp.maximum(log_inter, jnp.max(log_intra, axis=-1, keepdims=True))
        w_intra = jnp.exp(log_intra - m_t)
        w_inter = jnp.exp(log_inter - m_t)
        qk = lax.dot_general(q, k, (((1,), (1,)), ((), ())), preferred_element_type=F32) * w_intra
        v_aug = jnp.concatenate([v, ones_col], axis=1)
        tot = jnp.dot(qk.astype(BF16), v_aug, preferred_element_type=F32)
        tot = tot + w_inter * jnp.dot(q, state.astype(BF16), preferred_element_type=F32)
        num = tot[:, :d]
        den = tot[:, d:d + 1]
        hval = num / jnp.maximum(jnp.abs(den), jnp.exp(-m_t))

        b_last = b_col[L - 1:L, :]
        log_w_state = b_last - b_col + i_col
        m_new = jnp.maximum(b_last + m_prev, jnp.max(log_w_state, axis=0, keepdims=True))
        decay = jnp.exp(b_last + m_prev - m_new)
        w_s = jnp.exp(log_w_state - m_new)
        wv = (w_s * v_aug.astype(F32)).astype(BF16)
        upd = lax.dot_general(k, wv, (((0,), (0,)), ((), ())), preferred_element_type=F32)
        state_ref[h] = decay * state + upd
        m_ref[h:h + 1, :] = jnp.broadcast_to(m_new, (1, m_ref.shape[1]))

        hn = _rms(hval) * gain_ref[:, sl]
        o_ref[:, sl] = (jax.nn.sigmoid(mo_ref[:, sl]) * hn).astype(o_ref.dtype)


def mlstm_mixer(proj, mo, gcol, grow, out_gain, n_heads, q_block):
    S = proj.shape[0]
    L = min(MLSTM_CHUNK, S)
    W = n_heads * HEAD_DIM
    return pl.pallas_call(
        functools.partial(_mlstm_kernel, n_heads=n_heads),
        out_shape=jax.ShapeDtypeStruct((S, W), BF16),
        grid=(S // L,),
        in_specs=[pl.BlockSpec((L, W), lambda c: (c, q_block)),
                  pl.BlockSpec((L, W), lambda c: (c, q_block + 1)),
                  pl.BlockSpec((L, W), lambda c: (c, q_block + 2)),
                  pl.BlockSpec((L, W), lambda c: (c, 0)),
                  pl.BlockSpec((L, gcol.shape[1]), lambda c: (c, 0)),
                  pl.BlockSpec((grow.shape[0], L), lambda c: (0, c)),
                  pl.BlockSpec((1, W), lambda c: (0, 0))],
        out_specs=pl.BlockSpec((L, W), lambda c: (c, 0)),
        scratch_shapes=[pltpu.VMEM((n_heads, HEAD_DIM, 2 * HEAD_DIM), F32),
                        pltpu.VMEM((n_heads, V7X_LANES), F32)],
        compiler_params=pltpu.CompilerParams(
            dimension_semantics=("arbitrary",),
            vmem_limit_bytes=_vmem_limit(2 * L * W * (3 * 2 + 4 + 2) + 16 * L * L * 4)),
        name="mlstm",
    )(proj, proj, proj, mo, gcol, grow, out_gain.reshape(1, W))


def _outproj_kernel(a1_ref, a2_ref, w_ref, x_ref, o_ref):
    k1 = a1_ref.shape[1]
    acc = jnp.dot(a1_ref[...], w_ref[:k1, :], preferred_element_type=F32)
    acc += jnp.dot(a2_ref[...], w_ref[k1:, :], preferred_element_type=F32)
    o_ref[...] = x_ref[...] + acc


def outproj_residual(a1, a2, w, x):
    S, K1 = a1.shape
    K2 = a2.shape[1]
    N = w.shape[1]
    tm, tn = min(MM_ROWS, S), MM_COLS
    return pl.pallas_call(
        _outproj_kernel,
        out_shape=jax.ShapeDtypeStruct((S, N), F32),
        grid=(S // tm, N // tn),
        in_specs=[pl.BlockSpec((tm, K1), lambda i, j: (i, 0)),
                  pl.BlockSpec((tm, K2), lambda i, j: (i, 0)),
                  pl.BlockSpec((K1 + K2, tn), lambda i, j: (0, j)),
                  pl.BlockSpec((tm, tn), lambda i, j: (i, j))],
        out_specs=pl.BlockSpec((tm, tn), lambda i, j: (i, j)),
        compiler_params=pltpu.CompilerParams(
            dimension_semantics=("parallel", "parallel"),
            vmem_limit_bytes=_vmem_limit(2 * (tm * (K1 + K2) * 2 + (K1 + K2) * tn * 2 + 2 * tm * tn * 4)
                                         + tm * tn * 4)),
        name="outproj",
    )(a1, a2, w, x)


def _pool_kernel(x_ref, g_ref, w_ref, b_ref, scale_ref, o_ref, carry_ref):
    i = pl.program_id(0)
    tm = x_ref.shape[0]
    gw = w_ref.shape[1]

    @pl.when(i == 0)
    def _():
        carry_ref[...] = jnp.zeros_like(carry_ref)

    x = x_ref[...]
    hn = _rms(x) * g_ref[...]
    t = i * tm + lax.broadcasted_iota(jnp.int32, (tm, 1), 0)
    for g, w in enumerate(POOL_WINDOWS):
        sl = slice(g * gw, (g + 1) * gw)
        hg = hn[:, sl]
        cur = jnp.concatenate([carry_ref[:, sl], hg], axis=0)
        k = 1
        while k < w:
            cur = cur + pltpu.roll(cur, k, axis=0)
            k *= 2
        window_sum = cur[POOL_HALO:, :]
        count = jnp.minimum(t + 1, w).astype(F32)
        pooled = window_sum / count - hg
        y = jnp.dot(pooled.astype(BF16), w_ref[g], preferred_element_type=F32) + b_ref[:, sl]
        o_ref[:, sl] = x[:, sl] + y * scale_ref[:, sl]
    carry_ref[...] = hn[tm - POOL_HALO:, :]


def pool_mixer_residual(x, g, pool_w, pool_b, pool_scale):
    S, D = x.shape
    tm = min(NORM_ROWS, S)
    G, gw, _ = pool_w.shape
    return pl.pallas_call(
        _pool_kernel,
        out_shape=jax.ShapeDtypeStruct((S, D), F32),
        grid=(S // tm,),
        in_specs=[pl.BlockSpec((tm, D), lambda i: (i, 0)),
                  pl.BlockSpec((1, D), lambda i: (0, 0)),
                  pl.BlockSpec((G, gw, gw), lambda i: (0, 0, 0)),
                  pl.BlockSpec((1, D), lambda i: (0, 0)),
                  pl.BlockSpec((1, D), lambda i: (0, 0))],
        out_specs=pl.BlockSpec((tm, D), lambda i: (i, 0)),
        scratch_shapes=[pltpu.VMEM((POOL_HALO, D), F32)],
        compiler_params=pltpu.CompilerParams(
            dimension_semantics=("arbitrary",),
            vmem_limit_bytes=_vmem_limit(4 * tm * D * 4 + 2 * G * gw * gw * 2 + 6 * tm * D * 4)),
        name="pool_mixer",
    )(x, g.reshape(1, D), pool_w, pool_b.reshape(1, D), pool_scale.reshape(1, D))


def _store_slabs(ref, val):
    n = val.shape[0]
    n_chunks = val.shape[1] // V7X_LANES
    for c in range(n_chunks):
        ref[pl.ds(c, n, stride=n_chunks), :] = val[:, c * V7X_LANES:(c + 1) * V7X_LANES]


def _load_slab_chunk(ref, lead, c, n, pitch):
    return ref[lead + (pl.ds(c, n, stride=pitch), slice(None))]


def _gather_pitch(n_chunks):
    return n_chunks + 4 if n_chunks % 8 == 0 else n_chunks


def _dot_bf16x3(a, b):
    a_hi = a.astype(BF16)
    a_lo = (a - a_hi.astype(F32)).astype(BF16)
    b_hi = b.astype(BF16)
    b_lo = (b - b_hi.astype(F32)).astype(BF16)
    out = jnp.dot(a_hi, b_hi, preferred_element_type=F32)
    out += jnp.dot(a_hi, b_lo, preferred_element_type=F32)
    out += jnp.dot(a_lo, b_hi, preferred_element_type=F32)
    return out


def _router_kernel(x_ref, g_ref, w_ref, b_ref, hn_ref, route_ref, wb_ref):
    hn = _rms(x_ref[...]) * g_ref[...]
    _store_slabs(hn_ref, hn)
    logits = _dot_bf16x3(hn, w_ref[...]) + b_ref[...]
    lane = lax.broadcasted_iota(jnp.int32, logits.shape, 1).astype(F32)
    n_lanes = float(logits.shape[1])

    def first_argmax(vals):
        top = jnp.max(vals, axis=-1, keepdims=True)
        return top, jnp.min(jnp.where(vals == top, lane, n_lanes), axis=-1, keepdims=True)

    is_group = lane < N_GROUPS
    g_top, g_sel = first_argmax(jnp.where(is_group, logits, NEG_BIG))
    g_w = 1.0 / jnp.sum(jnp.where(is_group, jnp.exp(logits - g_top), 0.0), axis=-1, keepdims=True)
    lo = N_GROUPS + EXPERTS_PER_GROUP * g_sel
    e_logits = jnp.where(jnp.logical_and(lane >= lo, lane < lo + EXPERTS_PER_GROUP), logits, NEG_BIG)
    v1, i1 = first_argmax(e_logits)
    v2, i2 = first_argmax(jnp.where(lane == i1, NEG_BIG, e_logits))
    e21 = jnp.exp(v2 - v1)
    w1 = g_w / (1.0 + e21)
    w2 = g_w * e21 / (1.0 + e21)
    route_ref[...] = jnp.where(lane == 0, i1 - N_GROUPS,
                               jnp.where(lane == 1, i2 - N_GROUPS,
                                         jnp.where(lane == 2, w1, jnp.where(lane == 3, w2, 0.0))))
    wb_ref[:, :V7X_LANES] = jnp.broadcast_to(w1, (w1.shape[0], V7X_LANES))
    wb_ref[:, V7X_LANES:] = jnp.broadcast_to(w2, (w2.shape[0], V7X_LANES))


def router(x, g, w_router, b_router):
    S, D = x.shape
    W = w_router.shape[1]
    tm = min(NORM_ROWS, S)
    n_chunks = D // V7X_LANES
    return pl.pallas_call(
        _router_kernel,
        out_shape=(jax.ShapeDtypeStruct((S * n_chunks, V7X_LANES), F32), jax.ShapeDtypeStruct((S, W), F32),
                   jax.ShapeDtypeStruct((S, TOP_K * V7X_LANES), F32)),
        grid=(S // tm,),
        in_specs=[pl.BlockSpec((tm, D), lambda i: (i, 0)),
                  pl.BlockSpec((1, D), lambda i: (0, 0)),
                  pl.BlockSpec((D, W), lambda i: (0, 0)),
                  pl.BlockSpec((1, W), lambda i: (0, 0))],
        out_specs=(pl.BlockSpec((tm * n_chunks, V7X_LANES), lambda i: (i, 0)),
                   pl.BlockSpec((tm, W), lambda i: (i, 0)),
                   pl.BlockSpec((tm, TOP_K * V7X_LANES), lambda i: (i, 0))),
        compiler_params=pltpu.CompilerParams(
            dimension_semantics=("parallel",),
            vmem_limit_bytes=_vmem_limit(2 * tm * D * 8 + 2 * D * W * 4 + 4 * tm * D * 4)),
        name="router",
    )(x, g.reshape(1, D), w_router, b_router)


CAST_ROWS = 128
GATHER_UNROLL = 8
WEIGHT_DMA_PRIORITY = 1


ROW_SLOTS = 3


def _expert_kernel(be_ref, first_ref, next_ref, active_ref, tok_ref, tok1_ref, tok2_ref, hn_hbm, wg_hbm, wu_hbm,
                   wd_hbm, o_ref, xbuf, stage_g, stage_u, stage_d, wg_ref, wu_ref, wd_ref, wsem, gsem,
                   *, layer):
    b = pl.program_id(0)
    n_blocks = pl.num_programs(0)
    n_chunks = wg_ref.shape[0] // V7X_LANES
    R = o_ref.shape[0] // n_chunks
    pitch = _gather_pitch(n_chunks)
    slot = lax.rem(b, ROW_SLOTS)
    slot2 = lax.rem(b + 2, ROW_SLOTS)

    def row_copy(idx_ref, s, r):
        src = pl.multiple_of(idx_ref[0, r], n_chunks)
        return pltpu.make_async_copy(hn_hbm.at[pl.ds(src, n_chunks), :],
                                     xbuf.at[s, pl.ds(r * pitch, n_chunks), :], gsem.at[s])

    def start_rows_loop(idx_ref, s):
        def issue(r, carry):
            row_copy(idx_ref, s, r).start()
            return carry
        lax.fori_loop(0, R, issue, 0, unroll=GATHER_UNROLL)

    def wait_rows(s):
        pltpu.make_async_copy(hn_hbm.at[pl.ds(0, R * n_chunks), :], xbuf.at[s, pl.ds(0, R * n_chunks), :],
                              gsem.at[s]).wait()

    def weight_copies(e):
        return (pltpu.make_async_copy(wg_hbm.at[layer, e], stage_g, wsem.at[0]),
                pltpu.make_async_copy(wu_hbm.at[layer, e], stage_u, wsem.at[1]),
                pltpu.make_async_copy(wd_hbm.at[layer, e], stage_d, wsem.at[2]))

    @pl.when(b == 0)
    def _():
        start_rows_loop(tok_ref, 0)
        start_rows_loop(tok1_ref, 1)
        for cp in weight_copies(be_ref[0]):
            cp.start(priority=WEIGHT_DMA_PRIORITY)

    @pl.when(first_ref[b] == 1)
    def _():
        for cp in weight_copies(be_ref[b]):
            cp.wait()
        for stage, dst in ((stage_g, wg_ref), (stage_u, wu_ref), (stage_d, wd_ref)):
            def cast_rows(r, carry, stage=stage, dst=dst):
                rows = pl.ds(pl.multiple_of(r * CAST_ROWS, CAST_ROWS), CAST_ROWS)
                dst[rows, :] = stage[rows, :].astype(BF16)
                return carry
            lax.fori_loop(0, stage.shape[0] // CAST_ROWS, cast_rows, 0)

        @pl.when(next_ref[b] >= 0)
        def _():
            for cp in weight_copies(next_ref[b]):
                cp.start(priority=WEIGHT_DMA_PRIORITY)

    @pl.when(active_ref[b] == 1)
    def _():
        wait_rows(slot)
        x = jnp.concatenate([_load_slab_chunk(xbuf, (slot,), c, R, pitch).astype(BF16)
                             for c in range(n_chunks)], axis=1)
        for r in range(R):
            row_copy(tok2_ref, slot2, r).start()
        a = jnp.dot(x, wg_ref[...], preferred_element_type=F32)
        u = jnp.dot(x, wu_ref[...], preferred_element_type=F32)
        hmid = (a * jax.nn.sigmoid(a) * u).astype(BF16)
        _store_slabs(o_ref, jnp.dot(hmid, wd_ref[...], preferred_element_type=F32))

    @pl.when(active_ref[b] == 0)
    def _():
        @pl.when(jnp.logical_or(b < 2, active_ref[jnp.maximum(b - 2, 0)] == 1))
        def _():
            wait_rows(slot)
        o_ref[...] = jnp.zeros_like(o_ref)

    @pl.when(b == n_blocks - 1)
    def _():
        @pl.when(active_ref[n_blocks - 2] == 1)
        def _():
            wait_rows(lax.rem(b + 1, ROW_SLOTS))

        @pl.when(active_ref[n_blocks - 1] == 1)
        def _():
            wait_rows(slot2)


def expert_blocks(block_e, first, next_e, active, row_tok, hn, w_gate, w_up, w_down, layer):
    n_blocks, _, R = row_tok.shape
    assert n_blocks >= 2
    D, Dh = w_gate.shape[2:]
    n_chunks = D // V7X_LANES
    hbm = pl.BlockSpec(memory_space=pl.ANY)

    def tok_spec(ahead):
        return pl.BlockSpec((None, 1, R), lambda b, *_: (jnp.minimum(b + ahead, n_blocks - 1), 0, 0),
                            memory_space=pltpu.SMEM)

    grid_spec = pltpu.PrefetchScalarGridSpec(
        num_scalar_prefetch=4,
        grid=(n_blocks,),
        in_specs=[tok_spec(0), tok_spec(1), tok_spec(2), hbm, hbm, hbm, hbm],
        out_specs=pl.BlockSpec((R * n_chunks, V7X_LANES), lambda b, *_: (b, 0)),
        scratch_shapes=[pltpu.VMEM((ROW_SLOTS, R * _gather_pitch(n_chunks), V7X_LANES), F32),
                        pltpu.VMEM((D, Dh), F32), pltpu.VMEM((D, Dh), F32), pltpu.VMEM((Dh, D), F32),
                        pltpu.VMEM((D, Dh), BF16), pltpu.VMEM((D, Dh), BF16), pltpu.VMEM((Dh, D), BF16),
                        pltpu.SemaphoreType.DMA((3,)), pltpu.SemaphoreType.DMA((ROW_SLOTS,))],
    )
    return pl.pallas_call(
        functools.partial(_expert_kernel, layer=layer),
        out_shape=jax.ShapeDtypeStruct((n_blocks * R * n_chunks, V7X_LANES), F32),
        grid_spec=grid_spec,
        compiler_params=pltpu.CompilerParams(
            dimension_semantics=("arbitrary",),
            vmem_limit_bytes=_vmem_limit(3 * D * Dh * (4 + 2) + (ROW_SLOTS + 2) * R * D * 4 + R * D * 2
                                         + 6 * R * Dh * 4)),
        name="moe_experts",
    )(block_e, first, next_e, active, row_tok, row_tok, row_tok, hn, w_gate, w_up, w_down)


COMBINE_ROWS = 256


def _combine_kernel(idx_ref, idx_next_ref, x_ref, wb_ref, yb_hbm, o_ref, cbuf, gsem):
    i = pl.program_id(0)
    n_tiles = pl.num_programs(0)
    tm, D = x_ref.shape
    n_chunks = D // V7X_LANES
    pitch = _gather_pitch(n_chunks)
    slot = lax.rem(i, 2)

    def row_copy(idx, s, r, k):
        src = pl.multiple_of(idx[0, TOP_K * r + k], n_chunks)
        return pltpu.make_async_copy(yb_hbm.at[pl.ds(src, n_chunks), :],
                                     cbuf.at[s, k, pl.ds(r * pitch, n_chunks), :], gsem.at[s])

    def wait_rows(s):
        for k in range(TOP_K):
            pltpu.make_async_copy(yb_hbm.at[pl.ds(0, tm * n_chunks), :],
                                  cbuf.at[s, k, pl.ds(0, tm * n_chunks), :], gsem.at[s]).wait()

    @pl.when(i == 0)
    def _():
        def issue(r, carry):
            for k in range(TOP_K):
                row_copy(idx_ref, 0, r, k).start()
            return carry
        lax.fori_loop(0, tm, issue, 0, unroll=GATHER_UNROLL)

    wait_rows(slot)
    for r in range(tm):
        for k in range(TOP_K):
            row_copy(idx_next_ref, 1 - slot, r, k).start(priority=k % 2)
    for c in range(n_chunks):
        cols = slice(c * V7X_LANES, (c + 1) * V7X_LANES)
        acc = x_ref[:, cols]
        for k in range(TOP_K):
            acc = acc + wb_ref[:, k * V7X_LANES:(k + 1) * V7X_LANES] * _load_slab_chunk(cbuf, (slot, k), c, tm, pitch)
        o_ref[:, cols] = acc

    @pl.when(i == n_tiles - 1)
    def _():
        wait_rows(1 - slot)


def combine_residual(x, wb, dest, yb):
    T, D = x.shape
    tm = min(COMBINE_ROWS, T)
    n_tiles = T // tm
    idx = (dest * (D // V7X_LANES)).reshape(n_tiles, 1, tm * TOP_K)
    return pl.pallas_call(
        _combine_kernel,
        out_shape=jax.ShapeDtypeStruct((T, D), F32),
        grid=(n_tiles,),
        in_specs=[pl.BlockSpec((None, 1, tm * TOP_K), lambda i: (i, 0, 0), memory_space=pltpu.SMEM),
                  pl.BlockSpec((None, 1, tm * TOP_K), lambda i: (jnp.minimum(i + 1, n_tiles - 1), 0, 0),
                               memory_space=pltpu.SMEM),
                  pl.BlockSpec((tm, D), lambda i: (i, 0)),
                  pl.BlockSpec((tm, wb.shape[1]), lambda i: (i, 0)),
                  pl.BlockSpec(memory_space=pl.ANY)],
        out_specs=pl.BlockSpec((tm, D), lambda i: (i, 0)),
        scratch_shapes=[pltpu.VMEM((2, TOP_K, tm * _gather_pitch(D // V7X_LANES), V7X_LANES), F32),
                        pltpu.SemaphoreType.DMA((2,))],
        compiler_params=pltpu.CompilerParams(
            dimension_semantics=("arbitrary",),
            vmem_limit_bytes=_vmem_limit(2 * TOP_K * tm * D * 4 + 6 * tm * D * 4)),
        name="moe_combine",
    )(idx, idx, x, wb, yb)


def moe_residual(x, g, rgw, rgb, rew, reb, w_gate, w_up, w_down, layer):
    T, D = x.shape
    R = MOE_ROWS
    assert TOP_K == 2
    pad = V7X_LANES - N_GROUPS - N_EXPERTS
    w_router = jnp.concatenate([rgw, rew, jnp.zeros((D, pad), F32)], axis=1)
    b_router = jnp.concatenate([rgb, reb, jnp.zeros((pad,), F32)]).reshape(1, V7X_LANES)
    hn, route, wb = router(x, g, w_router, b_router)
    expert_id = route[:, :TOP_K].astype(jnp.int32)

    n_assign = T * TOP_K
    flat_e = expert_id.reshape(-1)
    onehot = (flat_e[:, None] == jnp.arange(N_EXPERTS, dtype=jnp.int32)[None, :]).astype(jnp.int32)
    running = jnp.cumsum(onehot, axis=0)
    counts = running[-1]
    rank = jnp.take_along_axis(running, flat_e[:, None], axis=1)[:, 0] - 1
    padded = (counts + R - 1) // R * R
    pends = jnp.cumsum(padded)
    pstarts = pends - padded
    dest = pstarts[flat_e] + rank
    n_blocks = (n_assign + N_EXPERTS * (R - 1) + R - 1) // R
    flat_tok = jnp.repeat(jnp.arange(T, dtype=jnp.int32), TOP_K)
    pad_tok = jnp.arange(n_blocks * R, dtype=jnp.int32) % T
    row_tok = pad_tok.at[dest].set(flat_tok)

    blk_start = jnp.arange(n_blocks, dtype=jnp.int32) * R
    active = blk_start < pends[-1]
    block_e = jnp.minimum(jnp.sum(blk_start[:, None] >= pends[None, :], axis=1), N_EXPERTS - 1).astype(jnp.int32)
    prev_e = jnp.concatenate([jnp.full((1,), -1, jnp.int32), block_e[:-1]])
    first = jnp.logical_and(active, block_e != prev_e)
    later = lax.cummin(jnp.where(first, block_e, N_EXPERTS)[::-1])[::-1]
    next_e = jnp.concatenate([later[1:], jnp.full((1,), N_EXPERTS, jnp.int32)])
    next_e = jnp.where(next_e >= N_EXPERTS, -1, next_e).astype(jnp.int32)

    yb = expert_blocks(block_e, first.astype(jnp.int32), next_e, active.astype(jnp.int32),
                       (row_tok * (D // V7X_LANES)).reshape(n_blocks, 1, R), hn, w_gate, w_up, w_down, layer)
    return combine_residual(x, wb, dest.reshape(T, TOP_K), yb)


def fox_mlstm_residual(xt, norm_g, w, fox_f_bias, fox_q_gain, fox_k_gain, mlstm_i_bias, mlstm_f_bias,
                       mlstm_out_gain, w_out):
    S, D = xt.shape
    H = fox_f_bias.shape[0]
    assert mlstm_i_bias.shape[0] == H and 3 * H <= 32
    fw = H * HEAD_DIM
    o_ff = 3 * fw
    o_mq = o_ff + H
    o_mi = o_mq + 3 * fw
    o_mo = o_mi + 2 * H
    w_main = jnp.concatenate([w[:, :o_ff], w[:, o_mq:o_mi]], axis=1).astype(BF16)
    gate_pad = V7X_LANES - 3 * H
    w_aux = jnp.concatenate([w[:, o_mo:], w[:, o_ff:o_mq], w[:, o_mi:o_mo],
                             jnp.zeros((D, gate_pad), F32)], axis=1).astype(BF16)
    gains = jnp.concatenate([fox_q_gain[None] * (LOG2E * HEAD_DIM ** -0.5), fox_k_gain[None],
                             jnp.zeros((6, HEAD_DIM), F32)], axis=0)
    bias_row = jnp.concatenate([fox_f_bias, mlstm_i_bias, mlstm_f_bias,
                                jnp.zeros((gate_pad,), F32)]).reshape(1, V7X_LANES)

    hn, mo, gates_pre = norm_inproj_aux(xt, norm_g, w_aux, fw)
    proj = inproj_main(hn, w_main, gains)
    gcol = gate_activations(gates_pre, bias_row, H)
    grow = gcol[:, :32].T
    cq = jnp.broadcast_to(grow[:H, :, None], (H, S, V7X_LANES))
    ck = grow[:H].reshape(H, 1, S)
    y_fox = fox_attention(proj, cq, ck, H, gains[0], gains[1])
    y_mlstm = mlstm_mixer(proj, mo, gcol, grow, mlstm_out_gain, H, 3)
    return outproj_residual(y_fox, y_mlstm, w_out.astype(BF16), xt)


def kernel(x, norm_mix, norm_ffn, w_in, fox_f_bias, fox_q_gain, fox_k_gain, mlstm_i_bias, mlstm_f_bias,
           mlstm_out_gain, w_out, pool_w, pool_b, pool_scale, router_group_w, router_group_b,
           router_expert_w, router_expert_b, w_gate, w_up, w_down):
    B, S, D = x.shape
    assert B == 1
    depth = norm_mix.shape[0]
    xt = x.reshape(S, D)

    for layer in range(depth):
        j = layer // 2
        if layer % 2 == 0:
            xt = fox_mlstm_residual(xt, norm_mix[layer], w_in[j], fox_f_bias[j], fox_q_gain[j],
                                    fox_k_gain[j], mlstm_i_bias[j], mlstm_f_bias[j], mlstm_out_gain[j],
                                    w_out[j])
        else:
            xt = pool_mixer_residual(xt, norm_mix[layer], pool_w[j].astype(BF16), pool_b[j], pool_scale[j])
        xt = moe_residual(xt, norm_ffn[layer], router_group_w[layer], router_group_b[layer],
                          router_expert_w[layer], router_expert_b[layer],
                          w_gate, w_up, w_down, layer)
    return xt.reshape(B, S, D)
```
